```python
import jax, jax.numpy as jnp
from jax import lax
import numpy as np

D_MODEL = 1024
BATCH = 16
SEQ = 2048
DEPTH = 1

N_Q_HEADS = 8
N_KV_HEADS = 2
HEAD_DIM = 128
Q_GROUP = N_Q_HEADS // N_KV_HEADS
ROPE_THETA = 10000.0
Q_BLOCK = 128
GRID_W = 64
LRU_WIDTH = 1024
LRU_BLOCKS = 8
LRU_BLOCK_DIM = LRU_WIDTH // LRU_BLOCKS
CONV_WIDTH = 4
LRU_C = 8.0
N_EXPERTS = 32
TOP_K = 4
D_EXPERT = 1024
SWIGLU_LIMIT = 7.0
SWIGLU_ALPHA = 1.702
MOE_BLOCK = 256
PLE_DIM = 256
EPS = 1e-6

Q_W = N_Q_HEADS * HEAD_DIM
KV_W = N_KV_HEADS * HEAD_DIM
IN_SPLITS = (Q_W, KV_W, KV_W, LRU_WIDTH, LRU_WIDTH, D_MODEL, D_MODEL)
IN_WIDTH = Q_W + 2 * KV_W + 2 * LRU_WIDTH + 2 * D_MODEL

kernel_name = "hybrid_gqa_rglru_moe_ple_encoder"


def rms_norm(x, g):
    xf = x.astype(jnp.float32)
    y = xf * lax.rsqrt(jnp.mean(xf * xf, axis=-1, keepdims=True) + EPS)
    return (y * g.astype(jnp.float32)).astype(x.dtype)


def axial_rope_tables(seq_len):
    rows = seq_len // GRID_W
    row_ids = jnp.repeat(jnp.arange(rows), GRID_W).astype(jnp.float32)
    col_ids = jnp.tile(jnp.arange(GRID_W), rows).astype(jnp.float32)
    axis_dim = HEAD_DIM // 2
    inv_freq = ROPE_THETA ** (-jnp.arange(0, axis_dim, 2, dtype=jnp.float32) / axis_dim)
    ang_r = row_ids[:, None] * inv_freq[None, :]
    ang_c = col_ids[:, None] * inv_freq[None, :]
    return (jnp.cos(ang_r), jnp.sin(ang_r), jnp.cos(ang_c), jnp.sin(ang_c))


def rope_rotate_half(x, cos, sin):
    x1, x2 = jnp.split(x, 2, axis=-1)
    c = cos[:, None, :]
    s = sin[:, None, :]
    return jnp.concatenate([x1 * c - x2 * s, x2 * c + x1 * s], axis=-1).astype(x.dtype)


def apply_axial_rope(x, tables):
    cr, sr, cc, sc = tables
    x_row, x_col = jnp.split(x, 2, axis=-1)
    return jnp.concatenate([rope_rotate_half(x_row, cr, sr), rope_rotate_half(x_col, cc, sc)], axis=-1)


def blocked_gqa(q, k, v):
    B, S = q.shape[0], q.shape[1]
    n_blk = S // Q_BLOCK
    qb = q.reshape(B, n_blk, Q_BLOCK, N_KV_HEADS, Q_GROUP, HEAD_DIM).transpose(1, 0, 2, 3, 4, 5)
    scale = HEAD_DIM ** -0.5

    def one_block(q_blk):
        s = jnp.einsum('bqkgd,bskd->bkgqs', q_blk, k, preferred_element_type=jnp.float32) * scale
        pr = jax.nn.softmax(s, axis=-1)
        return jnp.einsum('bkgqs,bskd->bqkgd', pr.astype(v.dtype), v)

    ob = lax.map(one_block, qb)
    return ob.transpose(1, 0, 2, 3, 4, 5).reshape(B, S, Q_W)


def centred_depthwise_conv(x, w, b):
    left = CONV_WIDTH // 2
    right = CONV_WIDTH - 1 - left
    y = lax.conv_general_dilated(
        x, w[:, None, :].astype(x.dtype), window_strides=(1,), padding=[(left, right)],
        dimension_numbers=('NWC', 'WIO', 'NWC'), feature_group_count=x.shape[-1])
    return y + b


def block_diag_linear(x, w, b):
    B, S, C = x.shape
    xb = x.reshape(B, S, LRU_BLOCKS, LRU_BLOCK_DIM)
    return jnp.einsum('bsnd,nde->bsne', xb, w).reshape(B, S, C) + b


def linear_recurrence(a, b):
    def combine(c1, c2):
        a1, b1 = c1
        a2, b2 = c2
        return a1 * a2, a2 * b1 + b2
    _, h = lax.associative_scan(combine, (a, b), axis=1)
    return h


def rg_lru(x, w_a, b_a, w_i, b_i, lam, reverse):
    xf = x.astype(jnp.float32)
    r = jax.nn.sigmoid(block_diag_linear(x, w_a, b_a).astype(jnp.float32))
    i = jax.nn.sigmoid(block_diag_linear(x, w_i, b_i).astype(jnp.float32))
    log_a = -LRU_C * r * jax.nn.softplus(-lam.astype(jnp.float32))
    a = jnp.exp(log_a)
    mult = jnp.sqrt(-jnp.expm1(2.0 * log_a))
    bt = mult * i * xf
    if reverse:
        h = jnp.flip(linear_recurrence(jnp.flip(a, 1), jnp.flip(bt, 1)), 1)
    else:
        h = linear_recurrence(a, bt)
    return h.astype(x.dtype)


def hybrid_mixer(h, rope, g_mix, w_in, q_norm, k_norm, conv_w, conv_b,
                 lru_wa, lru_ba, lru_wi, lru_bi, lru_lam, w_attn_br, w_lru_br, w_out):
    B, S, _ = h.shape
    u = rms_norm(h, g_mix)
    z = u @ w_in
    points = np.cumsum(IN_SPLITS)[:-1].tolist()
    q, k, v, xr, xg, ga, gr = jnp.split(z, points, axis=-1)
    q = q.reshape(B, S, N_Q_HEADS, HEAD_DIM)
    k = k.reshape(B, S, N_KV_HEADS, HEAD_DIM)
    v = v.reshape(B, S, N_KV_HEADS, HEAD_DIM)
    q = apply_axial_rope(rms_norm(q, q_norm), rope)
    k = apply_axial_rope(rms_norm(k, k_norm), rope)
    y_attn = blocked_gqa(q, k, v) @ w_attn_br
    c = centred_depthwise_conv(xr, conv_w, conv_b)
    h_fwd = rg_lru(c, lru_wa[0], lru_ba[0], lru_wi[0], lru_bi[0], lru_lam[0], reverse=False)
    h_bwd = rg_lru(c, lru_wa[1], lru_ba[1], lru_wi[1], lru_bi[1], lru_lam[1], reverse=True)
    y_lru = ((h_fwd + h_bwd) * jax.nn.gelu(xg, approximate=True)) @ w_lru_br
    merged = jax.nn.sigmoid(ga) * y_attn + jax.nn.sigmoid(gr) * y_lru
    return merged @ w_out


def moe_ffn(h, g_moe, w_router, b_router, w_gu, b_gu, w_dn, b_dn):
    B, S, D = h.shape
    T = B * S
    TK = T * TOP_K
    u = rms_norm(h, g_moe).reshape(T, D)
    logits = (u @ w_router + b_router).astype(jnp.float32)
    top_val, top_idx = lax.top_k(logits, TOP_K)
    gates = jax.nn.softmax(top_val, axis=-1)
    e_flat = top_idx.reshape(-1).astype(jnp.int32)
    tok_flat = jnp.arange(TK, dtype=jnp.int32) // TOP_K
    g_flat = gates.reshape(-1)
    order = jnp.argsort(e_flat)
    e_sorted = e_flat[order]
    counts = jnp.zeros((N_EXPERTS,), jnp.int32).at[e_flat].add(1)
    padded = ((counts + MOE_BLOCK - 1) // MOE_BLOCK) * MOE_BLOCK
    start = jnp.cumsum(counts) - counts
    pad_end = jnp.cumsum(padded)
    pad_start = pad_end - padded
    dest = pad_start[e_sorted] + (jnp.arange(TK, dtype=jnp.int32) - start[e_sorted])
    cap = TK + N_EXPERTS * MOE_BLOCK
    n_blocks = cap // MOE_BLOCK
    slot_tok = jnp.zeros((cap,), jnp.int32).at[dest].set(tok_flat[order])
    slot_gate = jnp.zeros((cap,), jnp.float32).at[dest].set(g_flat[order])
    blk_start = jnp.arange(n_blocks, dtype=jnp.int32) * MOE_BLOCK
    blk_exp = jnp.minimum(jnp.sum(blk_start[:, None] >= pad_end[None, :], axis=1), N_EXPERTS - 1)
    xs = u[slot_tok].reshape(n_blocks, MOE_BLOCK, D)

    def expert_block(args):
        xb, e = args
        hu = xb @ w_gu[e] + b_gu[e]
        gate, up = jnp.split(hu, 2, axis=-1)
        gate = jnp.minimum(gate, SWIGLU_LIMIT)
        up = jnp.clip(up, -SWIGLU_LIMIT, SWIGLU_LIMIT)
        act = gate * jax.nn.sigmoid(SWIGLU_ALPHA * gate)
        return ((up + 1.0) * act) @ w_dn[e] + b_dn[e]

    ys = lax.map(expert_block, (xs, blk_exp)).reshape(cap, D)
    out = jnp.zeros((T, D), ys.dtype).at[slot_tok].add(ys * slot_gate[:, None].astype(ys.dtype))
    return out.reshape(B, S, D)


def setup_inputs(seed: int = 0) -> dict:
    key = jax.random.key(seed)
    ks = jax.random.split(key, 32)
    L, D, E, F = DEPTH, D_MODEL, N_EXPERTS, D_EXPERT

    def nrm(k, shape, scale):
        return jax.random.normal(k, shape, jnp.float32) * scale

    def gain(k, shape):
        return 1.0 + 0.05 * jax.random.normal(k, shape, jnp.float32)

    a_c = jax.random.uniform(ks[12], (L, 2, LRU_WIDTH), jnp.float32, 0.9, 0.999)
    s = a_c ** (1.0 / LRU_C)
    lam = jnp.log(s) - jnp.log1p(-s)
    return {
        "x": nrm(ks[0], (BATCH, SEQ, D), 1.0),
        "p": nrm(ks[1], (DEPTH, BATCH, SEQ, PLE_DIM), 1.0),
        "g_mix": gain(ks[2], (L, D)),
        "w_in": nrm(ks[3], (L, D, IN_WIDTH), D ** -0.5),
        "q_norm": gain(ks[4], (L, HEAD_DIM)),
        "k_norm": gain(ks[5], (L, HEAD_DIM)),
        "conv_w": nrm(ks[6], (L, CONV_WIDTH, LRU_WIDTH), CONV_WIDTH ** -0.5),
        "conv_b": nrm(ks[7], (L, LRU_WIDTH), 0.01),
        "lru_wa": nrm(ks[8], (L, 2, LRU_BLOCKS, LRU_BLOCK_DIM, LRU_BLOCK_DIM), LRU_BLOCK_DIM ** -0.5),
        "lru_ba": nrm(ks[9], (L, 2, LRU_WIDTH), 0.01),
        "lru_wi": nrm(ks[10], (L, 2, LRU_BLOCKS, LRU_BLOCK_DIM, LRU_BLOCK_DIM), LRU_BLOCK_DIM ** -0.5),
        "lru_bi": nrm(ks[11], (L, 2, LRU_WIDTH), 0.01),
        "lru_lam": lam,
        "w_attn_br": nrm(ks[13], (L, Q_W, D), Q_W ** -0.5),
        "w_lru_br": nrm(ks[14], (L, LRU_WIDTH, D), LRU_WIDTH ** -0.5),
        "w_out": nrm(ks[15], (L, D, D), D ** -0.5),
        "g_moe": gain(ks[16], (L, D)),
        "w_router": nrm(ks[17], (L, D, E), D ** -0.5),
        "b_router": nrm(ks[18], (L, E), 0.01),
        "w_gu": nrm(ks[19], (L, E, D, 2 * F), D ** -0.5),
        "b_gu": nrm(ks[20], (L, E, 2 * F), 0.01),
        "w_dn": nrm(ks[21], (L, E, F, D), F ** -0.5),
        "b_dn": nrm(ks[22], (L, E, D), 0.01),
        "g_ple": gain(ks[23], (L, D)),
        "w_ple_gate": nrm(ks[24], (L, D, D), D ** -0.5),
        "w_ple_proj": nrm(ks[25], (L, PLE_DIM, D), PLE_DIM ** -0.5),
    }


def reference(x, p, g_mix, w_in, q_norm, k_norm, conv_w, conv_b, lru_wa, lru_ba, lru_wi, lru_bi,
              lru_lam, w_attn_br, w_lru_br, w_out, g_moe, w_router, b_router, w_gu, b_gu, w_dn, b_dn,
              g_ple, w_ple_gate, w_ple_proj):
    S = x.shape[1]
    rope = axial_rope_tables(S)
    h = x
    for l in range(DEPTH):
        h = h + hybrid_mixer(h, rope, g_mix[l], w_in[l], q_norm[l], k_norm[l], conv_w[l], conv_b[l],
                             lru_wa[l], lru_ba[l], lru_wi[l], lru_bi[l], lru_lam[l],
                             w_attn_br[l], w_lru_br[l], w_out[l])
        h = h + moe_ffn(h, g_moe[l], w_router[l], b_router[l], w_gu[l], b_gu[l], w_dn[l], b_dn[l])
        u = rms_norm(h, g_ple[l])
        h = h + jax.nn.sigmoid(u @ w_ple_gate[l]) * (p[l] @ w_ple_proj[l])
    return h
```

```python
import functools

import jax
import jax.numpy as jnp
import numpy as np
from jax import lax
from jax.experimental import pallas as pl
from jax.experimental.pallas import tpu as pltpu

D_MODEL = 1024
N_Q_HEADS = 8
N_KV_HEADS = 2
HEAD_DIM = 128
Q_GROUP = N_Q_HEADS // N_KV_HEADS
ROPE_THETA = 10000.0
GRID_W = 64
LRU_WIDTH = 1024
LRU_BLOCKS = 8
LRU_BLOCK_DIM = LRU_WIDTH // LRU_BLOCKS
LRU_C = 8.0
N_EXPERTS = 32
TOP_K = 4
D_EXPERT = 1024
SWIGLU_LIMIT = 7.0
SWIGLU_ALPHA = 1.702
PLE_DIM = 256
EPS = 1e-6
Q_W = N_Q_HEADS * HEAD_DIM
KV_W = N_KV_HEADS * HEAD_DIM
GROUP_W = Q_GROUP * HEAD_DIM
IN_WIDTH = Q_W + 2 * KV_W + 2 * LRU_WIDTH + 2 * D_MODEL
OFF_Q = 0
OFF_K = OFF_Q + Q_W
OFF_V = OFF_K + KV_W
OFF_XR = OFF_V + KV_W
OFF_XG = OFF_XR + LRU_WIDTH
OFF_GA = OFF_XG + LRU_WIDTH
OFF_GR = OFF_GA + D_MODEL

V7X_LANES = 128
V7X_VMEM_LIMIT_BYTES = 56 * 1024 * 1024
TOKEN_TILE = 512
ATTN_Q_TILE = 256
LRU_STEPS = 32
EXPERT_TILE = 512
RANK_TILE = 512
DISPATCH_TOKENS = 512
COMBINE_TOKENS = 256

F32 = jnp.float32
BF16 = jnp.bfloat16


def _params(*sem):
    return pltpu.CompilerParams(dimension_semantics=sem, vmem_limit_bytes=V7X_VMEM_LIMIT_BYTES)


def _bdot(a, b):
    return jnp.dot(a, b, preferred_element_type=F32)


def _rms(x, g):
    return x * lax.rsqrt(jnp.mean(x * x, axis=-1, keepdims=True) + EPS) * g


def _sigmoid(x):
    return 1.0 / (1.0 + jnp.exp(-x))


def _in_proj_kernel(x_ref, g_ref, w_ref, qn_ref, kn_ref, cos_ref, sina_ref, sinb_ref,
                    q_ref, k_ref, v_ref, xr_ref, xg_ref, ga_ref, gr_ref):
    u = _rms(x_ref[...], g_ref[...]).astype(BF16)
    cos, sina, sinb = cos_ref[...], sina_ref[...], sinb_ref[...]

    def head(z, gain):
        y = _rms(z, gain)
        return y * cos + pltpu.roll(y, 96, 1) * sina + pltpu.roll(y, 32, 1) * sinb

    zq = _bdot(u, w_ref[:, OFF_Q:OFF_Q + Q_W])
    for h in range(N_Q_HEADS):
        sl = slice(h * HEAD_DIM, (h + 1) * HEAD_DIM)
        q_ref[:, sl] = head(zq[:, sl], qn_ref[...]).astype(BF16)
    zk = _bdot(u, w_ref[:, OFF_K:OFF_K + KV_W])
    for h in range(N_KV_HEADS):
        sl = slice(h * HEAD_DIM, (h + 1) * HEAD_DIM)
        k_ref[:, sl] = head(zk[:, sl], kn_ref[...]).astype(BF16)
    v_ref[...] = _bdot(u, w_ref[:, OFF_V:OFF_V + KV_W]).astype(BF16)
    xr_ref[...] = _bdot(u, w_ref[:, OFF_XR:OFF_XR + LRU_WIDTH])
    xg_ref[...] = _bdot(u, w_ref[:, OFF_XG:OFF_XG + LRU_WIDTH])
    ga_ref[...] = _bdot(u, w_ref[:, OFF_GA:OFF_GA + D_MODEL])
    gr_ref[...] = _bdot(u, w_ref[:, OFF_GR:OFF_GR + D_MODEL])


def _in_proj(x, g_mix, w_in, qn, kn, cos, sina, sinb):
    B, S, D = x.shape
    tm = TOKEN_TILE
    bm = lambda w: pl.BlockSpec((None, tm, w), lambda b, i: (b, i, 0))
    full = lambda a: pl.BlockSpec(a.shape, lambda b, i: (0,) * a.ndim)
    tab = pl.BlockSpec((tm, HEAD_DIM), lambda b, i: (i, 0))
    return pl.pallas_call(
        _in_proj_kernel,
        grid=(B, S // tm),
        in_specs=[bm(D), full(g_mix), full(w_in), full(qn), full(kn), tab, tab, tab],
        out_specs=[bm(Q_W), bm(KV_W), bm(KV_W), bm(LRU_WIDTH), bm(LRU_WIDTH), bm(D), bm(D)],
        out_shape=[
            jax.ShapeDtypeStruct((B, S, Q_W), BF16),
            jax.ShapeDtypeStruct((B, S, KV_W), BF16),
            jax.ShapeDtypeStruct((B, S, KV_W), BF16),
            jax.ShapeDtypeStruct((B, S, LRU_WIDTH), F32),
            jax.ShapeDtypeStruct((B, S, LRU_WIDTH), F32),
            jax.ShapeDtypeStruct((B, S, D), F32),
            jax.ShapeDtypeStruct((B, S, D), F32),
        ],
        compiler_params=_params("arbitrary", "arbitrary"),
        name="in_proj",
    )(x, g_mix, w_in, qn, kn, cos, sina, sinb)


def _attn_kernel(q_ref, k_ref, v_ref, o_ref):
    k = k_ref[...]
    v = v_ref[...]
    for h in range(Q_GROUP):
        sl = slice(h * HEAD_DIM, (h + 1) * HEAD_DIM)
        s = lax.dot_general(q_ref[:, sl], k, (((1,), (1,)), ((), ())), preferred_element_type=F32)
        p = jnp.exp(s - jnp.max(s, axis=-1, keepdims=True))
        l = jnp.sum(p, axis=-1, keepdims=True)
        o = _bdot(p.astype(BF16), v)
        o_ref[:, sl] = (o * (1.0 / l)).astype(BF16)


def _attention(q, k, v):
    B, S, _ = q.shape
    tq = ATTN_Q_TILE
    return pl.pallas_call(
        _attn_kernel,
        grid=(B, N_KV_HEADS, S // tq),
        in_specs=[
            pl.BlockSpec((None, tq, GROUP_W), lambda b, g, i: (b, i, g)),
            pl.BlockSpec((None, S, HEAD_DIM), lambda b, g, i: (b, 0, g)),
            pl.BlockSpec((None, S, HEAD_DIM), lambda b, g, i: (b, 0, g)),
        ],
        out_specs=pl.BlockSpec((None, tq, GROUP_W), lambda b, g, i: (b, i, g)),
        out_shape=jax.ShapeDtypeStruct((B, S, Q_W), BF16),
        compiler_params=_params("arbitrary", "arbitrary", "arbitrary"),
        name="attention",
    )(q, k, v)


def _gelu_tanh(x):
    return 0.5 * x * (1.0 + jnp.tanh(np.sqrt(2.0 / np.pi) * (x + 0.044715 * (x * x * x))))


def _lru_kernel(*refs, batch, reverse):
    if reverse:
        (xc_ref, xp_ref, xn_ref, cw_ref, cb_ref, w_ref, b_ref, lam_ref, hf_ref, xg_ref,
         o_ref, ext_ref, a_ref, bt_ref, h_ref) = refs
    else:
        (xc_ref, xp_ref, xn_ref, cw_ref, cb_ref, w_ref, b_ref, lam_ref,
         o_ref, ext_ref, a_ref, bt_ref, h_ref) = refs
    step = pl.program_id(0)
    nsteps = pl.num_programs(0)
    chunk = (nsteps - 1 - step) if reverse else step
    rows = LRU_STEPS * batch

    @pl.when(step == 0)
    def _():
        h_ref[...] = jnp.zeros_like(h_ref)

    ext_ref[0:2 * batch, :] = jnp.where(chunk > 0, xp_ref[...], 0.0)
    ext_ref[2 * batch:2 * batch + rows, :] = xc_ref[...]
    ext_ref[2 * batch + rows:3 * batch + rows, :] = jnp.where(chunk < nsteps - 1, xn_ref[...], 0.0)
    c = cb_ref[...] + sum(cw_ref[j:j + 1, :] * ext_ref[j * batch:j * batch + rows, :] for j in range(4))
    cb16 = c.astype(BF16)
    lam = lam_ref[...]
    neg_softplus = -(jnp.maximum(-lam, 0.0) + jnp.log1p(jnp.exp(-jnp.abs(lam))))
    for n in range(LRU_BLOCKS):
        sl = slice(n * LRU_BLOCK_DIM, (n + 1) * LRU_BLOCK_DIM)
        pre = _bdot(cb16[:, sl], w_ref[n]) + b_ref[n]
        r = _sigmoid(pre[:, :LRU_BLOCK_DIM])
        ig = _sigmoid(pre[:, LRU_BLOCK_DIM:])
        log_a = LRU_C * r * neg_softplus[:, sl]
        a = jnp.exp(log_a)
        a_ref[:, sl] = a
        bt_ref[:, sl] = jnp.sqrt(-jnp.tanh(log_a) * (a * a + 1.0)) * ig * c[:, sl]

    def scan_step(t, h):
        tt = (LRU_STEPS - 1 - t) if reverse else t
        rs = pl.ds(pl.multiple_of(tt * batch, batch), batch)
        h = a_ref[rs, :] * h + bt_ref[rs, :]
        if reverse:
            o_ref[rs, :] = ((h + hf_ref[rs, :]) * _gelu_tanh(xg_ref[rs, :])).astype(o_ref.dtype)
        else:
            o_ref[rs, :] = h
        return h

    h_ref[...] = lax.fori_loop(0, LRU_STEPS, scan_step, h_ref[...], unroll=4)


def _lru_scan(xr2, conv_w, conv_b, wcat, bcat, lam, *, batch, reverse, h_fwd=None, xg2=None):
    n_rows, width = xr2.shape
    rows = LRU_STEPS * batch
    n = n_rows // rows
    ch = (lambda i: n - 1 - i) if reverse else (lambda i: i)
    tile = pl.BlockSpec((rows, width), lambda i: (ch(i), 0))
    prev = pl.BlockSpec((2 * batch, width), lambda i: (jnp.maximum(ch(i) * (LRU_STEPS // 2) - 1, 0), 0))
    nxt = pl.BlockSpec((batch, width),
                       lambda i: (jnp.minimum((ch(i) + 1) * LRU_STEPS, n * LRU_STEPS - 1), 0))
    full = lambda a: pl.BlockSpec(a.shape, lambda i: (0,) * a.ndim)
    ins = [xr2, xr2, xr2, conv_w, conv_b, wcat, bcat, lam]
    specs = [tile, prev, nxt, full(conv_w), full(conv_b), full(wcat), full(bcat), full(lam)]
    if reverse:
        ins += [h_fwd, xg2]
        specs += [tile, tile]
    return pl.pallas_call(
        functools.partial(_lru_kernel, batch=batch, reverse=reverse),
        grid=(n,),
        in_specs=specs,
        out_specs=tile,
        out_shape=jax.ShapeDtypeStruct((n_rows, width), BF16 if reverse else F32),
        scratch_shapes=[
            pltpu.VMEM((rows + 3 * batch, width), F32),
            pltpu.VMEM((rows, width), F32),
            pltpu.VMEM((rows, width), F32),
            pltpu.VMEM((batch, width), F32),
        ],
        compiler_params=_params("arbitrary"),
        name="lru_bwd" if reverse else "lru_fwd",
    )(*ins)


def _merge_kernel(x_ref, at_ref, yl_ref, ga_ref, gr_ref, wa_ref, wl_ref, wo_ref, gm_ref, wr_ref, br_ref,
                  h_ref, u_ref, idx_ref, gate_ref, cnt_ref):
    y_attn = _bdot(at_ref[...], wa_ref[...])
    y_lru = _bdot(yl_ref[...], wl_ref[...])
    merged = _sigmoid(ga_ref[...]) * y_attn + _sigmoid(gr_ref[...]) * y_lru
    h = x_ref[...] + _bdot(merged.astype(BF16), wo_ref[...])
    h_ref[...] = h
    u = _rms(h, gm_ref[...])
    u_ref[...] = u
    logits = jnp.dot(u, wr_ref[...], preferred_element_type=F32, precision=lax.Precision.HIGHEST)
    lane = lax.broadcasted_iota(jnp.int32, logits.shape, 1)
    work = jnp.where(lane < N_EXPERTS, logits + br_ref[...], -jnp.inf)
    vals, hits = [], jnp.zeros(logits.shape, F32)
    for kk in range(TOP_K):
        m = jnp.max(work, axis=-1, keepdims=True)
        idx = jnp.min(jnp.where(work == m, lane, V7X_LANES), axis=-1, keepdims=True)
        pick = lane == idx
        work = jnp.where(pick, -jnp.inf, work)
        hits = hits + pick.astype(F32)
        vals.append(m)
        idx_ref[:, kk:kk + 1] = idx
    es = [jnp.exp(vv - vals[0]) for vv in vals]
    inv = 1.0 / sum(es)
    for kk in range(TOP_K):
        gate_ref[:, kk:kk + 1] = es[kk] * inv

    @pl.when((pl.program_id(0) == 0) & (pl.program_id(1) == 0))
    def _():
        cnt_ref[...] = jnp.zeros_like(cnt_ref)

    cnt_ref[0:1, :] += jnp.sum(hits, axis=0, keepdims=True)


def _merge(x, attn, ylru, ga, gr, wa, wl, wo, g_moe, wr, br):
    B, S, D = x.shape
    tm = TOKEN_TILE
    bm = lambda w: pl.BlockSpec((None, tm, w), lambda b, i: (b, i, 0))
    full = lambda a: pl.BlockSpec(a.shape, lambda b, i: (0,) * a.ndim)
    return pl.pallas_call(
        _merge_kernel,
        grid=(B, S // tm),
        in_specs=[bm(D), bm(Q_W), bm(LRU_WIDTH), bm(D), bm(D), full(wa), full(wl), full(wo),
                  full(g_moe), full(wr), full(br)],
        out_specs=[bm(D), bm(D), bm(TOP_K), bm(TOP_K), pl.BlockSpec((8, V7X_LANES), lambda b, i: (0, 0))],
        out_shape=[
            jax.ShapeDtypeStruct((B, S, D), F32),
            jax.ShapeDtypeStruct((B, S, D), F32),
            jax.ShapeDtypeStruct((B, S, TOP_K), jnp.int32),
            jax.ShapeDtypeStruct((B, S, TOP_K), F32),
            jax.ShapeDtypeStruct((8, V7X_LANES), F32),
        ],
        compiler_params=_params("arbitrary", "arbitrary"),
        name="merge_router",
    )(x, attn, ylru, ga, gr, wa, wl, wo, g_moe, wr, br)


def _rank_kernel(idx_ref, ps_ref, dest_ref, carry_ref):
    @pl.when(pl.program_id(0) == 0)
    def _():
        carry_ref[...] = jnp.zeros_like(carry_ref)

    tm = idx_ref.shape[0]
    lane = lax.broadcasted_iota(jnp.int32, (tm, V7X_LANES), 1)
    picks = [lane == idx_ref[:, kk:kk + 1] for kk in range(TOP_K)]
    hits = sum(p.astype(F32) for p in picks)
    earlier = (lax.broadcasted_iota(jnp.int32, (tm, tm), 0) >
               lax.broadcasted_iota(jnp.int32, (tm, tm), 1)).astype(BF16)
    base = _bdot(earlier, hits.astype(BF16)) + carry_ref[...] + ps_ref[...]
    for kk in range(TOP_K):
        dest_ref[:, kk:kk + 1] = jnp.sum(jnp.where(picks[kk], base, 0.0), axis=-1,
                                         keepdims=True).astype(jnp.int32)
    carry_ref[...] += jnp.sum(hits, axis=0, keepdims=True)


def _rank(idx, pad_start_row):
    T = idx.shape[0]
    tm = RANK_TILE
    return pl.pallas_call(
        _rank_kernel,
        grid=(T // tm,),
        in_specs=[pl.BlockSpec((tm, TOP_K), lambda i: (i, 0)),
                  pl.BlockSpec((1, V7X_LANES), lambda i: (0, 0))],
        out_specs=pl.BlockSpec((tm, TOP_K), lambda i: (i, 0)),
        out_shape=jax.ShapeDtypeStruct((T, TOP_K), jnp.int32),
        scratch_shapes=[pltpu.VMEM((1, V7X_LANES), F32)],
        compiler_params=_params("arbitrary"),
        name="slot_rank",
    )(idx, pad_start_row)


def _dispatch_kernel(pad_end_ref, padded_ref, dest_ref, u_hbm, xs_hbm, zero_ref, sem, zsem):
    step = pl.program_id(0)

    def zero_copy(e):
        start = pl.multiple_of(pad_end_ref[e] - EXPERT_TILE, EXPERT_TILE)
        return pltpu.make_async_copy(zero_ref, xs_hbm.at[pl.ds(start, EXPERT_TILE)], zsem)

    @pl.when(step == 0)
    def _():
        zero_ref[...] = jnp.zeros_like(zero_ref)
        for e in range(N_EXPERTS):
            @pl.when(padded_ref[e] > 0)
            def _():
                zero_copy(e).start()
        for e in range(N_EXPERTS):
            @pl.when(padded_ref[e] > 0)
            def _():
                zero_copy(e).wait()

    def row_copy(tok, slot):
        return pltpu.make_async_copy(u_hbm.at[pl.ds(tok, 1)], xs_hbm.at[pl.ds(slot, 1)], sem)

    def issue(j, carry):
        for kk in range(TOP_K):
            row_copy(step * DISPATCH_TOKENS + j, dest_ref[j * TOP_K + kk]).start()
        return carry

    def drain(j, carry):
        for kk in range(TOP_K):
            row_copy(0, 0).wait()
        return carry

    lax.fori_loop(0, DISPATCH_TOKENS, issue, 0)
    lax.fori_loop(0, DISPATCH_TOKENS, drain, 0)


def _dispatch(pad_end, padded, dest_flat, u, cap):
    T, D = u.shape
    n = DISPATCH_TOKENS * TOP_K
    return pl.pallas_call(
        _dispatch_kernel,
        grid_spec=pltpu.PrefetchScalarGridSpec(
            num_scalar_prefetch=2,
            grid=(T // DISPATCH_TOKENS,),
            in_specs=[pl.BlockSpec((n,), lambda i, pe, pd: (i,), memory_space=pltpu.SMEM),
                      pl.BlockSpec(memory_space=pl.ANY)],
            out_specs=pl.BlockSpec(memory_space=pl.ANY),
            scratch_shapes=[pltpu.VMEM((EXPERT_TILE, D), F32),
                            pltpu.SemaphoreType.DMA(()), pltpu.SemaphoreType.DMA(())],
        ),
        out_shape=jax.ShapeDtypeStruct((cap, D), F32),
        compiler_params=_params("arbitrary"),
        name="dispatch",
    )(pad_end, padded, dest_flat, u)


def _expert_kernel(be_ref, nu_ref, xs_ref, wgu_ref, bgu_ref, wdn_ref, bdn_ref, ys_ref):
    @pl.when(pl.program_id(0) < nu_ref[0])
    def _():
        hu = _bdot(xs_ref[...].astype(BF16), wgu_ref[...].astype(BF16)) + bgu_ref[...]
        gate = jnp.minimum(hu[:, :D_EXPERT], SWIGLU_LIMIT)
        up = jnp.clip(hu[:, D_EXPERT:], -SWIGLU_LIMIT, SWIGLU_LIMIT)
        act = gate * _sigmoid(SWIGLU_ALPHA * gate)
        ys_ref[...] = _bdot(((up + 1.0) * act).astype(BF16), wdn_ref[...].astype(BF16)) + bdn_ref[...]


def _experts(blk_exp, n_used, xs, w_gu, b_gu, w_dn, b_dn):
    cap, D = xs.shape
    row = lambda j, be, nu: (jnp.minimum(j, nu[0] - 1), 0)
    exp3 = lambda j, be, nu: (be[j], 0, 0)
    return pl.pallas_call(
        _expert_kernel,
        grid_spec=pltpu.PrefetchScalarGridSpec(
            num_scalar_prefetch=2,
            grid=(cap // EXPERT_TILE,),
            in_specs=[pl.BlockSpec((EXPERT_TILE, D), row),
                      pl.BlockSpec((None, D, 2 * D_EXPERT), exp3),
                      pl.BlockSpec((None, 1, 2 * D_EXPERT), exp3),
                      pl.BlockSpec((None, D_EXPERT, D), exp3),
                      pl.BlockSpec((None, 1, D), exp3)],
            out_specs=pl.BlockSpec((EXPERT_TILE, D), row),
        ),
        out_shape=jax.ShapeDtypeStruct((cap, D), F32),
        compiler_params=_params("arbitrary"),
        name="experts",
    )(blk_exp, n_used, xs, w_gu, b_gu, w_dn, b_dn)


def _combine_kernel(dest_ref, ys_hbm, gate_ref, h_ref, p_ref, gp_ref, wg_ref, wp_ref, o_ref, buf_ref, sem):
    def row_copy(j, kk, slot):
        return pltpu.make_async_copy(ys_hbm.at[pl.ds(slot, 1)], buf_ref.at[kk, pl.ds(j, 1)], sem)

    def issue(j, carry):
        for kk in range(TOP_K):
            row_copy(j, kk, dest_ref[j * TOP_K + kk]).start()
        return carry

    def drain(j, carry):
        for kk in range(TOP_K):
            row_copy(j, kk, 0).wait()
        return carry

    lax.fori_loop(0, COMBINE_TOKENS, issue, 0)
    lax.fori_loop(0, COMBINE_TOKENS, drain, 0)
    h = h_ref[...]
    for kk in range(TOP_K):
        h = h + gate_ref[:, kk:kk + 1] * buf_ref[kk]
    u = _rms(h, gp_ref[...]).astype(BF16)
    o_ref[...] = h + _sigmoid(_bdot(u, wg_ref[...])) * _bdot(p_ref[...].astype(BF16), wp_ref[...])


def _combine(dest_flat, ys, gate, h, p, g_ple, wg, wp):
    T, D = h.shape
    tm = COMBINE_TOKENS
    full = lambda a: pl.BlockSpec(a.shape, lambda i: (0,) * a.ndim)
    return pl.pallas_call(
        _combine_kernel,
        grid=(T // tm,),
        in_specs=[pl.BlockSpec((tm * TOP_K,), lambda i: (i,), memory_space=pltpu.SMEM),
                  pl.BlockSpec(memory_space=pl.ANY),
                  pl.BlockSpec((tm, TOP_K), lambda i: (i, 0)),
                  pl.BlockSpec((tm, D), lambda i: (i, 0)),
                  pl.BlockSpec((tm, PLE_DIM), lambda i: (i, 0)),
                  full(g_ple), full(wg), full(wp)],
        out_specs=pl.BlockSpec((tm, D), lambda i: (i, 0)),
        out_shape=jax.ShapeDtypeStruct((T, D), F32),
        scratch_shapes=[pltpu.VMEM((TOP_K, tm, D), F32), pltpu.SemaphoreType.DMA(())],
        compiler_params=_params("arbitrary"),
        name="combine_ple",
    )(dest_flat, ys, gate, h, p, g_ple, wg, wp)


def _rope_tables(seq_len):
    pos = jnp.arange(seq_len)
    axis_dim = HEAD_DIM // 2
    inv_freq = ROPE_THETA ** (-jnp.arange(0, axis_dim, 2, dtype=F32) / axis_dim)
    ang_r = (pos // GRID_W).astype(F32)[:, None] * inv_freq[None, :]
    ang_c = (pos % GRID_W).astype(F32)[:, None] * inv_freq[None, :]
    zero = jnp.zeros_like(ang_r)
    cos = jnp.concatenate([jnp.cos(ang_r)] * 2 + [jnp.cos(ang_c)] * 2, axis=-1)
    sina = jnp.concatenate([-jnp.sin(ang_r), zero, -jnp.sin(ang_c), zero], axis=-1)
    sinb = jnp.concatenate([zero, jnp.sin(ang_r), zero, jnp.sin(ang_c)], axis=-1)
    return cos, sina, sinb


def _layer(h, p, g_mix, w_in, q_norm, k_norm, conv_w, conv_b, lru_wa, lru_ba, lru_wi, lru_bi, lru_lam,
           w_attn_br, w_lru_br, w_out, g_moe, w_router, b_router, w_gu, b_gu, w_dn, b_dn,
           g_ple, w_ple_gate, w_ple_proj, rope):
    B, S, D = h.shape
    T = B * S
    row = lambda a: a.reshape(1, -1)
    q, k, v, xr, xg, ga, gr = _in_proj(
        h, row(g_mix), w_in.astype(BF16), row(q_norm) * HEAD_DIM ** -0.5, row(k_norm), *rope)
    attn = _attention(q, k, v)
    xr2 = xr.transpose(1, 0, 2).reshape(S * B, LRU_WIDTH)
    xg2 = xg.transpose(1, 0, 2).reshape(S * B, LRU_WIDTH)
    wcat = jnp.concatenate([lru_wa, lru_wi], axis=-1).astype(BF16)
    bcat = jnp.concatenate([lru_ba.reshape(2, LRU_BLOCKS, 1, LRU_BLOCK_DIM),
                            lru_bi.reshape(2, LRU_BLOCKS, 1, LRU_BLOCK_DIM)], axis=-1)
    h_fwd = _lru_scan(xr2, conv_w, row(conv_b), wcat[0], bcat[0], row(lru_lam[0]), batch=B, reverse=False)
    ylru = _lru_scan(xr2, conv_w, row(conv_b), wcat[1], bcat[1], row(lru_lam[1]), batch=B, reverse=True,
                     h_fwd=h_fwd, xg2=xg2).reshape(S, B, LRU_WIDTH).transpose(1, 0, 2)
    w_router_p = jnp.zeros((D, V7X_LANES), F32).at[:, :N_EXPERTS].set(w_router)
    b_router_p = jnp.zeros((1, V7X_LANES), F32).at[0, :N_EXPERTS].set(b_router)
    h1, u2, idx, gate, cnt = _merge(h, attn, ylru, ga, gr, w_attn_br.astype(BF16), w_lru_br.astype(BF16),
                                    w_out.astype(BF16), row(g_moe), w_router_p, b_router_p)
    cap = T * TOP_K + N_EXPERTS * EXPERT_TILE
    counts = cnt[0, :N_EXPERTS].astype(jnp.int32)
    padded = ((counts + EXPERT_TILE - 1) // EXPERT_TILE) * EXPERT_TILE
    pad_end = jnp.cumsum(padded)
    pad_start = pad_end - padded
    blk_start = jnp.arange(cap // EXPERT_TILE, dtype=jnp.int32) * EXPERT_TILE
    blk_exp = jnp.minimum(jnp.sum(blk_start[:, None] >= pad_end[None, :], axis=1), N_EXPERTS - 1)
    n_used = (pad_end[-1:] // EXPERT_TILE).astype(jnp.int32)
    ps_row = jnp.zeros((1, V7X_LANES), F32).at[0, :N_EXPERTS].set(pad_start.astype(F32))
    dest = _rank(idx.reshape(T, TOP_K), ps_row).reshape(T * TOP_K)
    xs = _dispatch(pad_end.astype(jnp.int32), padded.astype(jnp.int32), dest, u2.reshape(T, D), cap)
    ys = _experts(blk_exp.astype(jnp.int32), n_used, xs, w_gu, b_gu.reshape(N_EXPERTS, 1, -1),
                  w_dn, b_dn.reshape(N_EXPERTS, 1, -1))
    out = _combine(dest, ys, gate.reshape(T, TOP_K), h1.reshape(T, D), p.reshape(T, PLE_DIM),
                   row(g_ple), w_ple_gate.astype(BF16), w_ple_proj.astype(BF16))
    return out.reshape(B, S, D)


def kernel(x, p, g_mix, w_in, q_norm, k_norm, conv_w, conv_b, lru_wa, lru_ba, lru_wi, lru_bi, lru_lam,
           w_attn_br, w_lru_br, w_out, g_moe, w_router, b_router, w_gu, b_gu, w_dn, b_dn,
           g_ple, w_ple_gate, w_ple_proj):
    rope = _rope_tables(x.shape[1])
    h = x
    for l in range(p.shape[0]):
        h = _layer(h, p[l], g_mix[l], w_in[l], q_norm[l], k_norm[l], conv_w[l], conv_b[l], lru_wa[l],
                   lru_ba[l], lru_wi[l], lru_bi[l], lru_lam[l], w_attn_br[l], w_lru_br[l], w_out[l],
                   g_moe[l], w_router[l], b_router[l], w_gu[l], b_gu[l], w_dn[l], b_dn[l],
                   g_ple[l], w_ple_gate[l], w_ple_proj[l], rope)
    return h
```

```python
import functools

import jax
import jax.numpy as jnp
import numpy as np
from jax import lax
from jax.experimental import pallas as pl
from jax.experimental.pallas import tpu as pltpu

D_MODEL = 1024
N_Q_HEADS = 8
N_KV_HEADS = 2
HEAD_DIM = 128
Q_GROUP = N_Q_HEADS // N_KV_HEADS
ROPE_THETA = 10000.0
GRID_W = 64
LRU_WIDTH = 1024
LRU_BLOCKS = 8
LRU_BLOCK_DIM = LRU_WIDTH // LRU_BLOCKS
LRU_C = 8.0
N_EXPERTS = 32
TOP_K = 4
D_EXPERT = 1024
SWIGLU_LIMIT = 7.0
SWIGLU_ALPHA = 1.702
PLE_DIM = 256
EPS = 1e-6
Q_W = N_Q_HEADS * HEAD_DIM
KV_W = N_KV_HEADS * HEAD_DIM
GROUP_W = Q_GROUP * HEAD_DIM
IN_WIDTH = Q_W + 2 * KV_W + 2 * LRU_WIDTH + 2 * D_MODEL
OFF_Q = 0
OFF_K = OFF_Q + Q_W
OFF_V = OFF_K + KV_W
OFF_XR = OFF_V + KV_W
OFF_XG = OFF_XR + LRU_WIDTH
OFF_GA = OFF_XG + LRU_WIDTH
OFF_GR = OFF_GA + D_MODEL

V7X_LANES = 128
V7X_VMEM_LIMIT_BYTES = 56 * 1024 * 1024
TOKEN_TILE = 512
ATTN_Q_TILE = 256
LRU_STEPS = 32
EXPERT_TILE = 512
RANK_TILE = 512
DISPATCH_TOKENS = 512
COMBINE_TOKENS = 256

F32 = jnp.float32
BF16 = jnp.bfloat16


def _params(*sem):
    return pltpu.CompilerParams(dimension_semantics=sem, vmem_limit_bytes=V7X_VMEM_LIMIT_BYTES)


def _bdot(a, b):
    return jnp.dot(a, b, preferred_element_type=F32)


def _rms(x, g):
    return x * lax.rsqrt(jnp.mean(x * x, axis=-1, keepdims=True) + EPS) * g


def _sigmoid(x):
    return 1.0 / (1.0 + jnp.exp(-x))


def _in_proj_kernel(x_ref, g_ref, w_ref, qn_ref, kn_ref, cos_ref, sina_ref, sinb_ref,
                    q_ref, k_ref, v_ref, xr_ref, xg_ref, ga_ref, gr_ref):
    u = _rms(x_ref[...], g_ref[...]).astype(BF16)
    cos, sina, sinb = cos_ref[...], sina_ref[...], sinb_ref[...]

    def head(z, gain):
        y = _rms(z, gain)
        return y * cos + pltpu.roll(y, 96, 1) * sina + pltpu.roll(y, 32, 1) * sinb

    zq = _bdot(u, w_ref[:, OFF_Q:OFF_Q + Q_W])
    for h in range(N_Q_HEADS):
        sl = slice(h * HEAD_DIM, (h + 1) * HEAD_DIM)
        q_ref[:, sl] = head(zq[:, sl], qn_ref[...]).astype(BF16)
    zk = _bdot(u, w_ref[:, OFF_K:OFF_K + KV_W])
    for h in range(N_KV_HEADS):
        sl = slice(h * HEAD_DIM, (h + 1) * HEAD_DIM)
        k_ref[:, sl] = head(zk[:, sl], kn_ref[...]).astype(BF16)
    v_ref[...] = _bdot(u, w_ref[:, OFF_V:OFF_V + KV_W]).astype(BF16)
    xr_ref[...] = _bdot(u, w_ref[:, OFF_XR:OFF_XR + LRU_WIDTH])
    xg_ref[...] = _bdot(u, w_ref[:, OFF_XG:OFF_XG + LRU_WIDTH])
    ga_ref[...] = _bdot(u, w_ref[:, OFF_GA:OFF_GA + D_MODEL])
    gr_ref[...] = _bdot(u, w_ref[:, OFF_GR:OFF_GR + D_MODEL])


def _in_proj(x, g_mix, w_in, qn, kn, cos, sina, sinb):
    B, S, D = x.shape
    tm = TOKEN_TILE
    bm = lambda w: pl.BlockSpec((None, tm, w), lambda b, i: (b, i, 0))
    full = lambda a: pl.BlockSpec(a.shape, lambda b, i: (0,) * a.ndim)
    tab = pl.BlockSpec((tm, HEAD_DIM), lambda b, i: (i, 0))
    return pl.pallas_call(
        _in_proj_kernel,
        grid=(B, S // tm),
        in_specs=[bm(D), full(g_mix), full(w_in), full(qn), full(kn), tab, tab, tab],
        out_specs=[bm(Q_W), bm(KV_W), bm(KV_W), bm(LRU_WIDTH), bm(LRU_WIDTH), bm(D), bm(D)],
        out_shape=[
            jax.ShapeDtypeStruct((B, S, Q_W), BF16),
            jax.ShapeDtypeStruct((B, S, KV_W), BF16),
            jax.ShapeDtypeStruct((B, S, KV_W), BF16),
            jax.ShapeDtypeStruct((B, S, LRU_WIDTH), F32),
            jax.ShapeDtypeStruct((B, S, LRU_WIDTH), F32),
            jax.ShapeDtypeStruct((B, S, D), F32),
            jax.ShapeDtypeStruct((B, S, D), F32),
        ],
        compiler_params=_params("arbitrary", "arbitrary"),
        name="in_proj",
    )(x, g_mix, w_in, qn, kn, cos, sina, sinb)


def _attn_kernel(q_ref, k_ref, v_ref, o_ref):
    k = k_ref[...]
    v = v_ref[...]
    for h in range(Q_GROUP):
        sl = slice(h * HEAD_DIM, (h + 1) * HEAD_DIM)
        s = lax.dot_general(q_ref[:, sl], k, (((1,), (1,)), ((), ())), preferred_element_type=F32)
        p = jnp.exp(s - jnp.max(s, axis=-1, keepdims=True))
        l = jnp.sum(p, axis=-1, keepdims=True)
        o = _bdot(p.astype(BF16), v)
        o_ref[:, sl] = (o * (1.0 / l)).astype(BF16)


def _attention(q, k, v):
    B, S, _ = q.shape
    tq = ATTN_Q_TILE
    return pl.pallas_call(
        _attn_kernel,
        grid=(B, N_KV_HEADS, S // tq),
        in_specs=[
            pl.BlockSpec((None, tq, GROUP_W), lambda b, g, i: (b, i, g)),
            pl.BlockSpec((None, S, HEAD_DIM), lambda b, g, i: (b, 0, g)),
            pl.BlockSpec((None, S, HEAD_DIM), lambda b, g, i: (b, 0, g)),
        ],
        out_specs=pl.BlockSpec((None, tq, GROUP_W), lambda b, g, i: (b, i, g)),
        out_shape=jax.ShapeDtypeStruct((B, S, Q_W), BF16),
        compiler_params=_params("arbitrary", "arbitrary", "arbitrary"),
        name="attention",
    )(q, k, v)


def _gelu_tanh(x):
    return 0.5 * x * (1.0 + jnp.tanh(np.sqrt(2.0 / np.pi) * (x + 0.044715 * (x * x * x))))


def _lru_kernel(*refs, batch, reverse):
    if reverse:
        (xc_ref, xp_ref, xn_ref, cw_ref, cb_ref, w_ref, b_ref, lam_ref, hf_ref, xg_ref,
         o_ref, ext_ref, a_ref, bt_ref, h_ref) = refs
    else:
        (xc_ref, xp_ref, xn_ref, cw_ref, cb_ref, w_ref, b_ref, lam_ref,
         o_ref, ext_ref, a_ref, bt_ref, h_ref) = refs
    step = pl.program_id(0)
    nsteps = pl.num_programs(0)
    chunk = (nsteps - 1 - step) if reverse else step
    rows = LRU_STEPS * batch

    @pl.when(step == 0)
    def _():
        h_ref[...] = jnp.zeros_like(h_ref)

    ext_ref[0:2 * batch, :] = jnp.where(chunk > 0, xp_ref[...], 0.0)
    ext_ref[2 * batch:2 * batch + rows, :] = xc_ref[...]
    ext_ref[2 * batch + rows:3 * batch + rows, :] = jnp.where(chunk < nsteps - 1, xn_ref[...], 0.0)
    c = cb_ref[...] + sum(cw_ref[j:j + 1, :] * ext_ref[j * batch:j * batch + rows, :] for j in range(4))
    cb16 = c.astype(BF16)
    lam = lam_ref[...]
    neg_softplus = -(jnp.maximum(-lam, 0.0) + jnp.log1p(jnp.exp(-jnp.abs(lam))))
    for n in range(LRU_BLOCKS):
        sl = slice(n * LRU_BLOCK_DIM, (n + 1) * LRU_BLOCK_DIM)
        pre = _bdot(cb16[:, sl], w_ref[n]) + b_ref[n]
        r = _sigmoid(pre[:, :LRU_BLOCK_DIM])
        ig = _sigmoid(pre[:, LRU_BLOCK_DIM:])
        log_a = LRU_C * r * neg_softplus[:, sl]
        a = jnp.exp(log_a)
        a_ref[:, sl] = a
        bt_ref[:, sl] = jnp.sqrt(-jnp.tanh(log_a) * (a * a + 1.0)) * ig * c[:, sl]

    def scan_step(t, h):
        tt = (LRU_STEPS - 1 - t) if reverse else t
        rs = pl.ds(pl.multiple_of(tt * batch, batch), batch)
        h = a_ref[rs, :] * h + bt_ref[rs, :]
        if reverse:
            o_ref[rs, :] = ((h + hf_ref[rs, :]) * _gelu_tanh(xg_ref[rs, :])).astype(o_ref.dtype)
        else:
            o_ref[rs, :] = h
        return h

    h_ref[...] = lax.fori_loop(0, LRU_STEPS, scan_step, h_ref[...], unroll=4)


def _lru_scan(xr2, conv_w, conv_b, wcat, bcat, lam, *, batch, reverse, h_fwd=None, xg2=None):
    n_rows, width = xr2.shape
    rows = LRU_STEPS * batch
    n = n_rows // rows
    ch = (lambda i: n - 1 - i) if reverse else (lambda i: i)
    tile = pl.BlockSpec((rows, width), lambda i: (ch(i), 0))
    prev = pl.BlockSpec((2 * batch, width), lambda i: (jnp.maximum(ch(i) * (LRU_STEPS // 2) - 1, 0), 0))
    nxt = pl.BlockSpec((batch, width),
                       lambda i: (jnp.minimum((ch(i) + 1) * LRU_STEPS, n * LRU_STEPS - 1), 0))
    full = lambda a: pl.BlockSpec(a.shape, lambda i: (0,) * a.ndim)
    ins = [xr2, xr2, xr2, conv_w, conv_b, wcat, bcat, lam]
    specs = [tile, prev, nxt, full(conv_w), full(conv_b), full(wcat), full(bcat), full(lam)]
    if reverse:
        ins += [h_fwd, xg2]
        specs += [tile, tile]
    return pl.pallas_call(
        functools.partial(_lru_kernel, batch=batch, reverse=reverse),
        grid=(n,),
        in_specs=specs,
        out_specs=tile,
        out_shape=jax.ShapeDtypeStruct((n_rows, width), BF16 if reverse else F32),
        scratch_shapes=[
            pltpu.VMEM((rows + 3 * batch, width), F32),
            pltpu.VMEM((rows, width), F32),
            pltpu.VMEM((rows, width), F32),
            pltpu.VMEM((batch, width), F32),
        ],
        compiler_params=_params("arbitrary"),
        name="lru_bwd" if reverse else "lru_fwd",
    )(*ins)


def _merge_kernel(x_ref, at_ref, yl_ref, ga_ref, gr_ref, wa_ref, wl_ref, wo_ref, gm_ref, wr_ref, br_ref,
                  h_ref, u_ref, idx_ref, gate_ref, cnt_ref):
    y_attn = _bdot(at_ref[...], wa_ref[...])
    y_lru = _bdot(yl_ref[...], wl_ref[...])
    merged = _sigmoid(ga_ref[...]) * y_attn + _sigmoid(gr_ref[...]) * y_lru
    h = x_ref[...] + _bdot(merged.astype(BF16), wo_ref[...])
    h_ref[...] = h
    u = _rms(h, gm_ref[...])
    u_ref[...] = u
    logits = jnp.dot(u, wr_ref[...], preferred_element_type=F32, precision=lax.Precision.HIGHEST)
    lane = lax.broadcasted_iota(jnp.int32, logits.shape, 1)
    work = jnp.where(lane < N_EXPERTS, logits + br_ref[...], -jnp.inf)
    vals, hits = [], jnp.zeros(logits.shape, F32)
    for kk in range(TOP_K):
        m = jnp.max(work, axis=-1, keepdims=True)
        idx = jnp.min(jnp.where(work == m, lane, V7X_LANES), axis=-1, keepdims=True)
        pick = lane == idx
        work = jnp.where(pick, -jnp.inf, work)
        hits = hits + pick.astype(F32)
        vals.append(m)
        idx_ref[:, kk:kk + 1] = idx
    es = [jnp.exp(vv - vals[0]) for vv in vals]
    inv = 1.0 / sum(es)
    for kk in range(TOP_K):
        gate_ref[:, kk:kk + 1] = es[kk] * inv

    @pl.when((pl.program_id(0) == 0) & (pl.program_id(1) == 0))
    def _():
        cnt_ref[...] = jnp.zeros_like(cnt_ref)

    cnt_ref[0:1, :] += jnp.sum(hits, axis=0, keepdims=True)


def _merge(x, attn, ylru, ga, gr, wa, wl, wo, g_moe, wr, br):
    B, S, D = x.shape
    tm = TOKEN_TILE
    bm = lambda w: pl.BlockSpec((None, tm, w), lambda b, i: (b, i, 0))
    full = lambda a: pl.BlockSpec(a.shape, lambda b, i: (0,) * a.ndim)
    return pl.pallas_call(
        _merge_kernel,
        grid=(B, S // tm),
        in_specs=[bm(D), bm(Q_W), bm(LRU_WIDTH), bm(D), bm(D), full(wa), full(wl), full(wo),
                  full(g_moe), full(wr), full(br)],
        out_specs=[bm(D), bm(D), bm(TOP_K), bm(TOP_K), pl.BlockSpec((8, V7X_LANES), lambda b, i: (0, 0))],
        out_shape=[
            jax.ShapeDtypeStruct((B, S, D), F32),
            jax.ShapeDtypeStruct((B, S, D), F32),
            jax.ShapeDtypeStruct((B, S, TOP_K), jnp.int32),
            jax.ShapeDtypeStruct((B, S, TOP_K), F32),
            jax.ShapeDtypeStruct((8, V7X_LANES), F32),
        ],
        compiler_params=_params("arbitrary", "arbitrary"),
        name="merge_router",
    )(x, attn, ylru, ga, gr, wa, wl, wo, g_moe, wr, br)


def _rank_kernel(idx_ref, ps_ref, dest_ref, carry_ref):
    @pl.when(pl.program_id(0) == 0)
    def _():
        carry_ref[...] = jnp.zeros_like(carry_ref)

    tm = idx_ref.shape[0]
    lane = lax.broadcasted_iota(jnp.int32, (tm, V7X_LANES), 1)
    picks = [lane == idx_ref[:, kk:kk + 1] for kk in range(TOP_K)]
    hits = sum(p.astype(F32) for p in picks)
    earlier = (lax.broadcasted_iota(jnp.int32, (tm, tm), 0) >
               lax.broadcasted_iota(jnp.int32, (tm, tm), 1)).astype(BF16)
    base = _bdot(earlier, hits.astype(BF16)) + carry_ref[...] + ps_ref[...]
    for kk in range(TOP_K):
        dest_ref[:, kk:kk + 1] = jnp.sum(jnp.where(picks[kk], base, 0.0), axis=-1,
                                         keepdims=True).astype(jnp.int32)
    carry_ref[...] += jnp.sum(hits, axis=0, keepdims=True)


def _rank(idx, pad_start_row):
    T = idx.shape[0]
    tm = RANK_TILE
    return pl.pallas_call(
        _rank_kernel,
        grid=(T // tm,),
        in_specs=[pl.BlockSpec((tm, TOP_K), lambda i: (i, 0)),
                  pl.BlockSpec((1, V7X_LANES), lambda i: (0, 0))],
        out_specs=pl.BlockSpec((tm, TOP_K), lambda i: (i, 0)),
        out_shape=jax.ShapeDtypeStruct((T, TOP_K), jnp.int32),
        scratch_shapes=[pltpu.VMEM((1, V7X_LANES), F32)],
        compiler_params=_params("arbitrary"),
        name="slot_rank",
    )(idx, pad_start_row)


def _dispatch_kernel(pad_end_ref, padded_ref, dest_ref, u_ref, xs_hbm, zero_ref, sem, zsem):
    step = pl.program_id(0)

    def zero_copy(e):
        start = pl.multiple_of(pad_end_ref[e] - EXPERT_TILE, EXPERT_TILE)
        return pltpu.make_async_copy(zero_ref, xs_hbm.at[pl.ds(start, EXPERT_TILE)], zsem)

    @pl.when(step == 0)
    def _():
        zero_ref[...] = jnp.zeros_like(zero_ref)
        for e in range(N_EXPERTS):
            @pl.when(padded_ref[e] > 0)
            def _():
                zero_copy(e).start()
        for e in range(N_EXPERTS):
            @pl.when(padded_ref[e] > 0)
            def _():
                zero_copy(e).wait()

    def row_copy(j, slot):
        return pltpu.make_async_copy(u_ref.at[pl.ds(j, 1)], xs_hbm.at[pl.ds(slot, 1)], sem)

    def issue(j, carry):
        for kk in range(TOP_K):
            row_copy(j, dest_ref[j * TOP_K + kk]).start()
        return carry

    def drain(j, carry):
        for kk in range(TOP_K):
            row_copy(0, 0).wait()
        return carry

    lax.fori_loop(0, DISPATCH_TOKENS, issue, 0)
    lax.fori_loop(0, DISPATCH_TOKENS, drain, 0)


def _dispatch(pad_end, padded, dest_flat, u, cap):
    T, D = u.shape
    n = DISPATCH_TOKENS * TOP_K
    return pl.pallas_call(
        _dispatch_kernel,
        grid_spec=pltpu.PrefetchScalarGridSpec(
            num_scalar_prefetch=2,
            grid=(T // DISPATCH_TOKENS,),
            in_specs=[pl.BlockSpec((n,), lambda i, pe, pd: (i,), memory_space=pltpu.SMEM),
                      pl.BlockSpec((DISPATCH_TOKENS, D), lambda i, pe, pd: (i, 0))],
            out_specs=pl.BlockSpec(memory_space=pl.ANY),
            scratch_shapes=[pltpu.VMEM((EXPERT_TILE, D), F32),
                            pltpu.SemaphoreType.DMA(()), pltpu.SemaphoreType.DMA(())],
        ),
        out_shape=jax.ShapeDtypeStruct((cap, D), F32),
        compiler_params=_params("arbitrary"),
        name="dispatch",
    )(pad_end, padded, dest_flat, u)


def _expert_kernel(be_ref, nu_ref, xs_ref, wgu_ref, bgu_ref, wdn_ref, bdn_ref, ys_ref):
    @pl.when(pl.program_id(0) < nu_ref[0])
    def _():
        hu = _bdot(xs_ref[...].astype(BF16), wgu_ref[...].astype(BF16)) + bgu_ref[...]
        gate = jnp.minimum(hu[:, :D_EXPERT], SWIGLU_LIMIT)
        up = jnp.clip(hu[:, D_EXPERT:], -SWIGLU_LIMIT, SWIGLU_LIMIT)
        act = gate * _sigmoid(SWIGLU_ALPHA * gate)
        ys_ref[...] = _bdot(((up + 1.0) * act).astype(BF16), wdn_ref[...].astype(BF16)) + bdn_ref[...]


def _experts(blk_exp, n_used, xs, w_gu, b_gu, w_dn, b_dn):
    cap, D = xs.shape
    row = lambda j, be, nu: (jnp.minimum(j, nu[0] - 1), 0)
    exp3 = lambda j, be, nu: (be[j], 0, 0)
    return pl.pallas_call(
        _expert_kernel,
        grid_spec=pltpu.PrefetchScalarGridSpec(
            num_scalar_prefetch=2,
            grid=(cap // EXPERT_TILE,),
            in_specs=[pl.BlockSpec((EXPERT_TILE, D), row),
                      pl.BlockSpec((None, D, 2 * D_EXPERT), exp3),
                      pl.BlockSpec((None, 1, 2 * D_EXPERT), exp3),
                      pl.BlockSpec((None, D_EXPERT, D), exp3),
                      pl.BlockSpec((None, 1, D), exp3)],
            out_specs=pl.BlockSpec((EXPERT_TILE, D), row),
        ),
        out_shape=jax.ShapeDtypeStruct((cap, D), F32),
        compiler_params=_params("arbitrary"),
        name="experts",
    )(blk_exp, n_used, xs, w_gu, b_gu, w_dn, b_dn)


def _combine_kernel(dest_ref, ys_hbm, gate_ref, h_ref, p_ref, gp_ref, wg_ref, wp_ref, o_ref, buf_ref, sem):
    def row_copy(j, kk, slot):
        return pltpu.make_async_copy(ys_hbm.at[pl.ds(slot, 1)], buf_ref.at[kk, pl.ds(j, 1)], sem)

    def issue(j, carry):
        for kk in range(TOP_K):
            row_copy(j, kk, dest_ref[j * TOP_K + kk]).start()
        return carry

    def drain(j, carry):
        for kk in range(TOP_K):
            row_copy(j, kk, 0).wait()
        return carry

    lax.fori_loop(0, COMBINE_TOKENS, issue, 0)
    lax.fori_loop(0, COMBINE_TOKENS, drain, 0)
    h = h_ref[...]
    for kk in range(TOP_K):
        h = h + gate_ref[:, kk:kk + 1] * buf_ref[kk]
    u = _rms(h, gp_ref[...]).astype(BF16)
    o_ref[...] = h + _sigmoid(_bdot(u, wg_ref[...])) * _bdot(p_ref[...].astype(BF16), wp_ref[...])


def _combine(dest_flat, ys, gate, h, p, g_ple, wg, wp):
    T, D = h.shape
    tm = COMBINE_TOKENS
    full = lambda a: pl.BlockSpec(a.shape, lambda i: (0,) * a.ndim)
    return pl.pallas_call(
        _combine_kernel,
        grid=(T // tm,),
        in_specs=[pl.BlockSpec((tm * TOP_K,), lambda i: (i,), memory_space=pltpu.SMEM),
                  pl.BlockSpec(memory_space=pl.ANY),
                  pl.BlockSpec((tm, TOP_K), lambda i: (i, 0)),
                  pl.BlockSpec((tm, D), lambda i: (i, 0)),
                  pl.BlockSpec((tm, PLE_DIM), lambda i: (i, 0)),
                  full(g_ple), full(wg), full(wp)],
        out_specs=pl.BlockSpec((tm, D), lambda i: (i, 0)),
        out_shape=jax.ShapeDtypeStruct((T, D), F32),
        scratch_shapes=[pltpu.VMEM((TOP_K, tm, D), F32), pltpu.SemaphoreType.DMA(())],
        compiler_params=_params("arbitrary"),
        name="combine_ple",
    )(dest_flat, ys, gate, h, p, g_ple, wg, wp)


def _rope_tables(seq_len):
    pos = jnp.arange(seq_len)
    axis_dim = HEAD_DIM // 2
    inv_freq = ROPE_THETA ** (-jnp.arange(0, axis_dim, 2, dtype=F32) / axis_dim)
    ang_r = (pos // GRID_W).astype(F32)[:, None] * inv_freq[None, :]
    ang_c = (pos % GRID_W).astype(F32)[:, None] * inv_freq[None, :]
    zero = jnp.zeros_like(ang_r)
    cos = jnp.concatenate([jnp.cos(ang_r)] * 2 + [jnp.cos(ang_c)] * 2, axis=-1)
    sina = jnp.concatenate([-jnp.sin(ang_r), zero, -jnp.sin(ang_c), zero], axis=-1)
    sinb = jnp.concatenate([zero, jnp.sin(ang_r), zero, jnp.sin(ang_c)], axis=-1)
    return cos, sina, sinb


def _layer(h, p, g_mix, w_in, q_norm, k_norm, conv_w, conv_b, lru_wa, lru_ba, lru_wi, lru_bi, lru_lam,
           w_attn_br, w_lru_br, w_out, g_moe, w_router, b_router, w_gu, b_gu, w_dn, b_dn,
           g_ple, w_ple_gate, w_ple_proj, rope):
    B, S, D = h.shape
    T = B * S
    row = lambda a: a.reshape(1, -1)
    q, k, v, xr, xg, ga, gr = _in_proj(
        h, row(g_mix), w_in.astype(BF16), row(q_norm) * HEAD_DIM ** -0.5, row(k_norm), *rope)
    attn = _attention(q, k, v)
    xr2 = xr.transpose(1, 0, 2).reshape(S * B, LRU_WIDTH)
    xg2 = xg.transpose(1, 0, 2).reshape(S * B, LRU_WIDTH)
    wcat = jnp.concatenate([lru_wa, lru_wi], axis=-1).astype(BF16)
    bcat = jnp.concatenate([lru_ba.reshape(2, LRU_BLOCKS, 1, LRU_BLOCK_DIM),
                            lru_bi.reshape(2, LRU_BLOCKS, 1, LRU_BLOCK_DIM)], axis=-1)
    h_fwd = _lru_scan(xr2, conv_w, row(conv_b), wcat[0], bcat[0], row(lru_lam[0]), batch=B, reverse=False)
    ylru = _lru_scan(xr2, conv_w, row(conv_b), wcat[1], bcat[1], row(lru_lam[1]), batch=B, reverse=True,
                     h_fwd=h_fwd, xg2=xg2).reshape(S, B, LRU_WIDTH).transpose(1, 0, 2)
    w_router_p = jnp.zeros((D, V7X_LANES), F32).at[:, :N_EXPERTS].set(w_router)
    b_router_p = jnp.zeros((1, V7X_LANES), F32).at[0, :N_EXPERTS].set(b_router)
    h1, u2, idx, gate, cnt = _merge(h, attn, ylru, ga, gr, w_attn_br.astype(BF16), w_lru_br.astype(BF16),
                                    w_out.astype(BF16), row(g_moe), w_router_p, b_router_p)
    cap = T * TOP_K + N_EXPERTS * EXPERT_TILE
    counts = cnt[0, :N_EXPERTS].astype(jnp.int32)
    padded = ((counts + EXPERT_TILE - 1) // EXPERT_TILE) * EXPERT_TILE
    pad_end = jnp.cumsum(padded)
    pad_start = pad_end - padded
    blk_start = jnp.arange(cap // EXPERT_TILE, dtype=jnp.int32) * EXPERT_TILE
    blk_exp = jnp.minimum(jnp.sum(blk_start[:, None] >= pad_end[None, :], axis=1), N_EXPERTS - 1)
    n_used = (pad_end[-1:] // EXPERT_TILE).astype(jnp.int32)
    ps_row = jnp.zeros((1, V7X_LANES), F32).at[0, :N_EXPERTS].set(pad_start.astype(F32))
    dest = _rank(idx.reshape(T, TOP_K), ps_row).reshape(T * TOP_K)
    xs = _dispatch(pad_end.astype(jnp.int32), padded.astype(jnp.int32), dest, u2.reshape(T, D), cap)
    ys = _experts(blk_exp.astype(jnp.int32), n_used, xs, w_gu, b_gu.reshape(N_EXPERTS, 1, -1),
                  w_dn, b_dn.reshape(N_EXPERTS, 1, -1))
    out = _combine(dest, ys, gate.reshape(T, TOP_K), h1.reshape(T, D), p.reshape(T, PLE_DIM),
                   row(g_ple), w_ple_gate.astype(BF16), w_ple_proj.astype(BF16))
    return out.reshape(B, S, D)


def kernel(x, p, g_mix, w_in, q_norm, k_norm, conv_w, conv_b, lru_wa, lru_ba, lru_wi, lru_bi, lru_lam,
           w_attn_br, w_lru_br, w_out, g_moe, w_router, b_router, w_gu, b_gu, w_dn, b_dn,
           g_ple, w_ple_gate, w_ple_proj):
    rope = _rope_tables(x.shape[1])
    h = x
    for l in range(p.shape[0]):
        h = _layer(h, p[l], g_mix[l], w_in[l], q_norm[l], k_norm[l], conv_w[l], conv_b[l], lru_wa[l],
                   lru_ba[l], lru_wi[l], lru_bi[l], lru_lam[l], w_attn_br[l], w_lru_br[l], w_out[l],
                   g_moe[l], w_router[l], b_router[l], w_gu[l], b_gu[l], w_dn[l], b_dn[l],
                   g_ple[l], w_ple_gate[l], w_ple_proj[l], rope)
    return h
```

```python
import functools

import jax
import jax.numpy as jnp
import numpy as np
from jax import lax
from jax.experimental import pallas as pl
from jax.experimental.pallas import tpu as pltpu

D_MODEL = 1024
N_Q_HEADS = 8
N_KV_HEADS = 2
HEAD_DIM = 128
Q_GROUP = N_Q_HEADS // N_KV_HEADS
ROPE_THETA = 10000.0
GRID_W = 64
LRU_WIDTH = 1024
LRU_BLOCKS = 8
LRU_BLOCK_DIM = LRU_WIDTH // LRU_BLOCKS
LRU_C = 8.0
N_EXPERTS = 32
TOP_K = 4
D_EXPERT = 1024
SWIGLU_LIMIT = 7.0
SWIGLU_ALPHA = 1.702
PLE_DIM = 256
EPS = 1e-6
Q_W = N_Q_HEADS * HEAD_DIM
KV_W = N_KV_HEADS * HEAD_DIM
GROUP_W = Q_GROUP * HEAD_DIM
IN_WIDTH = Q_W + 2 * KV_W + 2 * LRU_WIDTH + 2 * D_MODEL
OFF_Q = 0
OFF_K = OFF_Q + Q_W
OFF_V = OFF_K + KV_W
OFF_XR = OFF_V + KV_W
OFF_XG = OFF_XR + LRU_WIDTH
OFF_GA = OFF_XG + LRU_WIDTH
OFF_GR = OFF_GA + D_MODEL

V7X_LANES = 128
V7X_VMEM_LIMIT_BYTES = 56 * 1024 * 1024
TOKEN_TILE = 512
ATTN_Q_TILE = 256
LRU_STEPS = 32
EXPERT_TILE = 512
RANK_TILE = 512
INVERT_CHUNK = 8192
SPARE_ROWS = 1024

F32 = jnp.float32
BF16 = jnp.bfloat16


def _params(*sem):
    return pltpu.CompilerParams(dimension_semantics=sem, vmem_limit_bytes=V7X_VMEM_LIMIT_BYTES)


def _bdot(a, b):
    return jnp.dot(a, b, preferred_element_type=F32)


def _rms(x, g):
    return x * lax.rsqrt(jnp.mean(x * x, axis=-1, keepdims=True) + EPS) * g


def _sigmoid(x):
    return 1.0 / (1.0 + jnp.exp(-x))


def _in_proj_kernel(x_ref, g_ref, w_ref, qn_ref, kn_ref, cos_ref, sina_ref, sinb_ref,
                    q_ref, k_ref, v_ref, xr_ref, xg_ref, ga_ref, gr_ref):
    u = _rms(x_ref[...], g_ref[...]).astype(BF16)
    cos, sina, sinb = cos_ref[...], sina_ref[...], sinb_ref[...]

    def head(z, gain):
        y = _rms(z, gain)
        return y * cos + pltpu.roll(y, 96, 1) * sina + pltpu.roll(y, 32, 1) * sinb

    zq = _bdot(u, w_ref[:, OFF_Q:OFF_Q + Q_W])
    for h in range(N_Q_HEADS):
        sl = slice(h * HEAD_DIM, (h + 1) * HEAD_DIM)
        q_ref[:, sl] = head(zq[:, sl], qn_ref[...]).astype(BF16)
    zk = _bdot(u, w_ref[:, OFF_K:OFF_K + KV_W])
    for h in range(N_KV_HEADS):
        sl = slice(h * HEAD_DIM, (h + 1) * HEAD_DIM)
        k_ref[:, sl] = head(zk[:, sl], kn_ref[...]).astype(BF16)
    v_ref[...] = _bdot(u, w_ref[:, OFF_V:OFF_V + KV_W]).astype(BF16)
    xr_ref[...] = _bdot(u, w_ref[:, OFF_XR:OFF_XR + LRU_WIDTH])
    xg_ref[...] = _bdot(u, w_ref[:, OFF_XG:OFF_XG + LRU_WIDTH])
    ga_ref[...] = _bdot(u, w_ref[:, OFF_GA:OFF_GA + D_MODEL])
    gr_ref[...] = _bdot(u, w_ref[:, OFF_GR:OFF_GR + D_MODEL])


def _in_proj(x, g_mix, w_in, qn, kn, cos, sina, sinb):
    B, S, D = x.shape
    tm = TOKEN_TILE
    bm = lambda w: pl.BlockSpec((None, tm, w), lambda b, i: (b, i, 0))
    full = lambda a: pl.BlockSpec(a.shape, lambda b, i: (0,) * a.ndim)
    tab = pl.BlockSpec((tm, HEAD_DIM), lambda b, i: (i, 0))
    return pl.pallas_call(
        _in_proj_kernel,
        grid=(B, S // tm),
        in_specs=[bm(D), full(g_mix), full(w_in), full(qn), full(kn), tab, tab, tab],
        out_specs=[bm(Q_W), bm(KV_W), bm(KV_W), bm(LRU_WIDTH), bm(LRU_WIDTH), bm(D), bm(D)],
        out_shape=[
            jax.ShapeDtypeStruct((B, S, Q_W), BF16),
            jax.ShapeDtypeStruct((B, S, KV_W), BF16),
            jax.ShapeDtypeStruct((B, S, KV_W), BF16),
            jax.ShapeDtypeStruct((B, S, LRU_WIDTH), F32),
            jax.ShapeDtypeStruct((B, S, LRU_WIDTH), F32),
            jax.ShapeDtypeStruct((B, S, D), F32),
            jax.ShapeDtypeStruct((B, S, D), F32),
        ],
        compiler_params=_params("arbitrary", "arbitrary"),
        name="in_proj",
    )(x, g_mix, w_in, qn, kn, cos, sina, sinb)


def _attn_kernel(q_ref, k_ref, v_ref, o_ref):
    k = k_ref[...]
    v = v_ref[...]
    for h in range(Q_GROUP):
        sl = slice(h * HEAD_DIM, (h + 1) * HEAD_DIM)
        s = lax.dot_general(q_ref[:, sl], k, (((1,), (1,)), ((), ())), preferred_element_type=F32)
        p = jnp.exp(s - jnp.max(s, axis=-1, keepdims=True))
        l = jnp.sum(p, axis=-1, keepdims=True)
        o = _bdot(p.astype(BF16), v)
        o_ref[:, sl] = (o * (1.0 / l)).astype(BF16)


def _attention(q, k, v):
    B, S, _ = q.shape
    tq = ATTN_Q_TILE
    return pl.pallas_call(
        _attn_kernel,
        grid=(B, N_KV_HEADS, S // tq),
        in_specs=[
            pl.BlockSpec((None, tq, GROUP_W), lambda b, g, i: (b, i, g)),
            pl.BlockSpec((None, S, HEAD_DIM), lambda b, g, i: (b, 0, g)),
            pl.BlockSpec((None, S, HEAD_DIM), lambda b, g, i: (b, 0, g)),
        ],
        out_specs=pl.BlockSpec((None, tq, GROUP_W), lambda b, g, i: (b, i, g)),
        out_shape=jax.ShapeDtypeStruct((B, S, Q_W), BF16),
        compiler_params=_params("arbitrary", "arbitrary", "arbitrary"),
        name="attention",
    )(q, k, v)


def _gelu_tanh(x):
    return 0.5 * x * (1.0 + jnp.tanh(np.sqrt(2.0 / np.pi) * (x + 0.044715 * (x * x * x))))


def _lru_kernel(*refs, batch, reverse):
    if reverse:
        (xc_ref, xp_ref, xn_ref, cw_ref, cb_ref, w_ref, b_ref, lam_ref, hf_ref, xg_ref,
         o_ref, ext_ref, a_ref, bt_ref, h_ref) = refs
    else:
        (xc_ref, xp_ref, xn_ref, cw_ref, cb_ref, w_ref, b_ref, lam_ref,
         o_ref, ext_ref, a_ref, bt_ref, h_ref) = refs
    step = pl.program_id(0)
    nsteps = pl.num_programs(0)
    chunk = (nsteps - 1 - step) if reverse else step
    rows = LRU_STEPS * batch

    @pl.when(step == 0)
    def _():
        h_ref[...] = jnp.zeros_like(h_ref)

    ext_ref[0:2 * batch, :] = jnp.where(chunk > 0, xp_ref[...], 0.0)
    ext_ref[2 * batch:2 * batch + rows, :] = xc_ref[...]
    ext_ref[2 * batch + rows:3 * batch + rows, :] = jnp.where(chunk < nsteps - 1, xn_ref[...], 0.0)
    c = cb_ref[...] + sum(cw_ref[j:j + 1, :] * ext_ref[j * batch:j * batch + rows, :] for j in range(4))
    cb16 = c.astype(BF16)
    lam = lam_ref[...]
    neg_softplus = -(jnp.maximum(-lam, 0.0) + jnp.log1p(jnp.exp(-jnp.abs(lam))))
    for n in range(LRU_BLOCKS):
        sl = slice(n * LRU_BLOCK_DIM, (n + 1) * LRU_BLOCK_DIM)
        pre = _bdot(cb16[:, sl], w_ref[n]) + b_ref[n]
        r = _sigmoid(pre[:, :LRU_BLOCK_DIM])
        ig = _sigmoid(pre[:, LRU_BLOCK_DIM:])
        log_a = LRU_C * r * neg_softplus[:, sl]
        a = jnp.exp(log_a)
        a_ref[:, sl] = a
        bt_ref[:, sl] = jnp.sqrt(-jnp.tanh(log_a) * (a * a + 1.0)) * ig * c[:, sl]

    def scan_step(t, h):
        tt = (LRU_STEPS - 1 - t) if reverse else t
        rs = pl.ds(pl.multiple_of(tt * batch, batch), batch)
        h = a_ref[rs, :] * h + bt_ref[rs, :]
        if reverse:
            o_ref[rs, :] = ((h + hf_ref[rs, :]) * _gelu_tanh(xg_ref[rs, :])).astype(o_ref.dtype)
        else:
            o_ref[rs, :] = h
        return h

    h_ref[...] = lax.fori_loop(0, LRU_STEPS, scan_step, h_ref[...], unroll=4)


def _lru_scan(xr2, conv_w, conv_b, wcat, bcat, lam, *, batch, reverse, h_fwd=None, xg2=None):
    n_rows, width = xr2.shape
    rows = LRU_STEPS * batch
    n = n_rows // rows
    ch = (lambda i: n - 1 - i) if reverse else (lambda i: i)
    tile = pl.BlockSpec((rows, width), lambda i: (ch(i), 0))
    prev = pl.BlockSpec((2 * batch, width), lambda i: (jnp.maximum(ch(i) * (LRU_STEPS // 2) - 1, 0), 0))
    nxt = pl.BlockSpec((batch, width),
                       lambda i: (jnp.minimum((ch(i) + 1) * LRU_STEPS, n * LRU_STEPS - 1), 0))
    full = lambda a: pl.BlockSpec(a.shape, lambda i: (0,) * a.ndim)
    ins = [xr2, xr2, xr2, conv_w, conv_b, wcat, bcat, lam]
    specs = [tile, prev, nxt, full(conv_w), full(conv_b), full(wcat), full(bcat), full(lam)]
    if reverse:
        ins += [h_fwd, xg2]
        specs += [tile, tile]
    return pl.pallas_call(
        functools.partial(_lru_kernel, batch=batch, reverse=reverse),
        grid=(n,),
        in_specs=specs,
        out_specs=tile,
        out_shape=jax.ShapeDtypeStruct((n_rows, width), BF16 if reverse else F32),
        scratch_shapes=[
            pltpu.VMEM((rows + 3 * batch, width), F32),
            pltpu.VMEM((rows, width), F32),
            pltpu.VMEM((rows, width), F32),
            pltpu.VMEM((batch, width), F32),
        ],
        compiler_params=_params("arbitrary"),
        name="lru_bwd" if reverse else "lru_fwd",
    )(*ins)


def _merge_kernel(x_ref, at_ref, yl_ref, ga_ref, gr_ref, wa_ref, wl_ref, wo_ref, gm_ref, wr_ref, br_ref,
                  h_ref, u_ref, idx_ref, gate_ref, cnt_ref):
    y_attn = _bdot(at_ref[...], wa_ref[...])
    y_lru = _bdot(yl_ref[...], wl_ref[...])
    merged = _sigmoid(ga_ref[...]) * y_attn + _sigmoid(gr_ref[...]) * y_lru
    h = x_ref[...] + _bdot(merged.astype(BF16), wo_ref[...])
    h_ref[...] = h
    u = _rms(h, gm_ref[...])
    u_ref[...] = u
    logits = jnp.dot(u, wr_ref[...], preferred_element_type=F32, precision=lax.Precision.HIGHEST)
    lane = lax.broadcasted_iota(jnp.int32, logits.shape, 1)
    work = jnp.where(lane < N_EXPERTS, logits + br_ref[...], -jnp.inf)
    vals, hits = [], jnp.zeros(logits.shape, F32)
    for kk in range(TOP_K):
        m = jnp.max(work, axis=-1, keepdims=True)
        idx = jnp.min(jnp.where(work == m, lane, V7X_LANES), axis=-1, keepdims=True)
        pick = lane == idx
        work = jnp.where(pick, -jnp.inf, work)
        hits = hits + pick.astype(F32)
        vals.append(m)
        idx_ref[:, kk:kk + 1] = idx
    es = [jnp.exp(vv - vals[0]) for vv in vals]
    inv = 1.0 / sum(es)
    for kk in range(TOP_K):
        gate_ref[:, kk:kk + 1] = es[kk] * inv

    @pl.when((pl.program_id(0) == 0) & (pl.program_id(1) == 0))
    def _():
        cnt_ref[...] = jnp.zeros_like(cnt_ref)

    cnt_ref[0:1, :] += jnp.sum(hits, axis=0, keepdims=True)


def _merge(x, attn, ylru, ga, gr, wa, wl, wo, g_moe, wr, br):
    B, S, D = x.shape
    tm = TOKEN_TILE
    bm = lambda w: pl.BlockSpec((None, tm, w), lambda b, i: (b, i, 0))
    full = lambda a: pl.BlockSpec(a.shape, lambda b, i: (0,) * a.ndim)
    return pl.pallas_call(
        _merge_kernel,
        grid=(B, S // tm),
        in_specs=[bm(D), bm(Q_W), bm(LRU_WIDTH), bm(D), bm(D), full(wa), full(wl), full(wo),
                  full(g_moe), full(wr), full(br)],
        out_specs=[bm(D), bm(D), bm(TOP_K), bm(TOP_K), pl.BlockSpec((8, V7X_LANES), lambda b, i: (0, 0))],
        out_shape=[
            jax.ShapeDtypeStruct((B, S, D), F32),
            jax.ShapeDtypeStruct((B, S, D), F32),
            jax.ShapeDtypeStruct((B, S, TOP_K), jnp.int32),
            jax.ShapeDtypeStruct((B, S, TOP_K), F32),
            jax.ShapeDtypeStruct((8, V7X_LANES), F32),
        ],
        compiler_params=_params("arbitrary", "arbitrary"),
        name="merge_router",
    )(x, attn, ylru, ga, gr, wa, wl, wo, g_moe, wr, br)


def _rank_kernel(idx_ref, ps_ref, dest_ref, carry_ref):
    @pl.when(pl.program_id(0) == 0)
    def _():
        carry_ref[...] = jnp.zeros_like(carry_ref)

    tm = idx_ref.shape[0]
    lane = lax.broadcasted_iota(jnp.int32, (tm, V7X_LANES), 1)
    picks = [lane == idx_ref[:, kk:kk + 1] for kk in range(TOP_K)]
    hits = sum(p.astype(F32) for p in picks)
    earlier = (lax.broadcasted_iota(jnp.int32, (tm, tm), 0) >
               lax.broadcasted_iota(jnp.int32, (tm, tm), 1)).astype(BF16)
    base = _bdot(earlier, hits.astype(BF16)) + carry_ref[...] + ps_ref[...]
    for kk in range(TOP_K):
        dest_ref[:, kk:kk + 1] = jnp.sum(jnp.where(picks[kk], base, 0.0), axis=-1,
                                         keepdims=True).astype(jnp.int32)
    carry_ref[...] += jnp.sum(hits, axis=0, keepdims=True)


def _rank(idx, pad_start_row):
    T = idx.shape[0]
    tm = RANK_TILE
    return pl.pallas_call(
        _rank_kernel,
        grid=(T // tm,),
        in_specs=[pl.BlockSpec((tm, TOP_K), lambda i: (i, 0)),
                  pl.BlockSpec((1, V7X_LANES), lambda i: (0, 0))],
        out_specs=pl.BlockSpec((tm, TOP_K), lambda i: (i, 0)),
        out_shape=jax.ShapeDtypeStruct((T, TOP_K), jnp.int32),
        scratch_shapes=[pltpu.VMEM((1, V7X_LANES), F32)],
        compiler_params=_params("arbitrary"),
        name="slot_rank",
    )(idx, pad_start_row)


def _invert_kernel(pad_end_ref, count_ref, dest_ref, inv_ref, *, n_tokens, chunk):
    step = pl.program_id(0)
    log_t = n_tokens.bit_length() - 1

    @pl.when(step == 0)
    def _():
        def fill(s, carry):
            inv_ref[s] = TOP_K * n_tokens + (s & (SPARE_ROWS - 1))
            return carry

        for e in range(N_EXPERTS):
            lo = pad_end_ref[e] - ((count_ref[e] + EXPERT_TILE - 1) & -EXPERT_TILE) + count_ref[e]
            lax.fori_loop(lo, pad_end_ref[e], fill, 0)
        lax.fori_loop(pad_end_ref[N_EXPERTS - 1], inv_ref.shape[0], fill, 0)

    def body(i, carry):
        a = step * chunk + i
        inv_ref[dest_ref[i]] = ((a & (TOP_K - 1)) << log_t) | lax.shift_right_logical(a, 2)
        return carry

    lax.fori_loop(0, chunk, body, 0, unroll=8)


def _invert(pad_end, counts, dest_flat, cap, n_tokens):
    chunk = INVERT_CHUNK
    return pl.pallas_call(
        functools.partial(_invert_kernel, n_tokens=n_tokens, chunk=chunk),
        grid_spec=pltpu.PrefetchScalarGridSpec(
            num_scalar_prefetch=2,
            grid=(dest_flat.shape[0] // chunk,),
            in_specs=[pl.BlockSpec((chunk,), lambda i, pe, ct: (i,), memory_space=pltpu.SMEM)],
            out_specs=pl.BlockSpec(memory_space=pltpu.SMEM),
        ),
        out_shape=jax.ShapeDtypeStruct((cap,), jnp.int32),
        compiler_params=_params("arbitrary"),
        name="slot_invert",
    )(pad_end, counts, dest_flat)


def _expert_kernel(be_ref, nu_ref, invc_ref, invn_ref, u_hbm, wgu_ref, bgu_ref, wdn_ref, bdn_ref,
                   y_hbm, xbuf, ybuf, gsem, ssem, *, n_tokens):
    j = pl.program_id(0)
    n_used = nu_ref[0]
    parity = j & 1

    def gather(inv_ref, r, buf):
        tok = inv_ref[0, 0, r] & (n_tokens - 1)
        return pltpu.make_async_copy(u_hbm.at[pl.ds(tok, 1)], xbuf.at[buf, pl.ds(r, 1)], gsem.at[buf])

    def scatter(inv_ref, r, buf):
        return pltpu.make_async_copy(ybuf.at[buf, pl.ds(r, 1)], y_hbm.at[pl.ds(inv_ref[0, 0, r], 1)],
                                     ssem.at[buf])

    def wait_gather(buf):
        pltpu.make_async_copy(xbuf.at[buf], xbuf.at[buf], gsem.at[buf]).wait()

    def wait_scatter(buf):
        pltpu.make_async_copy(ybuf.at[buf], ybuf.at[buf], ssem.at[buf]).wait()

    @pl.when(j == 0)
    def _():
        def first(r, carry):
            gather(invc_ref, r, 0).start()
            return carry
        lax.fori_loop(0, EXPERT_TILE, first, 0)
        xbuf[1] = jnp.zeros((EXPERT_TILE, xbuf.shape[2]), xbuf.dtype)
        clears = [pltpu.make_async_copy(
            xbuf.at[1], y_hbm.at[pl.ds(TOP_K * n_tokens + c * EXPERT_TILE, EXPERT_TILE)], ssem.at[1])
            for c in range(SPARE_ROWS // EXPERT_TILE)]
        for c in clears:
            c.start()
        for c in clears:
            c.wait()

    def block(cur):
        oth = 1 - cur
        wait_gather(cur)

        @pl.when(j > 1)
        def _():
            wait_scatter(cur)

        for r in range(EXPERT_TILE):
            gather(invn_ref, r, oth).start(priority=r % 2)
        hu = _bdot(xbuf[cur].astype(BF16), wgu_ref[...].astype(BF16)) + bgu_ref[...]
        gate = jnp.minimum(hu[:, :D_EXPERT], SWIGLU_LIMIT)
        up = jnp.clip(hu[:, D_EXPERT:], -SWIGLU_LIMIT, SWIGLU_LIMIT)
        act = gate * _sigmoid(SWIGLU_ALPHA * gate)
        ybuf[cur] = _bdot(((up + 1.0) * act).astype(BF16), wdn_ref[...].astype(BF16)) + bdn_ref[...]
        for r in range(EXPERT_TILE):
            scatter(invc_ref, r, cur).start(priority=r % 2)

        @pl.when(j == n_used - 1)
        def _():
            wait_gather(oth)
            wait_scatter(cur)

            @pl.when(j > 0)
            def _():
                wait_scatter(oth)

    for cur in range(2):
        @pl.when((j < n_used) & (parity == cur))
        def _():
            block(cur)


def _experts(blk_exp, n_used, inv3, u, w_gu, b_gu, w_dn, b_dn):
    T, D = u.shape
    n_blocks = inv3.shape[0]
    exp3 = lambda j, be, nu: (be[j], 0, 0)
    inv_spec = lambda f: pl.BlockSpec((1, 1, EXPERT_TILE), lambda j, be, nu: (f(j, nu), 0, 0),
                                      memory_space=pltpu.SMEM)
    last = lambda j, nu: jnp.minimum(j, nu[0] - 1)
    return pl.pallas_call(
        functools.partial(_expert_kernel, n_tokens=T),
        grid_spec=pltpu.PrefetchScalarGridSpec(
            num_scalar_prefetch=2,
            grid=(n_blocks,),
            in_specs=[inv_spec(last),
                      inv_spec(lambda j, nu: jnp.minimum(j + 1, nu[0] - 1)),
                      pl.BlockSpec(memory_space=pl.ANY),
                      pl.BlockSpec((None, D, 2 * D_EXPERT), exp3),
                      pl.BlockSpec((None, 1, 2 * D_EXPERT), exp3),
                      pl.BlockSpec((None, D_EXPERT, D), exp3),
                      pl.BlockSpec((None, 1, D), exp3)],
            out_specs=pl.BlockSpec(memory_space=pl.ANY),
            scratch_shapes=[pltpu.VMEM((2, EXPERT_TILE, D), F32), pltpu.VMEM((2, EXPERT_TILE, D), F32),
                            pltpu.SemaphoreType.DMA((2,)), pltpu.SemaphoreType.DMA((2,))],
        ),
        out_shape=jax.ShapeDtypeStruct((TOP_K * T + SPARE_ROWS, D), F32),
        compiler_params=_params("arbitrary"),
        name="experts",
    )(blk_exp, n_used, inv3, inv3, u, w_gu, b_gu, w_dn, b_dn)


def _combine_kernel(y0_ref, y1_ref, y2_ref, y3_ref, gate_ref, h_ref, p_ref, gp_ref, wg_ref, wp_ref, o_ref):
    h = h_ref[...]
    for kk, y_ref in enumerate((y0_ref, y1_ref, y2_ref, y3_ref)):
        h = h + gate_ref[:, kk:kk + 1] * y_ref[...]
    u = _rms(h, gp_ref[...]).astype(BF16)
    o_ref[...] = h + _sigmoid(_bdot(u, wg_ref[...])) * _bdot(p_ref[...].astype(BF16), wp_ref[...])


def _combine(y4, gate, h, p, g_ple, wg, wp):
    T, D = h.shape
    tm = TOKEN_TILE
    full = lambda a: pl.BlockSpec(a.shape, lambda i: (0,) * a.ndim)
    plane = lambda kk: pl.BlockSpec((tm, D), lambda i: (kk * (T // tm) + i, 0))
    return pl.pallas_call(
        _combine_kernel,
        grid=(T // tm,),
        in_specs=[plane(0), plane(1), plane(2), plane(3),
                  pl.BlockSpec((tm, TOP_K), lambda i: (i, 0)),
                  pl.BlockSpec((tm, D), lambda i: (i, 0)),
                  pl.BlockSpec((tm, PLE_DIM), lambda i: (i, 0)),
                  full(g_ple), full(wg), full(wp)],
        out_specs=pl.BlockSpec((tm, D), lambda i: (i, 0)),
        out_shape=jax.ShapeDtypeStruct((T, D), F32),
        compiler_params=_params("arbitrary"),
        name="combine_ple",
    )(y4, y4, y4, y4, gate, h, p, g_ple, wg, wp)


def _rope_tables(seq_len):
    pos = jnp.arange(seq_len)
    axis_dim = HEAD_DIM // 2
    inv_freq = ROPE_THETA ** (-jnp.arange(0, axis_dim, 2, dtype=F32) / axis_dim)
    ang_r = (pos // GRID_W).astype(F32)[:, None] * inv_freq[None, :]
    ang_c = (pos % GRID_W).astype(F32)[:, None] * inv_freq[None, :]
    zero = jnp.zeros_like(ang_r)
    cos = jnp.concatenate([jnp.cos(ang_r)] * 2 + [jnp.cos(ang_c)] * 2, axis=-1)
    sina = jnp.concatenate([-jnp.sin(ang_r), zero, -jnp.sin(ang_c), zero], axis=-1)
    sinb = jnp.concatenate([zero, jnp.sin(ang_r), zero, jnp.sin(ang_c)], axis=-1)
    return cos, sina, sinb


def _layer(h, p, g_mix, w_in, q_norm, k_norm, conv_w, conv_b, lru_wa, lru_ba, lru_wi, lru_bi, lru_lam,
           w_attn_br, w_lru_br, w_out, g_moe, w_router, b_router, w_gu, b_gu, w_dn, b_dn,
           g_ple, w_ple_gate, w_ple_proj, rope):
    B, S, D = h.shape
    T = B * S
    row = lambda a: a.reshape(1, -1)
    q, k, v, xr, xg, ga, gr = _in_proj(
        h, row(g_mix), w_in.astype(BF16), row(q_norm) * HEAD_DIM ** -0.5, row(k_norm), *rope)
    attn = _attention(q, k, v)
    xr2 = xr.transpose(1, 0, 2).reshape(S * B, LRU_WIDTH)
    xg2 = xg.transpose(1, 0, 2).reshape(S * B, LRU_WIDTH)
    wcat = jnp.concatenate([lru_wa, lru_wi], axis=-1).astype(BF16)
    bcat = jnp.concatenate([lru_ba.reshape(2, LRU_BLOCKS, 1, LRU_BLOCK_DIM),
                            lru_bi.reshape(2, LRU_BLOCKS, 1, LRU_BLOCK_DIM)], axis=-1)
    h_fwd = _lru_scan(xr2, conv_w, row(conv_b), wcat[0], bcat[0], row(lru_lam[0]), batch=B, reverse=False)
    ylru = _lru_scan(xr2, conv_w, row(conv_b), wcat[1], bcat[1], row(lru_lam[1]), batch=B, reverse=True,
                     h_fwd=h_fwd, xg2=xg2).reshape(S, B, LRU_WIDTH).transpose(1, 0, 2)
    w_router_p = jnp.zeros((D, V7X_LANES), F32).at[:, :N_EXPERTS].set(w_router)
    b_router_p = jnp.zeros((1, V7X_LANES), F32).at[0, :N_EXPERTS].set(b_router)
    h1, u2, idx, gate, cnt = _merge(h, attn, ylru, ga, gr, w_attn_br.astype(BF16), w_lru_br.astype(BF16),
                                    w_out.astype(BF16), row(g_moe), w_router_p, b_router_p)
    cap = T * TOP_K + N_EXPERTS * EXPERT_TILE
    counts = cnt[0, :N_EXPERTS].astype(jnp.int32)
    padded = ((counts + EXPERT_TILE - 1) // EXPERT_TILE) * EXPERT_TILE
    pad_end = jnp.cumsum(padded)
    pad_start = pad_end - padded
    blk_start = jnp.arange(cap // EXPERT_TILE, dtype=jnp.int32) * EXPERT_TILE
    blk_exp = jnp.minimum(jnp.sum(blk_start[:, None] >= pad_end[None, :], axis=1), N_EXPERTS - 1)
    n_used = (pad_end[-1:] // EXPERT_TILE).astype(jnp.int32)
    ps_row = jnp.zeros((1, V7X_LANES), F32).at[0, :N_EXPERTS].set(pad_start.astype(F32))
    dest = _rank(idx.reshape(T, TOP_K), ps_row).reshape(T * TOP_K)
    assert T & (T - 1) == 0, "token count must be a power of two for the slot encoding"
    inv3 = _invert(pad_end.astype(jnp.int32), counts, dest, cap, T).reshape(cap // EXPERT_TILE, 1, EXPERT_TILE)
    y4 = _experts(blk_exp.astype(jnp.int32), n_used, inv3, u2.reshape(T, D), w_gu,
                  b_gu.reshape(N_EXPERTS, 1, -1), w_dn, b_dn.reshape(N_EXPERTS, 1, -1))
    out = _combine(y4, gate.reshape(T, TOP_K), h1.reshape(T, D),
                   p.reshape(T, PLE_DIM), row(g_ple), w_ple_gate.astype(BF16), w_ple_proj.astype(BF16))
    return out.reshape(B, S, D)


def kernel(x, p, g_mix, w_in, q_norm, k_norm, conv_w, conv_b, lru_wa, lru_ba, lru_wi, lru_bi, lru_lam,
           w_attn_br, w_lru_br, w_out, g_moe, w_router, b_router, w_gu, b_gu, w_dn, b_dn,
           g_ple, w_ple_gate, w_ple_proj):
    rope = _rope_tables(x.shape[1])
    h = x
    for l in range(p.shape[0]):
        h = _layer(h, p[l], g_mix[l], w_in[l], q_norm[l], k_norm[l], conv_w[l], conv_b[l], lru_wa[l],
                   lru_ba[l], lru_wi[l], lru_bi[l], lru_lam[l], w_attn_br[l], w_lru_br[l], w_out[l],
                   g_moe[l], w_router[l], b_router[l], w_gu[l], b_gu[l], w_dn[l], b_dn[l],
                   g_ple[l], w_ple_gate[l], w_ple_proj[l], rope)
    return h
```

```python
import functools

import jax
import jax.numpy as jnp
import numpy as np
from jax import lax
from jax.experimental import pallas as pl
from jax.experimental.pallas import tpu as pltpu

D_MODEL = 1024
N_Q_HEADS = 8
N_KV_HEADS = 2
HEAD_DIM = 128
Q_GROUP = N_Q_HEADS // N_KV_HEADS
ROPE_THETA = 10000.0
GRID_W = 64
LRU_WIDTH = 1024
LRU_BLOCKS = 8
LRU_BLOCK_DIM = LRU_WIDTH // LRU_BLOCKS
LRU_C = 8.0
N_EXPERTS = 32
TOP_K = 4
D_EXPERT = 1024
SWIGLU_LIMIT = 7.0
SWIGLU_ALPHA = 1.702
PLE_DIM = 256
EPS = 1e-6
Q_W = N_Q_HEADS * HEAD_DIM
KV_W = N_KV_HEADS * HEAD_DIM
GROUP_W = Q_GROUP * HEAD_DIM
IN_WIDTH = Q_W + 2 * KV_W + 2 * LRU_WIDTH + 2 * D_MODEL
OFF_Q = 0
OFF_K = OFF_Q + Q_W
OFF_V = OFF_K + KV_W
OFF_XR = OFF_V + KV_W
OFF_XG = OFF_XR + LRU_WIDTH
OFF_GA = OFF_XG + LRU_WIDTH
OFF_GR = OFF_GA + D_MODEL

V7X_LANES = 128
V7X_VMEM_LIMIT_BYTES = 56 * 1024 * 1024
TOKEN_TILE = 512
ATTN_Q_TILE = 256
LRU_STEPS = 32
EXPERT_TILE = 512
RANK_TILE = 512
INVERT_CHUNK = 8192
SPARE_ROWS = 1024

F32 = jnp.float32
BF16 = jnp.bfloat16


def _params(*sem):
    return pltpu.CompilerParams(dimension_semantics=sem, vmem_limit_bytes=V7X_VMEM_LIMIT_BYTES)


def _bdot(a, b):
    return jnp.dot(a, b, preferred_element_type=F32)


def _rms(x, g):
    return x * lax.rsqrt(jnp.mean(x * x, axis=-1, keepdims=True) + EPS) * g


def _sigmoid(x):
    return 0.5 * jnp.tanh(0.5 * x) + 0.5


def _in_proj_kernel(x_ref, g_ref, w_ref, qn_ref, kn_ref, cos_ref, sina_ref, sinb_ref,
                    q_ref, k_ref, v_ref, xr_ref, xg_ref, ga_ref, gr_ref):
    u = _rms(x_ref[...], g_ref[...]).astype(BF16)
    cos, sina, sinb = cos_ref[...], sina_ref[...], sinb_ref[...]

    def head(z, gain):
        y = _rms(z, gain)
        return y * cos + pltpu.roll(y, 96, 1) * sina + pltpu.roll(y, 32, 1) * sinb

    zq = _bdot(u, w_ref[:, OFF_Q:OFF_Q + Q_W])
    for h in range(N_Q_HEADS):
        sl = slice(h * HEAD_DIM, (h + 1) * HEAD_DIM)
        q_ref[:, sl] = head(zq[:, sl], qn_ref[...]).astype(BF16)
    zk = _bdot(u, w_ref[:, OFF_K:OFF_K + KV_W])
    for h in range(N_KV_HEADS):
        sl = slice(h * HEAD_DIM, (h + 1) * HEAD_DIM)
        k_ref[:, sl] = head(zk[:, sl], kn_ref[...]).astype(BF16)
    v_ref[...] = _bdot(u, w_ref[:, OFF_V:OFF_V + KV_W]).astype(BF16)
    xr_ref[...] = _bdot(u, w_ref[:, OFF_XR:OFF_XR + LRU_WIDTH])
    xg_ref[...] = _bdot(u, w_ref[:, OFF_XG:OFF_XG + LRU_WIDTH])
    ga_ref[...] = _bdot(u, w_ref[:, OFF_GA:OFF_GA + D_MODEL])
    gr_ref[...] = _bdot(u, w_ref[:, OFF_GR:OFF_GR + D_MODEL])


def _in_proj(x, g_mix, w_in, qn, kn, cos, sina, sinb):
    B, S, D = x.shape
    tm = TOKEN_TILE
    bm = lambda w: pl.BlockSpec((None, tm, w), lambda b, i: (b, i, 0))
    full = lambda a: pl.BlockSpec(a.shape, lambda b, i: (0,) * a.ndim)
    tab = pl.BlockSpec((tm, HEAD_DIM), lambda b, i: (i, 0))
    return pl.pallas_call(
        _in_proj_kernel,
        grid=(B, S // tm),
        in_specs=[bm(D), full(g_mix), full(w_in), full(qn), full(kn), tab, tab, tab],
        out_specs=[bm(Q_W), bm(KV_W), bm(KV_W), bm(LRU_WIDTH), bm(LRU_WIDTH), bm(D), bm(D)],
        out_shape=[
            jax.ShapeDtypeStruct((B, S, Q_W), BF16),
            jax.ShapeDtypeStruct((B, S, KV_W), BF16),
            jax.ShapeDtypeStruct((B, S, KV_W), BF16),
            jax.ShapeDtypeStruct((B, S, LRU_WIDTH), F32),
            jax.ShapeDtypeStruct((B, S, LRU_WIDTH), F32),
            jax.ShapeDtypeStruct((B, S, D), F32),
            jax.ShapeDtypeStruct((B, S, D), F32),
        ],
        compiler_params=_params("arbitrary", "arbitrary"),
        name="in_proj",
    )(x, g_mix, w_in, qn, kn, cos, sina, sinb)


def _attn_kernel(q_ref, k_ref, v_ref, o_ref):
    k = k_ref[...]
    v = v_ref[...]
    for h in range(Q_GROUP):
        sl = slice(h * HEAD_DIM, (h + 1) * HEAD_DIM)
        s = lax.dot_general(q_ref[:, sl], k, (((1,), (1,)), ((), ())), preferred_element_type=F32)
        p = jnp.exp(s - jnp.max(s, axis=-1, keepdims=True))
        l = jnp.sum(p, axis=-1, keepdims=True)
        o = _bdot(p.astype(BF16), v)
        o_ref[:, sl] = (o * (1.0 / l)).astype(BF16)


def _attention(q, k, v):
    B, S, _ = q.shape
    tq = ATTN_Q_TILE
    return pl.pallas_call(
        _attn_kernel,
        grid=(B, N_KV_HEADS, S // tq),
        in_specs=[
            pl.BlockSpec((None, tq, GROUP_W), lambda b, g, i: (b, i, g)),
            pl.BlockSpec((None, S, HEAD_DIM), lambda b, g, i: (b, 0, g)),
            pl.BlockSpec((None, S, HEAD_DIM), lambda b, g, i: (b, 0, g)),
        ],
        out_specs=pl.BlockSpec((None, tq, GROUP_W), lambda b, g, i: (b, i, g)),
        out_shape=jax.ShapeDtypeStruct((B, S, Q_W), BF16),
        compiler_params=_params("arbitrary", "arbitrary", "arbitrary"),
        name="attention",
    )(q, k, v)


def _gelu_tanh(x):
    return 0.5 * x * (1.0 + jnp.tanh(np.sqrt(2.0 / np.pi) * (x + 0.044715 * (x * x * x))))


def _lru_kernel(*refs, batch, reverse):
    if reverse:
        (xc_ref, xp_ref, xn_ref, cw_ref, cb_ref, w_ref, b_ref, lam_ref, hf_ref, xg_ref,
         o_ref, ext_ref, a_ref, bt_ref, h_ref) = refs
    else:
        (xc_ref, xp_ref, xn_ref, cw_ref, cb_ref, w_ref, b_ref, lam_ref,
         o_ref, ext_ref, a_ref, bt_ref, h_ref) = refs
    step = pl.program_id(0)
    nsteps = pl.num_programs(0)
    chunk = (nsteps - 1 - step) if reverse else step
    rows = LRU_STEPS * batch

    @pl.when(step == 0)
    def _():
        h_ref[...] = jnp.zeros_like(h_ref)

    ext_ref[0:2 * batch, :] = jnp.where(chunk > 0, xp_ref[...], 0.0)
    ext_ref[2 * batch:2 * batch + rows, :] = xc_ref[...]
    ext_ref[2 * batch + rows:3 * batch + rows, :] = jnp.where(chunk < nsteps - 1, xn_ref[...], 0.0)
    c = cb_ref[...] + sum(cw_ref[j:j + 1, :] * ext_ref[j * batch:j * batch + rows, :] for j in range(4))
    cb16 = c.astype(BF16)
    lam = lam_ref[...]
    neg_softplus = -(jnp.maximum(-lam, 0.0) + jnp.log1p(jnp.exp(-jnp.abs(lam))))
    for n in range(LRU_BLOCKS):
        sl = slice(n * LRU_BLOCK_DIM, (n + 1) * LRU_BLOCK_DIM)
        pre = _bdot(cb16[:, sl], w_ref[n]) + b_ref[n]
        r = _sigmoid(pre[:, :LRU_BLOCK_DIM])
        ig = _sigmoid(pre[:, LRU_BLOCK_DIM:])
        log_a = LRU_C * r * neg_softplus[:, sl]
        a = jnp.exp(log_a)
        a_ref[:, sl] = a
        bt_ref[:, sl] = jnp.sqrt(-jnp.tanh(log_a) * (a * a + 1.0)) * ig * c[:, sl]

    def scan_step(t, h):
        tt = (LRU_STEPS - 1 - t) if reverse else t
        rs = pl.ds(pl.multiple_of(tt * batch, batch), batch)
        h = a_ref[rs, :] * h + bt_ref[rs, :]
        if reverse:
            o_ref[rs, :] = ((h + hf_ref[rs, :]) * _gelu_tanh(xg_ref[rs, :])).astype(o_ref.dtype)
        else:
            o_ref[rs, :] = h
        return h

    h_ref[...] = lax.fori_loop(0, LRU_STEPS, scan_step, h_ref[...], unroll=4)


def _lru_scan(xr2, conv_w, conv_b, wcat, bcat, lam, *, batch, reverse, h_fwd=None, xg2=None):
    n_rows, width = xr2.shape
    rows = LRU_STEPS * batch
    n = n_rows // rows
    ch = (lambda i: n - 1 - i) if reverse else (lambda i: i)
    tile = pl.BlockSpec((rows, width), lambda i: (ch(i), 0))
    prev = pl.BlockSpec((2 * batch, width), lambda i: (jnp.maximum(ch(i) * (LRU_STEPS // 2) - 1, 0), 0))
    nxt = pl.BlockSpec((batch, width),
                       lambda i: (jnp.minimum((ch(i) + 1) * LRU_STEPS, n * LRU_STEPS - 1), 0))
    full = lambda a: pl.BlockSpec(a.shape, lambda i: (0,) * a.ndim)
    ins = [xr2, xr2, xr2, conv_w, conv_b, wcat, bcat, lam]
    specs = [tile, prev, nxt, full(conv_w), full(conv_b), full(wcat), full(bcat), full(lam)]
    if reverse:
        ins += [h_fwd, xg2]
        specs += [tile, tile]
    return pl.pallas_call(
        functools.partial(_lru_kernel, batch=batch, reverse=reverse),
        grid=(n,),
        in_specs=specs,
        out_specs=tile,
        out_shape=jax.ShapeDtypeStruct((n_rows, width), BF16 if reverse else F32),
        scratch_shapes=[
            pltpu.VMEM((rows + 3 * batch, width), F32),
            pltpu.VMEM((rows, width), F32),
            pltpu.VMEM((rows, width), F32),
            pltpu.VMEM((batch, width), F32),
        ],
        compiler_params=_params("arbitrary"),
        name="lru_bwd" if reverse else "lru_fwd",
    )(*ins)


def _merge_kernel(x_ref, at_ref, yl_ref, ga_ref, gr_ref, wa_ref, wl_ref, wo_ref, gm_ref, wr_ref, br_ref,
                  h_ref, u_ref, idx_ref, gate_ref, cnt_ref):
    y_attn = _bdot(at_ref[...], wa_ref[...])
    y_lru = _bdot(yl_ref[...], wl_ref[...])
    merged = _sigmoid(ga_ref[...]) * y_attn + _sigmoid(gr_ref[...]) * y_lru
    h = x_ref[...] + _bdot(merged.astype(BF16), wo_ref[...])
    h_ref[...] = h
    u = _rms(h, gm_ref[...])
    u_ref[...] = u
    u_hi = u.astype(BF16)
    u_lo = (u - u_hi.astype(F32)).astype(BF16)
    logits = _bdot(u_hi, wr_ref[0]) + (_bdot(u_lo, wr_ref[0]) + _bdot(u_hi, wr_ref[1]))
    lane = lax.broadcasted_iota(jnp.int32, logits.shape, 1)
    work = jnp.where(lane < N_EXPERTS, logits + br_ref[...], -jnp.inf)
    vals, hits = [], jnp.zeros(logits.shape, F32)
    for kk in range(TOP_K):
        m = jnp.max(work, axis=-1, keepdims=True)
        idx = jnp.min(jnp.where(work == m, lane, V7X_LANES), axis=-1, keepdims=True)
        pick = lane == idx
        work = jnp.where(pick, -jnp.inf, work)
        hits = hits + pick.astype(F32)
        vals.append(m)
        idx_ref[:, kk:kk + 1] = idx
    es = [jnp.exp(vv - vals[0]) for vv in vals]
    inv = 1.0 / sum(es)
    for kk in range(TOP_K):
        gate_ref[:, kk:kk + 1] = es[kk] * inv

    @pl.when((pl.program_id(0) == 0) & (pl.program_id(1) == 0))
    def _():
        cnt_ref[...] = jnp.zeros_like(cnt_ref)

    cnt_ref[0:1, :] += jnp.sum(hits, axis=0, keepdims=True)


def _merge(x, attn, ylru, ga, gr, wa, wl, wo, g_moe, wr, br):
    B, S, D = x.shape
    tm = TOKEN_TILE
    bm = lambda w: pl.BlockSpec((None, tm, w), lambda b, i: (b, i, 0))
    full = lambda a: pl.BlockSpec(a.shape, lambda b, i: (0,) * a.ndim)
    return pl.pallas_call(
        _merge_kernel,
        grid=(B, S // tm),
        in_specs=[bm(D), bm(Q_W), bm(LRU_WIDTH), bm(D), bm(D), full(wa), full(wl), full(wo),
                  full(g_moe), full(wr), full(br)],
        out_specs=[bm(D), bm(D), bm(TOP_K), bm(TOP_K), pl.BlockSpec((8, V7X_LANES), lambda b, i: (0, 0))],
        out_shape=[
            jax.ShapeDtypeStruct((B, S, D), F32),
            jax.ShapeDtypeStruct((B, S, D), F32),
            jax.ShapeDtypeStruct((B, S, TOP_K), jnp.int32),
            jax.ShapeDtypeStruct((B, S, TOP_K), F32),
            jax.ShapeDtypeStruct((8, V7X_LANES), F32),
        ],
        compiler_params=_params("arbitrary", "arbitrary"),
        name="merge_router",
    )(x, attn, ylru, ga, gr, wa, wl, wo, g_moe, wr, br)


def _rank_kernel(idx_ref, ps_ref, dest_ref, carry_ref):
    @pl.when(pl.program_id(0) == 0)
    def _():
        carry_ref[...] = jnp.zeros_like(carry_ref)

    tm = idx_ref.shape[0]
    lane = lax.broadcasted_iota(jnp.int32, (tm, V7X_LANES), 1)
    picks = [lane == idx_ref[:, kk:kk + 1] for kk in range(TOP_K)]
    hits = sum(p.astype(F32) for p in picks)
    earlier = (lax.broadcasted_iota(jnp.int32, (tm, tm), 0) >
               lax.broadcasted_iota(jnp.int32, (tm, tm), 1)).astype(BF16)
    base = _bdot(earlier, hits.astype(BF16)) + carry_ref[...] + ps_ref[...]
    for kk in range(TOP_K):
        dest_ref[:, kk:kk + 1] = jnp.sum(jnp.where(picks[kk], base, 0.0), axis=-1,
                                         keepdims=True).astype(jnp.int32)
    carry_ref[...] += jnp.sum(hits, axis=0, keepdims=True)


def _rank(idx, pad_start_row):
    T = idx.shape[0]
    tm = RANK_TILE
    return pl.pallas_call(
        _rank_kernel,
        grid=(T // tm,),
        in_specs=[pl.BlockSpec((tm, TOP_K), lambda i: (i, 0)),
                  pl.BlockSpec((1, V7X_LANES), lambda i: (0, 0))],
        out_specs=pl.BlockSpec((tm, TOP_K), lambda i: (i, 0)),
        out_shape=jax.ShapeDtypeStruct((T, TOP_K), jnp.int32),
        scratch_shapes=[pltpu.VMEM((1, V7X_LANES), F32)],
        compiler_params=_params("arbitrary"),
        name="slot_rank",
    )(idx, pad_start_row)


def _invert_kernel(pad_end_ref, count_ref, dest_ref, inv_ref, *, n_tokens, chunk):
    step = pl.program_id(0)

    @pl.when(step == 0)
    def _():
        def fill(s, carry):
            inv_ref[s] = TOP_K * n_tokens + (s & (SPARE_ROWS - 1))
            return carry

        for e in range(N_EXPERTS):
            lo = pad_end_ref[e] - ((count_ref[e] + EXPERT_TILE - 1) & -EXPERT_TILE) + count_ref[e]
            lax.fori_loop(lo, pad_end_ref[e], fill, 0)
        lax.fori_loop(pad_end_ref[N_EXPERTS - 1], inv_ref.shape[0], fill, 0)

    group = 4 * TOP_K

    def body(g, carry):
        tok0 = step * (chunk // TOP_K) + g * (group // TOP_K)
        for u in range(group):
            inv_ref[dest_ref[g * group + u]] = tok0 + ((u % TOP_K) * n_tokens + u // TOP_K)
        return carry

    lax.fori_loop(0, chunk // group, body, 0)


def _invert(pad_end, counts, dest_flat, cap, n_tokens):
    chunk = INVERT_CHUNK
    return pl.pallas_call(
        functools.partial(_invert_kernel, n_tokens=n_tokens, chunk=chunk),
        grid_spec=pltpu.PrefetchScalarGridSpec(
            num_scalar_prefetch=2,
            grid=(dest_flat.shape[0] // chunk,),
            in_specs=[pl.BlockSpec((chunk,), lambda i, pe, ct: (i,), memory_space=pltpu.SMEM)],
            out_specs=pl.BlockSpec(memory_space=pltpu.SMEM),
        ),
        out_shape=jax.ShapeDtypeStruct((cap,), jnp.int32),
        compiler_params=_params("arbitrary"),
        name="slot_invert",
    )(pad_end, counts, dest_flat)


def _expert_kernel(be_ref, nu_ref, invc_ref, invn_ref, u_hbm, wgu_ref, bgu_ref, wdn_ref, bdn_ref,
                   y_hbm, xbuf, ybuf, gsem, ssem, *, n_tokens):
    j = pl.program_id(0)
    n_used = nu_ref[0]
    parity = j & 1

    def gather(inv_ref, r, buf):
        tok = inv_ref[0, 0, r] & (n_tokens - 1)
        return pltpu.make_async_copy(u_hbm.at[pl.ds(tok, 1)], xbuf.at[buf, pl.ds(r, 1)], gsem.at[buf])

    def scatter(inv_ref, r, buf):
        return pltpu.make_async_copy(ybuf.at[buf, pl.ds(r, 1)], y_hbm.at[pl.ds(inv_ref[0, 0, r], 1)],
                                     ssem.at[buf])

    def wait_gather(buf):
        pltpu.make_async_copy(xbuf.at[buf], xbuf.at[buf], gsem.at[buf]).wait()

    def wait_scatter(buf):
        pltpu.make_async_copy(ybuf.at[buf], ybuf.at[buf], ssem.at[buf]).wait()

    @pl.when(j == 0)
    def _():
        def first(r, carry):
            gather(invc_ref, r, 0).start()
            return carry
        lax.fori_loop(0, EXPERT_TILE, first, 0)
        xbuf[1] = jnp.zeros((EXPERT_TILE, xbuf.shape[2]), xbuf.dtype)
        clears = [pltpu.make_async_copy(
            xbuf.at[1], y_hbm.at[pl.ds(TOP_K * n_tokens + c * EXPERT_TILE, EXPERT_TILE)], ssem.at[1])
            for c in range(SPARE_ROWS // EXPERT_TILE)]
        for c in clears:
            c.start()
        for c in clears:
            c.wait()

    def block(cur):
        oth = 1 - cur
        wait_gather(cur)

        @pl.when(j > 1)
        def _():
            wait_scatter(cur)

        for r in range(EXPERT_TILE):
            gather(invn_ref, r, oth).start(priority=r % 2)
        hu = _bdot(xbuf[cur].astype(BF16), wgu_ref[...].astype(BF16)) + bgu_ref[...]
        gate = jnp.minimum(hu[:, :D_EXPERT], SWIGLU_LIMIT)
        up = jnp.clip(hu[:, D_EXPERT:], -SWIGLU_LIMIT, SWIGLU_LIMIT)
        act = gate * _sigmoid(SWIGLU_ALPHA * gate)
        ybuf[cur] = _bdot(((up + 1.0) * act).astype(BF16), wdn_ref[...].astype(BF16)) + bdn_ref[...]
        for r in range(EXPERT_TILE):
            scatter(invc_ref, r, cur).start(priority=r % 2)

        @pl.when(j == n_used - 1)
        def _():
            wait_gather(oth)
            wait_scatter(cur)

            @pl.when(j > 0)
            def _():
                wait_scatter(oth)

    for cur in range(2):
        @pl.when((j < n_used) & (parity == cur))
        def _():
            block(cur)


def _experts(blk_exp, n_used, inv3, u, w_gu, b_gu, w_dn, b_dn):
    T, D = u.shape
    n_blocks = inv3.shape[0]
    exp3 = lambda j, be, nu: (be[j], 0, 0)
    inv_spec = lambda f: pl.BlockSpec((1, 1, EXPERT_TILE), lambda j, be, nu: (f(j, nu), 0, 0),
                                      memory_space=pltpu.SMEM)
    last = lambda j, nu: jnp.minimum(j, nu[0] - 1)
    return pl.pallas_call(
        functools.partial(_expert_kernel, n_tokens=T),
        grid_spec=pltpu.PrefetchScalarGridSpec(
            num_scalar_prefetch=2,
            grid=(n_blocks,),
            in_specs=[inv_spec(last),
                      inv_spec(lambda j, nu: jnp.minimum(j + 1, nu[0] - 1)),
                      pl.BlockSpec(memory_space=pl.ANY),
                      pl.BlockSpec((None, D, 2 * D_EXPERT), exp3),
                      pl.BlockSpec((None, 1, 2 * D_EXPERT), exp3),
                      pl.BlockSpec((None, D_EXPERT, D), exp3),
                      pl.BlockSpec((None, 1, D), exp3)],
            out_specs=pl.BlockSpec(memory_space=pl.ANY),
            scratch_shapes=[pltpu.VMEM((2, EXPERT_TILE, D), F32), pltpu.VMEM((2, EXPERT_TILE, D), F32),
                            pltpu.SemaphoreType.DMA((2,)), pltpu.SemaphoreType.DMA((2,))],
        ),
        out_shape=jax.ShapeDtypeStruct((TOP_K * T + SPARE_ROWS, D), F32),
        compiler_params=_params("arbitrary"),
        name="experts",
    )(blk_exp, n_used, inv3, inv3, u, w_gu, b_gu, w_dn, b_dn)


def _combine_kernel(y0_ref, y1_ref, y2_ref, y3_ref, gate_ref, h_ref, p_ref, gp_ref, wg_ref, wp_ref, o_ref):
    h = h_ref[...]
    for kk, y_ref in enumerate((y0_ref, y1_ref, y2_ref, y3_ref)):
        h = h + gate_ref[:, kk:kk + 1] * y_ref[...]
    u = _rms(h, gp_ref[...]).astype(BF16)
    o_ref[...] = h + _sigmoid(_bdot(u, wg_ref[...])) * _bdot(p_ref[...].astype(BF16), wp_ref[...])


def _combine(y4, gate, h, p, g_ple, wg, wp):
    T, D = h.shape
    tm = TOKEN_TILE
    full = lambda a: pl.BlockSpec(a.shape, lambda i: (0,) * a.ndim)
    plane = lambda kk: pl.BlockSpec((tm, D), lambda i: (kk * (T // tm) + i, 0))
    return pl.pallas_call(
        _combine_kernel,
        grid=(T // tm,),
        in_specs=[plane(0), plane(1), plane(2), plane(3),
                  pl.BlockSpec((tm, TOP_K), lambda i: (i, 0)),
                  pl.BlockSpec((tm, D), lambda i: (i, 0)),
                  pl.BlockSpec((tm, PLE_DIM), lambda i: (i, 0)),
                  full(g_ple), full(wg), full(wp)],
        out_specs=pl.BlockSpec((tm, D), lambda i: (i, 0)),
        out_shape=jax.ShapeDtypeStruct((T, D), F32),
        compiler_params=_params("arbitrary"),
        name="combine_ple",
    )(y4, y4, y4, y4, gate, h, p, g_ple, wg, wp)


def _rope_tables(seq_len):
    pos = jnp.arange(seq_len)
    axis_dim = HEAD_DIM // 2
    inv_freq = ROPE_THETA ** (-jnp.arange(0, axis_dim, 2, dtype=F32) / axis_dim)
    ang_r = (pos // GRID_W).astype(F32)[:, None] * inv_freq[None, :]
    ang_c = (pos % GRID_W).astype(F32)[:, None] * inv_freq[None, :]
    zero = jnp.zeros_like(ang_r)
    cos = jnp.concatenate([jnp.cos(ang_r)] * 2 + [jnp.cos(ang_c)] * 2, axis=-1)
    sina = jnp.concatenate([-jnp.sin(ang_r), zero, -jnp.sin(ang_c), zero], axis=-1)
    sinb = jnp.concatenate([zero, jnp.sin(ang_r), zero, jnp.sin(ang_c)], axis=-1)
    return cos, sina, sinb


def _layer(h, p, g_mix, w_in, q_norm, k_norm, conv_w, conv_b, lru_wa, lru_ba, lru_wi, lru_bi, lru_lam,
           w_attn_br, w_lru_br, w_out, g_moe, w_router, b_router, w_gu, b_gu, w_dn, b_dn,
           g_ple, w_ple_gate, w_ple_proj, rope):
    B, S, D = h.shape
    T = B * S
    row = lambda a: a.reshape(1, -1)
    q, k, v, xr, xg, ga, gr = _in_proj(
        h, row(g_mix), w_in.astype(BF16), row(q_norm) * HEAD_DIM ** -0.5, row(k_norm), *rope)
    attn = _attention(q, k, v)
    xr2 = xr.transpose(1, 0, 2).reshape(S * B, LRU_WIDTH)
    xg2 = xg.transpose(1, 0, 2).reshape(S * B, LRU_WIDTH)
    wcat = jnp.concatenate([lru_wa, lru_wi], axis=-1).astype(BF16)
    bcat = jnp.concatenate([lru_ba.reshape(2, LRU_BLOCKS, 1, LRU_BLOCK_DIM),
                            lru_bi.reshape(2, LRU_BLOCKS, 1, LRU_BLOCK_DIM)], axis=-1)
    h_fwd = _lru_scan(xr2, conv_w, row(conv_b), wcat[0], bcat[0], row(lru_lam[0]), batch=B, reverse=False)
    ylru = _lru_scan(xr2, conv_w, row(conv_b), wcat[1], bcat[1], row(lru_lam[1]), batch=B, reverse=True,
                     h_fwd=h_fwd, xg2=xg2).reshape(S, B, LRU_WIDTH).transpose(1, 0, 2)
    w_router_f = jnp.zeros((D, V7X_LANES), F32).at[:, :N_EXPERTS].set(w_router)
    w_router_hi = w_router_f.astype(BF16)
    w_router_p = jnp.stack([w_router_hi, (w_router_f - w_router_hi.astype(F32)).astype(BF16)])
    b_router_p = jnp.zeros((1, V7X_LANES), F32).at[0, :N_EXPERTS].set(b_router)
    h1, u2, idx, gate, cnt = _merge(h, attn, ylru, ga, gr, w_attn_br.astype(BF16), w_lru_br.astype(BF16),
                                    w_out.astype(BF16), row(g_moe), w_router_p, b_router_p)
    cap = T * TOP_K + N_EXPERTS * EXPERT_TILE
    counts = cnt[0, :N_EXPERTS].astype(jnp.int32)
    padded = ((counts + EXPERT_TILE - 1) // EXPERT_TILE) * EXPERT_TILE
    pad_end = jnp.cumsum(padded)
    pad_start = pad_end - padded
    blk_start = jnp.arange(cap // EXPERT_TILE, dtype=jnp.int32) * EXPERT_TILE
    blk_exp = jnp.minimum(jnp.sum(blk_start[:, None] >= pad_end[None, :], axis=1), N_EXPERTS - 1)
    n_used = (pad_end[-1:] // EXPERT_TILE).astype(jnp.int32)
    ps_row = jnp.zeros((1, V7X_LANES), F32).at[0, :N_EXPERTS].set(pad_start.astype(F32))
    dest = _rank(idx.reshape(T, TOP_K), ps_row).reshape(T * TOP_K)
    assert T & (T - 1) == 0, "token count must be a power of two for the slot encoding"
    inv3 = _invert(pad_end.astype(jnp.int32), counts, dest, cap, T).reshape(cap // EXPERT_TILE, 1, EXPERT_TILE)
    y4 = _experts(blk_exp.astype(jnp.int32), n_used, inv3, u2.reshape(T, D), w_gu,
                  b_gu.reshape(N_EXPERTS, 1, -1), w_dn, b_dn.reshape(N_EXPERTS, 1, -1))
    out = _combine(y4, gate.reshape(T, TOP_K), h1.reshape(T, D),
                   p.reshape(T, PLE_DIM), row(g_ple), w_ple_gate.astype(BF16), w_ple_proj.astype(BF16))
    return out.reshape(B, S, D)


def kernel(x, p, g_mix, w_in, q_norm, k_norm, conv_w, conv_b, lru_wa, lru_ba, lru_wi, lru_bi, lru_lam,
           w_attn_br, w_lru_br, w_out, g_moe, w_router, b_router, w_gu, b_gu, w_dn, b_dn,
           g_ple, w_ple_gate, w_ple_proj):
    rope = _rope_tables(x.shape[1])
    h = x
    for l in range(p.shape[0]):
        h = _layer(h, p[l], g_mix[l], w_in[l], q_norm[l], k_norm[l], conv_w[l], conv_b[l], lru_wa[l],
                   lru_ba[l], lru_wi[l], lru_bi[l], lru_lam[l], w_attn_br[l], w_lru_br[l], w_out[l],
                   g_moe[l], w_router[l], b_router[l], w_gu[l], b_gu[l], w_dn[l], b_dn[l],
                   g_ple[l], w_ple_gate[l], w_ple_proj[l], rope)
    return h
```

```python
import functools

import jax
import jax.numpy as jnp
import numpy as np
from jax import lax
from jax.experimental import pallas as pl
from jax.experimental.pallas import tpu as pltpu

D_MODEL = 1024
N_Q_HEADS = 8
N_KV_HEADS = 2
HEAD_DIM = 128
Q_GROUP = N_Q_HEADS // N_KV_HEADS
ROPE_THETA = 10000.0
GRID_W = 64
LRU_WIDTH = 1024
LRU_BLOCKS = 8
LRU_BLOCK_DIM = LRU_WIDTH // LRU_BLOCKS
LRU_C = 8.0
N_EXPERTS = 32
TOP_K = 4
D_EXPERT = 1024
SWIGLU_LIMIT = 7.0
SWIGLU_ALPHA = 1.702
PLE_DIM = 256
EPS = 1e-6
Q_W = N_Q_HEADS * HEAD_DIM
KV_W = N_KV_HEADS * HEAD_DIM
GROUP_W = Q_GROUP * HEAD_DIM
IN_WIDTH = Q_W + 2 * KV_W + 2 * LRU_WIDTH + 2 * D_MODEL
OFF_Q = 0
OFF_K = OFF_Q + Q_W
OFF_V = OFF_K + KV_W
OFF_XR = OFF_V + KV_W
OFF_XG = OFF_XR + LRU_WIDTH
OFF_GA = OFF_XG + LRU_WIDTH
OFF_GR = OFF_GA + D_MODEL

V7X_LANES = 128
V7X_VMEM_LIMIT_BYTES = 56 * 1024 * 1024
TOKEN_TILE = 512
ATTN_Q_TILE = 512
LRU_STEPS = 32
EXPERT_TILE = 512
RANK_TILE = 512
INVERT_CHUNK = 8192
SPARE_ROWS = 1024

F32 = jnp.float32
BF16 = jnp.bfloat16


def _params(*sem):
    return pltpu.CompilerParams(dimension_semantics=sem, vmem_limit_bytes=V7X_VMEM_LIMIT_BYTES)


def _bdot(a, b):
    return jnp.dot(a, b, preferred_element_type=F32)


def _rms(x, g):
    return x * lax.rsqrt(jnp.mean(x * x, axis=-1, keepdims=True) + EPS) * g


def _sigmoid(x):
    return 0.5 * jnp.tanh(0.5 * x) + 0.5


def _in_proj_kernel(x_ref, g_ref, w_ref, qn_ref, kn_ref, cos_ref, sina_ref, sinb_ref,
                    q_ref, k_ref, v_ref, xr_ref, xg_ref, ga_ref, gr_ref):
    u = _rms(x_ref[...], g_ref[...]).astype(BF16)
    cos, sina, sinb = cos_ref[...], sina_ref[...], sinb_ref[...]

    def head(z, gain):
        y = _rms(z, gain)
        return y * cos + pltpu.roll(y, 96, 1) * sina + pltpu.roll(y, 32, 1) * sinb

    zq = _bdot(u, w_ref[:, OFF_Q:OFF_Q + Q_W])
    for h in range(N_Q_HEADS):
        sl = slice(h * HEAD_DIM, (h + 1) * HEAD_DIM)
        q_ref[:, sl] = head(zq[:, sl], qn_ref[...]).astype(BF16)
    zk = _bdot(u, w_ref[:, OFF_K:OFF_K + KV_W])
    for h in range(N_KV_HEADS):
        sl = slice(h * HEAD_DIM, (h + 1) * HEAD_DIM)
        k_ref[:, sl] = head(zk[:, sl], kn_ref[...]).astype(BF16)
    v_ref[...] = _bdot(u, w_ref[:, OFF_V:OFF_V + KV_W]).astype(BF16)
    xr_ref[...] = _bdot(u, w_ref[:, OFF_XR:OFF_XR + LRU_WIDTH])
    xg_ref[...] = _bdot(u, w_ref[:, OFF_XG:OFF_XG + LRU_WIDTH])
    ga_ref[...] = _bdot(u, w_ref[:, OFF_GA:OFF_GA + D_MODEL])
    gr_ref[...] = _bdot(u, w_ref[:, OFF_GR:OFF_GR + D_MODEL])


def _in_proj(x, g_mix, w_in, qn, kn, cos, sina, sinb):
    B, S, D = x.shape
    tm = TOKEN_TILE
    bm = lambda w: pl.BlockSpec((None, tm, w), lambda b, i: (b, i, 0))
    full = lambda a: pl.BlockSpec(a.shape, lambda b, i: (0,) * a.ndim)
    tab = pl.BlockSpec((tm, HEAD_DIM), lambda b, i: (i, 0))
    return pl.pallas_call(
        _in_proj_kernel,
        grid=(B, S // tm),
        in_specs=[bm(D), full(g_mix), full(w_in), full(qn), full(kn), tab, tab, tab],
        out_specs=[bm(Q_W), bm(KV_W), bm(KV_W), bm(LRU_WIDTH), bm(LRU_WIDTH), bm(D), bm(D)],
        out_shape=[
            jax.ShapeDtypeStruct((B, S, Q_W), BF16),
            jax.ShapeDtypeStruct((B, S, KV_W), BF16),
            jax.ShapeDtypeStruct((B, S, KV_W), BF16),
            jax.ShapeDtypeStruct((B, S, LRU_WIDTH), F32),
            jax.ShapeDtypeStruct((B, S, LRU_WIDTH), F32),
            jax.ShapeDtypeStruct((B, S, D), F32),
            jax.ShapeDtypeStruct((B, S, D), F32),
        ],
        compiler_params=_params("arbitrary", "arbitrary"),
        name="in_proj",
    )(x, g_mix, w_in, qn, kn, cos, sina, sinb)


def _attn_kernel(q_ref, k_ref, v_ref, o_ref):
    k = k_ref[...]
    v = v_ref[...]
    for h in range(Q_GROUP):
        sl = slice(h * HEAD_DIM, (h + 1) * HEAD_DIM)
        s = lax.dot_general(q_ref[:, sl], k, (((1,), (1,)), ((), ())), preferred_element_type=F32)
        p = jnp.exp(s - jnp.max(s, axis=-1, keepdims=True))
        l = jnp.sum(p, axis=-1, keepdims=True)
        o = _bdot(p.astype(BF16), v)
        o_ref[:, sl] = (o * (1.0 / l)).astype(BF16)


def _attention(q, k, v):
    B, S, _ = q.shape
    tq = ATTN_Q_TILE
    return pl.pallas_call(
        _attn_kernel,
        grid=(B, N_KV_HEADS, S // tq),
        in_specs=[
            pl.BlockSpec((None, tq, GROUP_W), lambda b, g, i: (b, i, g)),
            pl.BlockSpec((None, S, HEAD_DIM), lambda b, g, i: (b, 0, g)),
            pl.BlockSpec((None, S, HEAD_DIM), lambda b, g, i: (b, 0, g)),
        ],
        out_specs=pl.BlockSpec((None, tq, GROUP_W), lambda b, g, i: (b, i, g)),
        out_shape=jax.ShapeDtypeStruct((B, S, Q_W), BF16),
        compiler_params=_params("arbitrary", "arbitrary", "arbitrary"),
        name="attention",
    )(q, k, v)


def _gelu_tanh(x):
    return 0.5 * x * (1.0 + jnp.tanh(np.sqrt(2.0 / np.pi) * (x + 0.044715 * (x * x * x))))


def _lru_kernel(*refs, batch, reverse):
    if reverse:
        (xc_ref, xp_ref, xn_ref, cw_ref, cb_ref, w_ref, b_ref, lam_ref, hf_ref, xg_ref,
         o_ref, ext_ref, a_ref, bt_ref, h_ref) = refs
    else:
        (xc_ref, xp_ref, xn_ref, cw_ref, cb_ref, w_ref, b_ref, lam_ref,
         o_ref, ext_ref, a_ref, bt_ref, h_ref) = refs
    step = pl.program_id(0)
    nsteps = pl.num_programs(0)
    chunk = (nsteps - 1 - step) if reverse else step
    rows = LRU_STEPS * batch

    @pl.when(step == 0)
    def _():
        h_ref[...] = jnp.zeros_like(h_ref)

    ext_ref[0:2 * batch, :] = jnp.where(chunk > 0, xp_ref[...], 0.0)
    ext_ref[2 * batch:2 * batch + rows, :] = xc_ref[...]
    ext_ref[2 * batch + rows:3 * batch + rows, :] = jnp.where(chunk < nsteps - 1, xn_ref[...], 0.0)
    c = cb_ref[...] + sum(cw_ref[j:j + 1, :] * ext_ref[j * batch:j * batch + rows, :] for j in range(4))
    cb16 = c.astype(BF16)
    lam = lam_ref[...]
    neg_softplus = -(jnp.maximum(-lam, 0.0) + jnp.log1p(jnp.exp(-jnp.abs(lam))))
    for n in range(LRU_BLOCKS):
        sl = slice(n * LRU_BLOCK_DIM, (n + 1) * LRU_BLOCK_DIM)
        pre = _bdot(cb16[:, sl], w_ref[n]) + b_ref[n]
        r = _sigmoid(pre[:, :LRU_BLOCK_DIM])
        ig = _sigmoid(pre[:, LRU_BLOCK_DIM:])
        log_a = LRU_C * r * neg_softplus[:, sl]
        a = jnp.exp(log_a)
        a_ref[:, sl] = a
        one_minus_a2 = -jnp.tanh(log_a) * (a * a + 1.0)
        root = jnp.where(one_minus_a2 > 0.0, one_minus_a2 * lax.rsqrt(one_minus_a2), 0.0)
        bt_ref[:, sl] = root * ig * c[:, sl]

    def scan_step(t, h):
        tt = (LRU_STEPS - 1 - t) if reverse else t
        rs = pl.ds(pl.multiple_of(tt * batch, batch), batch)
        h = a_ref[rs, :] * h + bt_ref[rs, :]
        if reverse:
            o_ref[rs, :] = ((h + hf_ref[rs, :]) * _gelu_tanh(xg_ref[rs, :])).astype(o_ref.dtype)
        else:
            o_ref[rs, :] = h
        return h

    h_ref[...] = lax.fori_loop(0, LRU_STEPS, scan_step, h_ref[...], unroll=4)


def _lru_scan(xr2, conv_w, conv_b, wcat, bcat, lam, *, batch, reverse, h_fwd=None, xg2=None):
    n_rows, width = xr2.shape
    rows = LRU_STEPS * batch
    n = n_rows // rows
    ch = (lambda i: n - 1 - i) if reverse else (lambda i: i)
    tile = pl.BlockSpec((rows, width), lambda i: (ch(i), 0))
    prev = pl.BlockSpec((2 * batch, width), lambda i: (jnp.maximum(ch(i) * (LRU_STEPS // 2) - 1, 0), 0))
    nxt = pl.BlockSpec((batch, width),
                       lambda i: (jnp.minimum((ch(i) + 1) * LRU_STEPS, n * LRU_STEPS - 1), 0))
    full = lambda a: pl.BlockSpec(a.shape, lambda i: (0,) * a.ndim)
    ins = [xr2, xr2, xr2, conv_w, conv_b, wcat, bcat, lam]
    specs = [tile, prev, nxt, full(conv_w), full(conv_b), full(wcat), full(bcat), full(lam)]
    if reverse:
        ins += [h_fwd, xg2]
        specs += [tile, tile]
    return pl.pallas_call(
        functools.partial(_lru_kernel, batch=batch, reverse=reverse),
        grid=(n,),
        in_specs=specs,
        out_specs=tile,
        out_shape=jax.ShapeDtypeStruct((n_rows, width), BF16 if reverse else F32),
        scratch_shapes=[
            pltpu.VMEM((rows + 3 * batch, width), F32),
            pltpu.VMEM((rows, width), F32),
            pltpu.VMEM((rows, width), F32),
            pltpu.VMEM((batch, width), F32),
        ],
        compiler_params=_params("arbitrary"),
        name="lru_bwd" if reverse else "lru_fwd",
    )(*ins)


def _merge_kernel(x_ref, at_ref, yl_ref, ga_ref, gr_ref, wa_ref, wl_ref, wo_ref, gm_ref, wr_ref, br_ref,
                  h_ref, u_ref, idx_ref, gate_ref, cnt_ref):
    y_attn = _bdot(at_ref[...], wa_ref[...])
    y_lru = _bdot(yl_ref[...], wl_ref[...])
    merged = _sigmoid(ga_ref[...]) * y_attn + _sigmoid(gr_ref[...]) * y_lru
    h = x_ref[...] + _bdot(merged.astype(BF16), wo_ref[...])
    h_ref[...] = h
    u = _rms(h, gm_ref[...])
    u_ref[...] = u
    u_hi = u.astype(BF16)
    u_lo = (u - u_hi.astype(F32)).astype(BF16)
    logits = _bdot(u_hi, wr_ref[0]) + (_bdot(u_lo, wr_ref[0]) + _bdot(u_hi, wr_ref[1]))
    lane = lax.broadcasted_iota(jnp.int32, logits.shape, 1)
    work = jnp.where(lane < N_EXPERTS, logits + br_ref[...], -jnp.inf)
    vals, hits = [], jnp.zeros(logits.shape, F32)
    for kk in range(TOP_K):
        m = jnp.max(work, axis=-1, keepdims=True)
        idx = jnp.min(jnp.where(work == m, lane, V7X_LANES), axis=-1, keepdims=True)
        pick = lane == idx
        work = jnp.where(pick, -jnp.inf, work)
        hits = hits + pick.astype(F32)
        vals.append(m)
        idx_ref[:, kk:kk + 1] = idx
    es = [jnp.exp(vv - vals[0]) for vv in vals]
    inv = 1.0 / sum(es)
    for kk in range(TOP_K):
        gate_ref[:, kk:kk + 1] = es[kk] * inv

    @pl.when((pl.program_id(0) == 0) & (pl.program_id(1) == 0))
    def _():
        cnt_ref[...] = jnp.zeros_like(cnt_ref)

    cnt_ref[0:1, :] += jnp.sum(hits, axis=0, keepdims=True)


def _merge(x, attn, ylru, ga, gr, wa, wl, wo, g_moe, wr, br):
    B, S, D = x.shape
    tm = TOKEN_TILE
    bm = lambda w: pl.BlockSpec((None, tm, w), lambda b, i: (b, i, 0))
    full = lambda a: pl.BlockSpec(a.shape, lambda b, i: (0,) * a.ndim)
    return pl.pallas_call(
        _merge_kernel,
        grid=(B, S // tm),
        in_specs=[bm(D), bm(Q_W), bm(LRU_WIDTH), bm(D), bm(D), full(wa), full(wl), full(wo),
                  full(g_moe), full(wr), full(br)],
        out_specs=[bm(D), bm(D), bm(TOP_K), bm(TOP_K), pl.BlockSpec((8, V7X_LANES), lambda b, i: (0, 0))],
        out_shape=[
            jax.ShapeDtypeStruct((B, S, D), F32),
            jax.ShapeDtypeStruct((B, S, D), F32),
            jax.ShapeDtypeStruct((B, S, TOP_K), jnp.int32),
            jax.ShapeDtypeStruct((B, S, TOP_K), F32),
            jax.ShapeDtypeStruct((8, V7X_LANES), F32),
        ],
        compiler_params=_params("arbitrary", "arbitrary"),
        name="merge_router",
    )(x, attn, ylru, ga, gr, wa, wl, wo, g_moe, wr, br)


def _rank_kernel(idx_ref, ps_ref, dest_ref, carry_ref):
    @pl.when(pl.program_id(0) == 0)
    def _():
        carry_ref[...] = jnp.zeros_like(carry_ref)

    tm = idx_ref.shape[0]
    lane = lax.broadcasted_iota(jnp.int32, (tm, V7X_LANES), 1)
    picks = [lane == idx_ref[:, kk:kk + 1] for kk in range(TOP_K)]
    hits = sum(p.astype(F32) for p in picks)
    earlier = (lax.broadcasted_iota(jnp.int32, (tm, tm), 0) >
               lax.broadcasted_iota(jnp.int32, (tm, tm), 1)).astype(BF16)
    base = _bdot(earlier, hits.astype(BF16)) + carry_ref[...] + ps_ref[...]
    for kk in range(TOP_K):
        dest_ref[:, kk:kk + 1] = jnp.sum(jnp.where(picks[kk], base, 0.0), axis=-1,
                                         keepdims=True).astype(jnp.int32)
    carry_ref[...] += jnp.sum(hits, axis=0, keepdims=True)


def _rank(idx, pad_start_row):
    T = idx.shape[0]
    tm = RANK_TILE
    return pl.pallas_call(
        _rank_kernel,
        grid=(T // tm,),
        in_specs=[pl.BlockSpec((tm, TOP_K), lambda i: (i, 0)),
                  pl.BlockSpec((1, V7X_LANES), lambda i: (0, 0))],
        out_specs=pl.BlockSpec((tm, TOP_K), lambda i: (i, 0)),
        out_shape=jax.ShapeDtypeStruct((T, TOP_K), jnp.int32),
        scratch_shapes=[pltpu.VMEM((1, V7X_LANES), F32)],
        compiler_params=_params("arbitrary"),
        name="slot_rank",
    )(idx, pad_start_row)


def _invert_kernel(pad_end_ref, count_ref, dest_ref, inv_ref, *, n_tokens, chunk):
    step = pl.program_id(0)

    @pl.when(step == 0)
    def _():
        def fill(s, carry):
            inv_ref[s] = TOP_K * n_tokens + (s & (SPARE_ROWS - 1))
            return carry

        for e in range(N_EXPERTS):
            lo = pad_end_ref[e] - ((count_ref[e] + EXPERT_TILE - 1) & -EXPERT_TILE) + count_ref[e]
            lax.fori_loop(lo, pad_end_ref[e], fill, 0)
        lax.fori_loop(pad_end_ref[N_EXPERTS - 1], inv_ref.shape[0], fill, 0)

    group = 4 * TOP_K

    def body(g, carry):
        tok0 = step * (chunk // TOP_K) + g * (group // TOP_K)
        for u in range(group):
            inv_ref[dest_ref[g * group + u]] = tok0 + ((u % TOP_K) * n_tokens + u // TOP_K)
        return carry

    lax.fori_loop(0, chunk // group, body, 0)


def _invert(pad_end, counts, dest_flat, cap, n_tokens):
    chunk = INVERT_CHUNK
    return pl.pallas_call(
        functools.partial(_invert_kernel, n_tokens=n_tokens, chunk=chunk),
        grid_spec=pltpu.PrefetchScalarGridSpec(
            num_scalar_prefetch=2,
            grid=(dest_flat.shape[0] // chunk,),
            in_specs=[pl.BlockSpec((chunk,), lambda i, pe, ct: (i,), memory_space=pltpu.SMEM)],
            out_specs=pl.BlockSpec(memory_space=pltpu.SMEM),
        ),
        out_shape=jax.ShapeDtypeStruct((cap,), jnp.int32),
        compiler_params=_params("arbitrary"),
        name="slot_invert",
    )(pad_end, counts, dest_flat)


def _expert_kernel(be_ref, nu_ref, invc_ref, invn_ref, u_hbm, wgu_ref, bgu_ref, wdn_ref, bdn_ref,
                   y_hbm, xbuf, ybuf, gsem, ssem, *, n_tokens):
    j = pl.program_id(0)
    n_used = nu_ref[0]
    parity = j & 1

    def gather(inv_ref, r, buf):
        tok = inv_ref[0, 0, r] & (n_tokens - 1)
        return pltpu.make_async_copy(u_hbm.at[pl.ds(tok, 1)], xbuf.at[buf, pl.ds(r, 1)], gsem.at[buf])

    def scatter(inv_ref, r, buf):
        return pltpu.make_async_copy(ybuf.at[buf, pl.ds(r, 1)], y_hbm.at[pl.ds(inv_ref[0, 0, r], 1)],
                                     ssem.at[buf])

    def wait_gather(buf):
        pltpu.make_async_copy(xbuf.at[buf], xbuf.at[buf], gsem.at[buf]).wait()

    def wait_scatter(buf):
        pltpu.make_async_copy(ybuf.at[buf], ybuf.at[buf], ssem.at[buf]).wait()

    @pl.when(j == 0)
    def _():
        def first(r, carry):
            gather(invc_ref, r, 0).start()
            return carry
        lax.fori_loop(0, EXPERT_TILE, first, 0)
        xbuf[1] = jnp.zeros((EXPERT_TILE, xbuf.shape[2]), xbuf.dtype)
        clears = [pltpu.make_async_copy(
            xbuf.at[1], y_hbm.at[pl.ds(TOP_K * n_tokens + c * EXPERT_TILE, EXPERT_TILE)], ssem.at[1])
            for c in range(SPARE_ROWS // EXPERT_TILE)]
        for c in clears:
            c.start()
        for c in clears:
            c.wait()

    def block(cur):
        oth = 1 - cur
        wait_gather(cur)

        @pl.when(j > 1)
        def _():
            wait_scatter(cur)

        for r in range(EXPERT_TILE):
            gather(invn_ref, r, oth).start(priority=r % 2)
        hu = _bdot(xbuf[cur].astype(BF16), wgu_ref[...].astype(BF16)) + bgu_ref[...]
        gate = jnp.minimum(hu[:, :D_EXPERT], SWIGLU_LIMIT)
        up = jnp.clip(hu[:, D_EXPERT:], -SWIGLU_LIMIT, SWIGLU_LIMIT)
        act = gate * _sigmoid(SWIGLU_ALPHA * gate)
        ybuf[cur] = _bdot(((up + 1.0) * act).astype(BF16), wdn_ref[...].astype(BF16)) + bdn_ref[...]
        for r in range(EXPERT_TILE):
            scatter(invc_ref, r, cur).start(priority=r % 2)

        @pl.when(j == n_used - 1)
        def _():
            wait_gather(oth)
            wait_scatter(cur)

            @pl.when(j > 0)
            def _():
                wait_scatter(oth)

    for cur in range(2):
        @pl.when((j < n_used) & (parity == cur))
        def _():
            block(cur)


def _experts(blk_exp, n_used, inv3, u, w_gu, b_gu, w_dn, b_dn):
    T, D = u.shape
    n_blocks = inv3.shape[0]
    exp3 = lambda j, be, nu: (be[j], 0, 0)
    inv_spec = lambda f: pl.BlockSpec((1, 1, EXPERT_TILE), lambda j, be, nu: (f(j, nu), 0, 0),
                                      memory_space=pltpu.SMEM)
    last = lambda j, nu: jnp.minimum(j, nu[0] - 1)
    return pl.pallas_call(
        functools.partial(_expert_kernel, n_tokens=T),
        grid_spec=pltpu.PrefetchScalarGridSpec(
            num_scalar_prefetch=2,
            grid=(n_blocks,),
            in_specs=[inv_spec(last),
                      inv_spec(lambda j, nu: jnp.minimum(j + 1, nu[0] - 1)),
                      pl.BlockSpec(memory_space=pl.ANY),
                      pl.BlockSpec((None, D, 2 * D_EXPERT), exp3),
                      pl.BlockSpec((None, 1, 2 * D_EXPERT), exp3),
                      pl.BlockSpec((None, D_EXPERT, D), exp3),
                      pl.BlockSpec((None, 1, D), exp3)],
            out_specs=pl.BlockSpec(memory_space=pl.ANY),
            scratch_shapes=[pltpu.VMEM((2, EXPERT_TILE, D), F32), pltpu.VMEM((2, EXPERT_TILE, D), F32),
                            pltpu.SemaphoreType.DMA((2,)), pltpu.SemaphoreType.DMA((2,))],
        ),
        out_shape=jax.ShapeDtypeStruct((TOP_K * T + SPARE_ROWS, D), F32),
        compiler_params=_params("arbitrary"),
        name="experts",
    )(blk_exp, n_used, inv3, inv3, u, w_gu, b_gu, w_dn, b_dn)


def _combine_kernel(y0_ref, y1_ref, y2_ref, y3_ref, gate_ref, h_ref, p_ref, gp_ref, wg_ref, wp_ref, o_ref):
    h = h_ref[...]
    for kk, y_ref in enumerate((y0_ref, y1_ref, y2_ref, y3_ref)):
        h = h + gate_ref[:, kk:kk + 1] * y_ref[...]
    u = _rms(h, gp_ref[...]).astype(BF16)
    o_ref[...] = h + _sigmoid(_bdot(u, wg_ref[...])) * _bdot(p_ref[...].astype(BF16), wp_ref[...])


def _combine(y4, gate, h, p, g_ple, wg, wp):
    T, D = h.shape
    tm = TOKEN_TILE
    full = lambda a: pl.BlockSpec(a.shape, lambda i: (0,) * a.ndim)
    plane = lambda kk: pl.BlockSpec((tm, D), lambda i: (kk * (T // tm) + i, 0))
    return pl.pallas_call(
        _combine_kernel,
        grid=(T // tm,),
        in_specs=[plane(0), plane(1), plane(2), plane(3),
                  pl.BlockSpec((tm, TOP_K), lambda i: (i, 0)),
                  pl.BlockSpec((tm, D), lambda i: (i, 0)),
                  pl.BlockSpec((tm, PLE_DIM), lambda i: (i, 0)),
                  full(g_ple), full(wg), full(wp)],
        out_specs=pl.BlockSpec((tm, D), lambda i: (i, 0)),
        out_shape=jax.ShapeDtypeStruct((T, D), F32),
        compiler_params=_params("arbitrary"),
        name="combine_ple",
    )(y4, y4, y4, y4, gate, h, p, g_ple, wg, wp)


def _rope_tables(seq_len):
    pos = jnp.arange(seq_len)
    axis_dim = HEAD_DIM // 2
    inv_freq = ROPE_THETA ** (-jnp.arange(0, axis_dim, 2, dtype=F32) / axis_dim)
    ang_r = (pos // GRID_W).astype(F32)[:, None] * inv_freq[None, :]
    ang_c = (pos % GRID_W).astype(F32)[:, None] * inv_freq[None, :]
    zero = jnp.zeros_like(ang_r)
    cos = jnp.concatenate([jnp.cos(ang_r)] * 2 + [jnp.cos(ang_c)] * 2, axis=-1)
    sina = jnp.concatenate([-jnp.sin(ang_r), zero, -jnp.sin(ang_c), zero], axis=-1)
    sinb = jnp.concatenate([zero, jnp.sin(ang_r), zero, jnp.sin(ang_c)], axis=-1)
    return cos, sina, sinb


def _layer(h, p, g_mix, w_in, q_norm, k_norm, conv_w, conv_b, lru_wa, lru_ba, lru_wi, lru_bi, lru_lam,
           w_attn_br, w_lru_br, w_out, g_moe, w_router, b_router, w_gu, b_gu, w_dn, b_dn,
           g_ple, w_ple_gate, w_ple_proj, rope):
    B, S, D = h.shape
    T = B * S
    row = lambda a: a.reshape(1, -1)
    q, k, v, xr, xg, ga, gr = _in_proj(
        h, row(g_mix), w_in.astype(BF16), row(q_norm) * HEAD_DIM ** -0.5, row(k_norm), *rope)
    attn = _attention(q, k, v)
    xr2 = xr.transpose(1, 0, 2).reshape(S * B, LRU_WIDTH)
    xg2 = xg.transpose(1, 0, 2).reshape(S * B, LRU_WIDTH)
    wcat = jnp.concatenate([lru_wa, lru_wi], axis=-1).astype(BF16)
    bcat = jnp.concatenate([lru_ba.reshape(2, LRU_BLOCKS, 1, LRU_BLOCK_DIM),
                            lru_bi.reshape(2, LRU_BLOCKS, 1, LRU_BLOCK_DIM)], axis=-1)
    h_fwd = _lru_scan(xr2, conv_w, row(conv_b), wcat[0], bcat[0], row(lru_lam[0]), batch=B, reverse=False)
    ylru = _lru_scan(xr2, conv_w, row(conv_b), wcat[1], bcat[1], row(lru_lam[1]), batch=B, reverse=True,
                     h_fwd=h_fwd, xg2=xg2).reshape(S, B, LRU_WIDTH).transpose(1, 0, 2)
    w_router_f = jnp.zeros((D, V7X_LANES), F32).at[:, :N_EXPERTS].set(w_router)
    w_router_hi = w_router_f.astype(BF16)
    w_router_p = jnp.stack([w_router_hi, (w_router_f - w_router_hi.astype(F32)).astype(BF16)])
    b_router_p = jnp.zeros((1, V7X_LANES), F32).at[0, :N_EXPERTS].set(b_router)
    h1, u2, idx, gate, cnt = _merge(h, attn, ylru, ga, gr, w_attn_br.astype(BF16), w_lru_br.astype(BF16),
                                    w_out.astype(BF16), row(g_moe), w_router_p, b_router_p)
    cap = T * TOP_K + N_EXPERTS * EXPERT_TILE
    counts = cnt[0, :N_EXPERTS].astype(jnp.int32)
    padded = ((counts + EXPERT_TILE - 1) // EXPERT_TILE) * EXPERT_TILE
    pad_end = jnp.cumsum(padded)
    pad_start = pad_end - padded
    blk_start = jnp.arange(cap // EXPERT_TILE, dtype=jnp.int32) * EXPERT_TILE
    blk_exp = jnp.minimum(jnp.sum(blk_start[:, None] >= pad_end[None, :], axis=1), N_EXPERTS - 1)
    n_used = (pad_end[-1:] // EXPERT_TILE).astype(jnp.int32)
    ps_row = jnp.zeros((1, V7X_LANES), F32).at[0, :N_EXPERTS].set(pad_start.astype(F32))
    dest = _rank(idx.reshape(T, TOP_K), ps_row).reshape(T * TOP_K)
    assert T & (T - 1) == 0, "token count must be a power of two for the slot encoding"
    inv3 = _invert(pad_end.astype(jnp.int32), counts, dest, cap, T).reshape(cap // EXPERT_TILE, 1, EXPERT_TILE)
    y4 = _experts(blk_exp.astype(jnp.int32), n_used, inv3, u2.reshape(T, D), w_gu,
                  b_gu.reshape(N_EXPERTS, 1, -1), w_dn, b_dn.reshape(N_EXPERTS, 1, -1))
    out = _combine(y4, gate.reshape(T, TOP_K), h1.reshape(T, D),
                   p.reshape(T, PLE_DIM), row(g_ple), w_ple_gate.astype(BF16), w_ple_proj.astype(BF16))
    return out.reshape(B, S, D)


def kernel(x, p, g_mix, w_in, q_norm, k_norm, conv_w, conv_b, lru_wa, lru_ba, lru_wi, lru_bi, lru_lam,
           w_attn_br, w_lru_br, w_out, g_moe, w_router, b_router, w_gu, b_gu, w_dn, b_dn,
           g_ple, w_ple_gate, w_ple_proj):
    rope = _rope_tables(x.shape[1])
    h = x
    for l in range(p.shape[0]):
        h = _layer(h, p[l], g_mix[l], w_in[l], q_norm[l], k_norm[l], conv_w[l], conv_b[l], lru_wa[l],
                   lru_ba[l], lru_wi[l], lru_bi[l], lru_lam[l], w_attn_br[l], w_lru_br[l], w_out[l],
                   g_moe[l], w_router[l], b_router[l], w_gu[l], b_gu[l], w_dn[l], b_dn[l],
                   g_ple[l], w_ple_gate[l], w_ple_proj[l], rope)
    return h
```

```python
import functools

import jax
import jax.numpy as jnp
import numpy as np
from jax import lax
from jax.experimental import pallas as pl
from jax.experimental.pallas import tpu as pltpu

D_MODEL = 1024
N_Q_HEADS = 8
N_KV_HEADS = 2
HEAD_DIM = 128
Q_GROUP = N_Q_HEADS // N_KV_HEADS
ROPE_THETA = 10000.0
GRID_W = 64
LRU_WIDTH = 1024
LRU_BLOCKS = 8
LRU_BLOCK_DIM = LRU_WIDTH // LRU_BLOCKS
LRU_C = 8.0
N_EXPERTS = 32
TOP_K = 4
D_EXPERT = 1024
SWIGLU_LIMIT = 7.0
SWIGLU_ALPHA = 1.702
PLE_DIM = 256
EPS = 1e-6
Q_W = N_Q_HEADS * HEAD_DIM
KV_W = N_KV_HEADS * HEAD_DIM
GROUP_W = Q_GROUP * HEAD_DIM
IN_WIDTH = Q_W + 2 * KV_W + 2 * LRU_WIDTH + 2 * D_MODEL
OFF_Q = 0
OFF_K = OFF_Q + Q_W
OFF_V = OFF_K + KV_W
OFF_XR = OFF_V + KV_W
OFF_XG = OFF_XR + LRU_WIDTH
OFF_GA = OFF_XG + LRU_WIDTH
OFF_GR = OFF_GA + D_MODEL

V7X_LANES = 128
V7X_VMEM_LIMIT_BYTES = 56 * 1024 * 1024
TOKEN_TILE = 512
ATTN_Q_TILE = 512
LRU_STEPS = 32
EXPERT_TILE = 512
RANK_TILE = 512
INVERT_CHUNK = 8192
SPARE_ROWS = 1024

F32 = jnp.float32
BF16 = jnp.bfloat16


def _params(*sem):
    return pltpu.CompilerParams(dimension_semantics=sem, vmem_limit_bytes=V7X_VMEM_LIMIT_BYTES)


def _bdot(a, b):
    return jnp.dot(a, b, preferred_element_type=F32)


def _rms(x, g):
    return x * lax.rsqrt(jnp.mean(x * x, axis=-1, keepdims=True) + EPS) * g


ROW_TILE = 8


def _store_row_tiles(ref, x):
    n = x.shape[0]
    for j in range(ROW_TILE):
        ref[pl.ds(j, n, stride=ROW_TILE), :] = x[:, j * V7X_LANES:(j + 1) * V7X_LANES]


def _load_row_tiles(ref):
    n = ref.shape[0] // ROW_TILE
    return jnp.concatenate([ref[pl.ds(j, n, stride=ROW_TILE), :] for j in range(ROW_TILE)], axis=-1)


def _sigmoid(x):
    return 0.5 * jnp.tanh(0.5 * x) + 0.5


def _in_proj_kernel(x_ref, g_ref, w_ref, qn_ref, kn_ref, cos_ref, sina_ref, sinb_ref,
                    q_ref, k_ref, v_ref, xr_ref, xg_ref, ga_ref, gr_ref):
    u = _rms(x_ref[...], g_ref[...]).astype(BF16)
    cos, sina, sinb = cos_ref[...], sina_ref[...], sinb_ref[...]

    def head(z, gain):
        y = _rms(z, gain)
        return y * cos + pltpu.roll(y, 96, 1) * sina + pltpu.roll(y, 32, 1) * sinb

    zq = _bdot(u, w_ref[:, OFF_Q:OFF_Q + Q_W])
    for h in range(N_Q_HEADS):
        sl = slice(h * HEAD_DIM, (h + 1) * HEAD_DIM)
        q_ref[:, sl] = head(zq[:, sl], qn_ref[...]).astype(BF16)
    zk = _bdot(u, w_ref[:, OFF_K:OFF_K + KV_W])
    for h in range(N_KV_HEADS):
        sl = slice(h * HEAD_DIM, (h + 1) * HEAD_DIM)
        k_ref[:, sl] = head(zk[:, sl], kn_ref[...]).astype(BF16)
    v_ref[...] = _bdot(u, w_ref[:, OFF_V:OFF_V + KV_W]).astype(BF16)
    xr_ref[...] = _bdot(u, w_ref[:, OFF_XR:OFF_XR + LRU_WIDTH])
    xg_ref[...] = _bdot(u, w_ref[:, OFF_XG:OFF_XG + LRU_WIDTH])
    ga_ref[...] = _bdot(u, w_ref[:, OFF_GA:OFF_GA + D_MODEL])
    gr_ref[...] = _bdot(u, w_ref[:, OFF_GR:OFF_GR + D_MODEL])


def _in_proj(x, g_mix, w_in, qn, kn, cos, sina, sinb):
    B, S, D = x.shape
    tm = TOKEN_TILE
    bm = lambda w: pl.BlockSpec((None, tm, w), lambda b, i: (b, i, 0))
    full = lambda a: pl.BlockSpec(a.shape, lambda b, i: (0,) * a.ndim)
    tab = pl.BlockSpec((tm, HEAD_DIM), lambda b, i: (i, 0))
    return pl.pallas_call(
        _in_proj_kernel,
        grid=(B, S // tm),
        in_specs=[bm(D), full(g_mix), full(w_in), full(qn), full(kn), tab, tab, tab],
        out_specs=[bm(Q_W), bm(KV_W), bm(KV_W), bm(LRU_WIDTH), bm(LRU_WIDTH), bm(D), bm(D)],
        out_shape=[
            jax.ShapeDtypeStruct((B, S, Q_W), BF16),
            jax.ShapeDtypeStruct((B, S, KV_W), BF16),
            jax.ShapeDtypeStruct((B, S, KV_W), BF16),
            jax.ShapeDtypeStruct((B, S, LRU_WIDTH), F32),
            jax.ShapeDtypeStruct((B, S, LRU_WIDTH), F32),
            jax.ShapeDtypeStruct((B, S, D), F32),
            jax.ShapeDtypeStruct((B, S, D), F32),
        ],
        compiler_params=_params("arbitrary", "arbitrary"),
        name="in_proj",
    )(x, g_mix, w_in, qn, kn, cos, sina, sinb)


def _attn_kernel(q_ref, k_ref, v_ref, o_ref):
    k = k_ref[...]
    v = v_ref[...]
    for h in range(Q_GROUP):
        sl = slice(h * HEAD_DIM, (h + 1) * HEAD_DIM)
        s = lax.dot_general(q_ref[:, sl], k, (((1,), (1,)), ((), ())), preferred_element_type=F32)
        p = jnp.exp(s - jnp.max(s, axis=-1, keepdims=True))
        l = jnp.sum(p, axis=-1, keepdims=True)
        o = _bdot(p.astype(BF16), v)
        o_ref[:, sl] = (o * (1.0 / l)).astype(BF16)


def _attention(q, k, v):
    B, S, _ = q.shape
    tq = ATTN_Q_TILE
    return pl.pallas_call(
        _attn_kernel,
        grid=(B, N_KV_HEADS, S // tq),
        in_specs=[
            pl.BlockSpec((None, tq, GROUP_W), lambda b, g, i: (b, i, g)),
            pl.BlockSpec((None, S, HEAD_DIM), lambda b, g, i: (b, 0, g)),
            pl.BlockSpec((None, S, HEAD_DIM), lambda b, g, i: (b, 0, g)),
        ],
        out_specs=pl.BlockSpec((None, tq, GROUP_W), lambda b, g, i: (b, i, g)),
        out_shape=jax.ShapeDtypeStruct((B, S, Q_W), BF16),
        compiler_params=_params("arbitrary", "arbitrary", "arbitrary"),
        name="attention",
    )(q, k, v)


def _gelu_tanh(x):
    return 0.5 * x * (1.0 + jnp.tanh(np.sqrt(2.0 / np.pi) * (x + 0.044715 * (x * x * x))))


def _lru_kernel(*refs, batch, reverse):
    if reverse:
        (xc_ref, xp_ref, xn_ref, cw_ref, cb_ref, w_ref, b_ref, lam_ref, hf_ref, xg_ref,
         o_ref, ext_ref, a_ref, bt_ref, h_ref) = refs
    else:
        (xc_ref, xp_ref, xn_ref, cw_ref, cb_ref, w_ref, b_ref, lam_ref,
         o_ref, ext_ref, a_ref, bt_ref, h_ref) = refs
    step = pl.program_id(0)
    nsteps = pl.num_programs(0)
    chunk = (nsteps - 1 - step) if reverse else step
    rows = LRU_STEPS * batch

    @pl.when(step == 0)
    def _():
        h_ref[...] = jnp.zeros_like(h_ref)

    ext_ref[0:2 * batch, :] = jnp.where(chunk > 0, xp_ref[...], 0.0)
    ext_ref[2 * batch:2 * batch + rows, :] = xc_ref[...]
    ext_ref[2 * batch + rows:3 * batch + rows, :] = jnp.where(chunk < nsteps - 1, xn_ref[...], 0.0)
    c = cb_ref[...] + sum(cw_ref[j:j + 1, :] * ext_ref[j * batch:j * batch + rows, :] for j in range(4))
    cb16 = c.astype(BF16)
    lam = lam_ref[...]
    neg_softplus = -(jnp.maximum(-lam, 0.0) + jnp.log1p(jnp.exp(-jnp.abs(lam))))
    for n in range(LRU_BLOCKS):
        sl = slice(n * LRU_BLOCK_DIM, (n + 1) * LRU_BLOCK_DIM)
        pre = _bdot(cb16[:, sl], w_ref[n]) + b_ref[n]
        r = _sigmoid(pre[:, :LRU_BLOCK_DIM])
        ig = _sigmoid(pre[:, LRU_BLOCK_DIM:])
        log_a = LRU_C * r * neg_softplus[:, sl]
        a = jnp.exp(log_a)
        a_ref[:, sl] = a
        one_minus_a2 = -jnp.tanh(log_a) * (a * a + 1.0)
        root = jnp.where(one_minus_a2 > 0.0, one_minus_a2 * lax.rsqrt(one_minus_a2), 0.0)
        bt_ref[:, sl] = root * ig * c[:, sl]

    def scan_step(t, h):
        tt = (LRU_STEPS - 1 - t) if reverse else t
        rs = pl.ds(pl.multiple_of(tt * batch, batch), batch)
        h = a_ref[rs, :] * h + bt_ref[rs, :]
        if reverse:
            o_ref[rs, :] = ((h + hf_ref[rs, :]) * _gelu_tanh(xg_ref[rs, :])).astype(o_ref.dtype)
        else:
            o_ref[rs, :] = h
        return h

    h_ref[...] = lax.fori_loop(0, LRU_STEPS, scan_step, h_ref[...], unroll=4)


def _lru_scan(xr2, conv_w, conv_b, wcat, bcat, lam, *, batch, reverse, h_fwd=None, xg2=None):
    n_rows, width = xr2.shape
    rows = LRU_STEPS * batch
    n = n_rows // rows
    ch = (lambda i: n - 1 - i) if reverse else (lambda i: i)
    tile = pl.BlockSpec((rows, width), lambda i: (ch(i), 0))
    prev = pl.BlockSpec((2 * batch, width), lambda i: (jnp.maximum(ch(i) * (LRU_STEPS // 2) - 1, 0), 0))
    nxt = pl.BlockSpec((batch, width),
                       lambda i: (jnp.minimum((ch(i) + 1) * LRU_STEPS, n * LRU_STEPS - 1), 0))
    full = lambda a: pl.BlockSpec(a.shape, lambda i: (0,) * a.ndim)
    ins = [xr2, xr2, xr2, conv_w, conv_b, wcat, bcat, lam]
    specs = [tile, prev, nxt, full(conv_w), full(conv_b), full(wcat), full(bcat), full(lam)]
    if reverse:
        ins += [h_fwd, xg2]
        specs += [tile, tile]
    return pl.pallas_call(
        functools.partial(_lru_kernel, batch=batch, reverse=reverse),
        grid=(n,),
        in_specs=specs,
        out_specs=tile,
        out_shape=jax.ShapeDtypeStruct((n_rows, width), BF16 if reverse else F32),
        scratch_shapes=[
            pltpu.VMEM((rows + 3 * batch, width), F32),
            pltpu.VMEM((rows, width), F32),
            pltpu.VMEM((rows, width), F32),
            pltpu.VMEM((batch, width), F32),
        ],
        compiler_params=_params("arbitrary"),
        name="lru_bwd" if reverse else "lru_fwd",
    )(*ins)


def _merge_kernel(x_ref, at_ref, yl_ref, ga_ref, gr_ref, wa_ref, wl_ref, wo_ref, gm_ref, wr_ref, br_ref,
                  h_ref, u_ref, idx_ref, gate_ref, cnt_ref):
    y_attn = _bdot(at_ref[...], wa_ref[...])
    y_lru = _bdot(yl_ref[...], wl_ref[...])
    merged = _sigmoid(ga_ref[...]) * y_attn + _sigmoid(gr_ref[...]) * y_lru
    h = x_ref[...] + _bdot(merged.astype(BF16), wo_ref[...])
    h_ref[...] = h
    u = _rms(h, gm_ref[...])
    _store_row_tiles(u_ref, u)
    u_hi = u.astype(BF16)
    u_lo = (u - u_hi.astype(F32)).astype(BF16)
    logits = _bdot(u_hi, wr_ref[0]) + (_bdot(u_lo, wr_ref[0]) + _bdot(u_hi, wr_ref[1]))
    lane = lax.broadcasted_iota(jnp.int32, logits.shape, 1)
    work = jnp.where(lane < N_EXPERTS, logits + br_ref[...], -jnp.inf)
    vals, hits = [], jnp.zeros(logits.shape, F32)
    for kk in range(TOP_K):
        m = jnp.max(work, axis=-1, keepdims=True)
        idx = jnp.min(jnp.where(work == m, lane, V7X_LANES), axis=-1, keepdims=True)
        pick = lane == idx
        work = jnp.where(pick, -jnp.inf, work)
        hits = hits + pick.astype(F32)
        vals.append(m)
        idx_ref[:, kk:kk + 1] = idx
    es = [jnp.exp(vv - vals[0]) for vv in vals]
    inv = 1.0 / sum(es)
    for kk in range(TOP_K):
        gate_ref[:, kk:kk + 1] = es[kk] * inv

    @pl.when((pl.program_id(0) == 0) & (pl.program_id(1) == 0))
    def _():
        cnt_ref[...] = jnp.zeros_like(cnt_ref)

    cnt_ref[0:1, :] += jnp.sum(hits, axis=0, keepdims=True)


def _merge(x, attn, ylru, ga, gr, wa, wl, wo, g_moe, wr, br):
    B, S, D = x.shape
    tm = TOKEN_TILE
    bm = lambda w: pl.BlockSpec((None, tm, w), lambda b, i: (b, i, 0))
    full = lambda a: pl.BlockSpec(a.shape, lambda b, i: (0,) * a.ndim)
    return pl.pallas_call(
        _merge_kernel,
        grid=(B, S // tm),
        in_specs=[bm(D), bm(Q_W), bm(LRU_WIDTH), bm(D), bm(D), full(wa), full(wl), full(wo),
                  full(g_moe), full(wr), full(br)],
        out_specs=[bm(D), pl.BlockSpec((None, tm * ROW_TILE, V7X_LANES), lambda b, i: (b, i, 0)),
                   bm(TOP_K), bm(TOP_K), pl.BlockSpec((8, V7X_LANES), lambda b, i: (0, 0))],
        out_shape=[
            jax.ShapeDtypeStruct((B, S, D), F32),
            jax.ShapeDtypeStruct((B, S * ROW_TILE, V7X_LANES), F32),
            jax.ShapeDtypeStruct((B, S, TOP_K), jnp.int32),
            jax.ShapeDtypeStruct((B, S, TOP_K), F32),
            jax.ShapeDtypeStruct((8, V7X_LANES), F32),
        ],
        compiler_params=_params("arbitrary", "arbitrary"),
        name="merge_router",
    )(x, attn, ylru, ga, gr, wa, wl, wo, g_moe, wr, br)


def _rank_kernel(idx_ref, ps_ref, dest_ref, carry_ref):
    @pl.when(pl.program_id(0) == 0)
    def _():
        carry_ref[...] = jnp.zeros_like(carry_ref)

    tm = idx_ref.shape[0]
    lane = lax.broadcasted_iota(jnp.int32, (tm, V7X_LANES), 1)
    picks = [lane == idx_ref[:, kk:kk + 1] for kk in range(TOP_K)]
    hits = sum(p.astype(F32) for p in picks)
    earlier = (lax.broadcasted_iota(jnp.int32, (tm, tm), 0) >
               lax.broadcasted_iota(jnp.int32, (tm, tm), 1)).astype(BF16)
    base = _bdot(earlier, hits.astype(BF16)) + carry_ref[...] + ps_ref[...]
    for kk in range(TOP_K):
        dest_ref[:, kk:kk + 1] = jnp.sum(jnp.where(picks[kk], base, 0.0), axis=-1,
                                         keepdims=True).astype(jnp.int32)
    carry_ref[...] += jnp.sum(hits, axis=0, keepdims=True)


def _rank(idx, pad_start_row):
    T = idx.shape[0]
    tm = RANK_TILE
    return pl.pallas_call(
        _rank_kernel,
        grid=(T // tm,),
        in_specs=[pl.BlockSpec((tm, TOP_K), lambda i: (i, 0)),
                  pl.BlockSpec((1, V7X_LANES), lambda i: (0, 0))],
        out_specs=pl.BlockSpec((tm, TOP_K), lambda i: (i, 0)),
        out_shape=jax.ShapeDtypeStruct((T, TOP_K), jnp.int32),
        scratch_shapes=[pltpu.VMEM((1, V7X_LANES), F32)],
        compiler_params=_params("arbitrary"),
        name="slot_rank",
    )(idx, pad_start_row)


def _invert_kernel(pad_end_ref, count_ref, dest_ref, inv_ref, *, n_tokens, chunk):
    step = pl.program_id(0)

    @pl.when(step == 0)
    def _():
        def fill(s, carry):
            inv_ref[s] = TOP_K * n_tokens + (s & (SPARE_ROWS - 1))
            return carry

        for e in range(N_EXPERTS):
            lo = pad_end_ref[e] - ((count_ref[e] + EXPERT_TILE - 1) & -EXPERT_TILE) + count_ref[e]
            lax.fori_loop(lo, pad_end_ref[e], fill, 0)
        lax.fori_loop(pad_end_ref[N_EXPERTS - 1], inv_ref.shape[0], fill, 0)

    group = 4 * TOP_K

    def body(g, carry):
        tok0 = step * (chunk // TOP_K) + g * (group // TOP_K)
        for u in range(group):
            inv_ref[dest_ref[g * group + u]] = tok0 + ((u % TOP_K) * n_tokens + u // TOP_K)
        return carry

    lax.fori_loop(0, chunk // group, body, 0)


def _invert(pad_end, counts, dest_flat, cap, n_tokens):
    chunk = INVERT_CHUNK
    return pl.pallas_call(
        functools.partial(_invert_kernel, n_tokens=n_tokens, chunk=chunk),
        grid_spec=pltpu.PrefetchScalarGridSpec(
            num_scalar_prefetch=2,
            grid=(dest_flat.shape[0] // chunk,),
            in_specs=[pl.BlockSpec((chunk,), lambda i, pe, ct: (i,), memory_space=pltpu.SMEM)],
            out_specs=pl.BlockSpec(memory_space=pltpu.SMEM),
        ),
        out_shape=jax.ShapeDtypeStruct((cap,), jnp.int32),
        compiler_params=_params("arbitrary"),
        name="slot_invert",
    )(pad_end, counts, dest_flat)


def _expert_kernel(be_ref, nu_ref, invc_ref, invn_ref, u_hbm, wgu_ref, bgu_ref, wdn_ref, bdn_ref,
                   y_hbm, xbuf, ybuf, gsem, ssem, *, n_tokens):
    j = pl.program_id(0)
    n_used = nu_ref[0]
    parity = j & 1

    def tile_rows(row):
        if isinstance(row, int):
            return pl.ds(row * ROW_TILE, ROW_TILE)
        return pl.ds(pl.multiple_of(row * ROW_TILE, ROW_TILE), ROW_TILE)

    def gather(inv_ref, r, buf):
        tok = inv_ref[0, 0, r] & (n_tokens - 1)
        return pltpu.make_async_copy(u_hbm.at[tile_rows(tok)], xbuf.at[buf, tile_rows(r)], gsem.at[buf])

    def scatter(inv_ref, r, buf):
        return pltpu.make_async_copy(ybuf.at[buf, tile_rows(r)], y_hbm.at[tile_rows(inv_ref[0, 0, r])],
                                     ssem.at[buf])

    def wait_gather(buf):
        pltpu.make_async_copy(xbuf.at[buf], xbuf.at[buf], gsem.at[buf]).wait()

    def wait_scatter(buf):
        pltpu.make_async_copy(ybuf.at[buf], ybuf.at[buf], ssem.at[buf]).wait()

    @pl.when(j == 0)
    def _():
        def first(r, carry):
            gather(invc_ref, r, 0).start()
            return carry
        lax.fori_loop(0, EXPERT_TILE, first, 0)
        xbuf[1] = jnp.zeros(xbuf.shape[1:], xbuf.dtype)
        clears = [pltpu.make_async_copy(
            xbuf.at[1], y_hbm.at[pl.ds((TOP_K * n_tokens + c * EXPERT_TILE) * ROW_TILE, EXPERT_TILE * ROW_TILE)],
            ssem.at[1]) for c in range(SPARE_ROWS // EXPERT_TILE)]
        for c in clears:
            c.start()
        for c in clears:
            c.wait()

    def block(cur):
        oth = 1 - cur
        wait_gather(cur)

        @pl.when(j > 1)
        def _():
            wait_scatter(cur)

        for r in range(EXPERT_TILE):
            gather(invn_ref, r, oth).start(priority=r % 2)
        hu = _bdot(_load_row_tiles(xbuf.at[cur]).astype(BF16), wgu_ref[...].astype(BF16)) + bgu_ref[...]
        gate = jnp.minimum(hu[:, :D_EXPERT], SWIGLU_LIMIT)
        up = jnp.clip(hu[:, D_EXPERT:], -SWIGLU_LIMIT, SWIGLU_LIMIT)
        act = gate * _sigmoid(SWIGLU_ALPHA * gate)
        _store_row_tiles(ybuf.at[cur],
                         _bdot(((up + 1.0) * act).astype(BF16), wdn_ref[...].astype(BF16)) + bdn_ref[...])
        for r in range(EXPERT_TILE):
            scatter(invc_ref, r, cur).start(priority=r % 2)

        @pl.when(j == n_used - 1)
        def _():
            wait_gather(oth)
            wait_scatter(cur)

            @pl.when(j > 0)
            def _():
                wait_scatter(oth)

    for cur in range(2):
        @pl.when((j < n_used) & (parity == cur))
        def _():
            block(cur)


def _experts(blk_exp, n_used, inv3, u, w_gu, b_gu, w_dn, b_dn):
    T, D = u.shape[0] // ROW_TILE, D_MODEL
    n_blocks = inv3.shape[0]
    exp3 = lambda j, be, nu: (be[j], 0, 0)
    inv_spec = lambda f: pl.BlockSpec((1, 1, EXPERT_TILE), lambda j, be, nu: (f(j, nu), 0, 0),
                                      memory_space=pltpu.SMEM)
    last = lambda j, nu: jnp.minimum(j, nu[0] - 1)
    return pl.pallas_call(
        functools.partial(_expert_kernel, n_tokens=T),
        grid_spec=pltpu.PrefetchScalarGridSpec(
            num_scalar_prefetch=2,
            grid=(n_blocks,),
            in_specs=[inv_spec(last),
                      inv_spec(lambda j, nu: jnp.minimum(j + 1, nu[0] - 1)),
                      pl.BlockSpec(memory_space=pl.ANY),
                      pl.BlockSpec((None, D, 2 * D_EXPERT), exp3),
                      pl.BlockSpec((None, 1, 2 * D_EXPERT), exp3),
                      pl.BlockSpec((None, D_EXPERT, D), exp3),
                      pl.BlockSpec((None, 1, D), exp3)],
            out_specs=pl.BlockSpec(memory_space=pl.ANY),
            scratch_shapes=[pltpu.VMEM((2, EXPERT_TILE * ROW_TILE, V7X_LANES), F32),
                            pltpu.VMEM((2, EXPERT_TILE * ROW_TILE, V7X_LANES), F32),
                            pltpu.SemaphoreType.DMA((2,)), pltpu.SemaphoreType.DMA((2,))],
        ),
        out_shape=jax.ShapeDtypeStruct(((TOP_K * T + SPARE_ROWS) * ROW_TILE, V7X_LANES), F32),
        compiler_params=_params("arbitrary"),
        name="experts",
    )(blk_exp, n_used, inv3, inv3, u, w_gu, b_gu, w_dn, b_dn)


def _combine_kernel(y0_ref, y1_ref, y2_ref, y3_ref, gate_ref, h_ref, p_ref, gp_ref, wg_ref, wp_ref, o_ref):
    h = h_ref[...]
    for kk, y_ref in enumerate((y0_ref, y1_ref, y2_ref, y3_ref)):
        h = h + gate_ref[:, kk:kk + 1] * _load_row_tiles(y_ref)
    u = _rms(h, gp_ref[...]).astype(BF16)
    o_ref[...] = h + _sigmoid(_bdot(u, wg_ref[...])) * _bdot(p_ref[...].astype(BF16), wp_ref[...])


def _combine(y4, gate, h, p, g_ple, wg, wp):
    T, D = h.shape
    tm = TOKEN_TILE
    full = lambda a: pl.BlockSpec(a.shape, lambda i: (0,) * a.ndim)
    plane = lambda kk: pl.BlockSpec((tm * ROW_TILE, V7X_LANES), lambda i: (kk * (T // tm) + i, 0))
    return pl.pallas_call(
        _combine_kernel,
        grid=(T // tm,),
        in_specs=[plane(0), plane(1), plane(2), plane(3),
                  pl.BlockSpec((tm, TOP_K), lambda i: (i, 0)),
                  pl.BlockSpec((tm, D), lambda i: (i, 0)),
                  pl.BlockSpec((tm, PLE_DIM), lambda i: (i, 0)),
                  full(g_ple), full(wg), full(wp)],
        out_specs=pl.BlockSpec((tm, D), lambda i: (i, 0)),
        out_shape=jax.ShapeDtypeStruct((T, D), F32),
        compiler_params=_params("arbitrary"),
        name="combine_ple",
    )(y4, y4, y4, y4, gate, h, p, g_ple, wg, wp)


def _rope_tables(seq_len):
    pos = jnp.arange(seq_len)
    axis_dim = HEAD_DIM // 2
    inv_freq = ROPE_THETA ** (-jnp.arange(0, axis_dim, 2, dtype=F32) / axis_dim)
    ang_r = (pos // GRID_W).astype(F32)[:, None] * inv_freq[None, :]
    ang_c = (pos % GRID_W).astype(F32)[:, None] * inv_freq[None, :]
    zero = jnp.zeros_like(ang_r)
    cos = jnp.concatenate([jnp.cos(ang_r)] * 2 + [jnp.cos(ang_c)] * 2, axis=-1)
    sina = jnp.concatenate([-jnp.sin(ang_r), zero, -jnp.sin(ang_c), zero], axis=-1)
    sinb = jnp.concatenate([zero, jnp.sin(ang_r), zero, jnp.sin(ang_c)], axis=-1)
    return cos, sina, sinb


def _layer(h, p, g_mix, w_in, q_norm, k_norm, conv_w, conv_b, lru_wa, lru_ba, lru_wi, lru_bi, lru_lam,
           w_attn_br, w_lru_br, w_out, g_moe, w_router, b_router, w_gu, b_gu, w_dn, b_dn,
           g_ple, w_ple_gate, w_ple_proj, rope):
    B, S, D = h.shape
    T = B * S
    row = lambda a: a.reshape(1, -1)
    q, k, v, xr, xg, ga, gr = _in_proj(
        h, row(g_mix), w_in.astype(BF16), row(q_norm) * HEAD_DIM ** -0.5, row(k_norm), *rope)
    attn = _attention(q, k, v)
    xr2 = xr.transpose(1, 0, 2).reshape(S * B, LRU_WIDTH)
    xg2 = xg.transpose(1, 0, 2).reshape(S * B, LRU_WIDTH)
    wcat = jnp.concatenate([lru_wa, lru_wi], axis=-1).astype(BF16)
    bcat = jnp.concatenate([lru_ba.reshape(2, LRU_BLOCKS, 1, LRU_BLOCK_DIM),
                            lru_bi.reshape(2, LRU_BLOCKS, 1, LRU_BLOCK_DIM)], axis=-1)
    h_fwd = _lru_scan(xr2, conv_w, row(conv_b), wcat[0], bcat[0], row(lru_lam[0]), batch=B, reverse=False)
    ylru = _lru_scan(xr2, conv_w, row(conv_b), wcat[1], bcat[1], row(lru_lam[1]), batch=B, reverse=True,
                     h_fwd=h_fwd, xg2=xg2).reshape(S, B, LRU_WIDTH).transpose(1, 0, 2)
    w_router_f = jnp.zeros((D, V7X_LANES), F32).at[:, :N_EXPERTS].set(w_router)
    w_router_hi = w_router_f.astype(BF16)
    w_router_p = jnp.stack([w_router_hi, (w_router_f - w_router_hi.astype(F32)).astype(BF16)])
    b_router_p = jnp.zeros((1, V7X_LANES), F32).at[0, :N_EXPERTS].set(b_router)
    h1, u2, idx, gate, cnt = _merge(h, attn, ylru, ga, gr, w_attn_br.astype(BF16), w_lru_br.astype(BF16),
                                    w_out.astype(BF16), row(g_moe), w_router_p, b_router_p)
    cap = T * TOP_K + N_EXPERTS * EXPERT_TILE
    counts = cnt[0, :N_EXPERTS].astype(jnp.int32)
    padded = ((counts + EXPERT_TILE - 1) // EXPERT_TILE) * EXPERT_TILE
    pad_end = jnp.cumsum(padded)
    pad_start = pad_end - padded
    blk_start = jnp.arange(cap // EXPERT_TILE, dtype=jnp.int32) * EXPERT_TILE
    blk_exp = jnp.minimum(jnp.sum(blk_start[:, None] >= pad_end[None, :], axis=1), N_EXPERTS - 1)
    n_used = (pad_end[-1:] // EXPERT_TILE).astype(jnp.int32)
    ps_row = jnp.zeros((1, V7X_LANES), F32).at[0, :N_EXPERTS].set(pad_start.astype(F32))
    dest = _rank(idx.reshape(T, TOP_K), ps_row).reshape(T * TOP_K)
    assert T & (T - 1) == 0, "token count must be a power of two for the slot encoding"
    inv3 = _invert(pad_end.astype(jnp.int32), counts, dest, cap, T).reshape(cap // EXPERT_TILE, 1, EXPERT_TILE)
    y4 = _experts(blk_exp.astype(jnp.int32), n_used, inv3, u2.reshape(T * ROW_TILE, V7X_LANES), w_gu,
                  b_gu.reshape(N_EXPERTS, 1, -1), w_dn, b_dn.reshape(N_EXPERTS, 1, -1))
    out = _combine(y4, gate.reshape(T, TOP_K), h1.reshape(T, D),
                   p.reshape(T, PLE_DIM), row(g_ple), w_ple_gate.astype(BF16), w_ple_proj.astype(BF16))
    return out.reshape(B, S, D)


def kernel(x, p, g_mix, w_in, q_norm, k_norm, conv_w, conv_b, lru_wa, lru_ba, lru_wi, lru_bi, lru_lam,
           w_attn_br, w_lru_br, w_out, g_moe, w_router, b_router, w_gu, b_gu, w_dn, b_dn,
           g_ple, w_ple_gate, w_ple_proj):
    rope = _rope_tables(x.shape[1])
    h = x
    for l in range(p.shape[0]):
        h = _layer(h, p[l], g_mix[l], w_in[l], q_norm[l], k_norm[l], conv_w[l], conv_b[l], lru_wa[l],
                   lru_ba[l], lru_wi[l], lru_bi[l], lru_lam[l], w_attn_br[l], w_lru_br[l], w_out[l],
                   g_moe[l], w_router[l], b_router[l], w_gu[l], b_gu[l], w_dn[l], b_dn[l],
                   g_ple[l], w_ple_gate[l], w_ple_proj[l], rope)
    return h
```

```python
import functools

import jax
import jax.numpy as jnp
import numpy as np
from jax import lax
from jax.experimental import pallas as pl
from jax.experimental.pallas import tpu as pltpu

D_MODEL = 1024
N_Q_HEADS = 8
N_KV_HEADS = 2
HEAD_DIM = 128
Q_GROUP = N_Q_HEADS // N_KV_HEADS
ROPE_THETA = 10000.0
GRID_W = 64
LRU_WIDTH = 1024
LRU_BLOCKS = 8
LRU_BLOCK_DIM = LRU_WIDTH // LRU_BLOCKS
LRU_C = 8.0
N_EXPERTS = 32
TOP_K = 4
D_EXPERT = 1024
SWIGLU_LIMIT = 7.0
SWIGLU_ALPHA = 1.702
PLE_DIM = 256
EPS = 1e-6
Q_W = N_Q_HEADS * HEAD_DIM
KV_W = N_KV_HEADS * HEAD_DIM
GROUP_W = Q_GROUP * HEAD_DIM
IN_WIDTH = Q_W + 2 * KV_W + 2 * LRU_WIDTH + 2 * D_MODEL
OFF_Q = 0
OFF_K = OFF_Q + Q_W
OFF_V = OFF_K + KV_W
OFF_XR = OFF_V + KV_W
OFF_XG = OFF_XR + LRU_WIDTH
OFF_GA = OFF_XG + LRU_WIDTH
OFF_GR = OFF_GA + D_MODEL

V7X_LANES = 128
V7X_VMEM_LIMIT_BYTES = 56 * 1024 * 1024
TOKEN_TILE = 512
ATTN_Q_TILE = 512
LRU_STEPS = 32
EXPERT_TILE = 512
RANK_TILE = 512
INVERT_CHUNK = 8192
SPARE_ROWS = 1024

F32 = jnp.float32
BF16 = jnp.bfloat16


def _params(*sem):
    return pltpu.CompilerParams(dimension_semantics=sem, vmem_limit_bytes=V7X_VMEM_LIMIT_BYTES)


def _bdot(a, b):
    return jnp.dot(a, b, preferred_element_type=F32)


def _rms(x, g):
    return x * lax.rsqrt(jnp.mean(x * x, axis=-1, keepdims=True) + EPS) * g


ROW_TILE = 8


def _store_row_tiles(ref, x):
    n = x.shape[0]
    for j in range(ROW_TILE):
        ref[pl.ds(j, n, stride=ROW_TILE), :] = x[:, j * V7X_LANES:(j + 1) * V7X_LANES]


def _load_row_tiles(ref):
    n = ref.shape[0] // ROW_TILE
    return jnp.concatenate([ref[pl.ds(j, n, stride=ROW_TILE), :] for j in range(ROW_TILE)], axis=-1)


def _sigmoid(x):
    return 0.5 * jnp.tanh(0.5 * x) + 0.5


def _in_proj_kernel(x_ref, g_ref, w_ref, qn_ref, kn_ref, cos_ref, sina_ref, sinb_ref,
                    q_ref, k_ref, v_ref, xr_ref, xg_ref, ga_ref, gr_ref):
    u = _rms(x_ref[...], g_ref[...]).astype(BF16)
    cos, sina, sinb = cos_ref[...], sina_ref[...], sinb_ref[...]

    def head(z, gain):
        y = _rms(z, gain)
        return y * cos + pltpu.roll(y, 96, 1) * sina + pltpu.roll(y, 32, 1) * sinb

    zq = _bdot(u, w_ref[:, OFF_Q:OFF_Q + Q_W])
    for h in range(N_Q_HEADS):
        sl = slice(h * HEAD_DIM, (h + 1) * HEAD_DIM)
        q_ref[:, sl] = head(zq[:, sl], qn_ref[...]).astype(BF16)
    zk = _bdot(u, w_ref[:, OFF_K:OFF_K + KV_W])
    for h in range(N_KV_HEADS):
        sl = slice(h * HEAD_DIM, (h + 1) * HEAD_DIM)
        k_ref[:, sl] = head(zk[:, sl], kn_ref[...]).astype(BF16)
    v_ref[...] = _bdot(u, w_ref[:, OFF_V:OFF_V + KV_W]).astype(BF16)
    xr_ref[...] = _bdot(u, w_ref[:, OFF_XR:OFF_XR + LRU_WIDTH])
    xg_ref[...] = _bdot(u, w_ref[:, OFF_XG:OFF_XG + LRU_WIDTH])
    ga_ref[...] = _bdot(u, w_ref[:, OFF_GA:OFF_GA + D_MODEL])
    gr_ref[...] = _bdot(u, w_ref[:, OFF_GR:OFF_GR + D_MODEL])


def _in_proj(x, g_mix, w_in, qn, kn, cos, sina, sinb):
    B, S, D = x.shape
    tm = TOKEN_TILE
    bm = lambda w: pl.BlockSpec((None, tm, w), lambda b, i: (b, i, 0))
    full = lambda a: pl.BlockSpec(a.shape, lambda b, i: (0,) * a.ndim)
    tab = pl.BlockSpec((tm, HEAD_DIM), lambda b, i: (i, 0))
    return pl.pallas_call(
        _in_proj_kernel,
        grid=(B, S // tm),
        in_specs=[bm(D), full(g_mix), full(w_in), full(qn), full(kn), tab, tab, tab],
        out_specs=[bm(Q_W), bm(KV_W), bm(KV_W), bm(LRU_WIDTH), bm(LRU_WIDTH), bm(D), bm(D)],
        out_shape=[
            jax.ShapeDtypeStruct((B, S, Q_W), BF16),
            jax.ShapeDtypeStruct((B, S, KV_W), BF16),
            jax.ShapeDtypeStruct((B, S, KV_W), BF16),
            jax.ShapeDtypeStruct((B, S, LRU_WIDTH), F32),
            jax.ShapeDtypeStruct((B, S, LRU_WIDTH), F32),
            jax.ShapeDtypeStruct((B, S, D), F32),
            jax.ShapeDtypeStruct((B, S, D), F32),
        ],
        compiler_params=_params("arbitrary", "arbitrary"),
        name="in_proj",
    )(x, g_mix, w_in, qn, kn, cos, sina, sinb)


def _attn_kernel(q_ref, k_ref, v_ref, o_ref):
    k = k_ref[...]
    v1 = jnp.concatenate([v_ref[...], jnp.ones(v_ref.shape, v_ref.dtype)], axis=-1)
    for h in range(Q_GROUP):
        sl = slice(h * HEAD_DIM, (h + 1) * HEAD_DIM)
        s = lax.dot_general(q_ref[:, sl], k, (((1,), (1,)), ((), ())), preferred_element_type=F32)
        p = jnp.exp2(s - jnp.max(s, axis=-1, keepdims=True))
        o = _bdot(p.astype(BF16), v1)
        o_ref[:, sl] = (o[:, :HEAD_DIM] * (1.0 / o[:, HEAD_DIM:HEAD_DIM + 1])).astype(BF16)


def _attention(q, k, v):
    B, S, _ = q.shape
    tq = ATTN_Q_TILE
    return pl.pallas_call(
        _attn_kernel,
        grid=(B, N_KV_HEADS, S // tq),
        in_specs=[
            pl.BlockSpec((None, tq, GROUP_W), lambda b, g, i: (b, i, g)),
            pl.BlockSpec((None, S, HEAD_DIM), lambda b, g, i: (b, 0, g)),
            pl.BlockSpec((None, S, HEAD_DIM), lambda b, g, i: (b, 0, g)),
        ],
        out_specs=pl.BlockSpec((None, tq, GROUP_W), lambda b, g, i: (b, i, g)),
        out_shape=jax.ShapeDtypeStruct((B, S, Q_W), BF16),
        compiler_params=_params("arbitrary", "arbitrary", "arbitrary"),
        name="attention",
    )(q, k, v)


def _gelu_tanh(x):
    return 0.5 * x * (1.0 + jnp.tanh(np.sqrt(2.0 / np.pi) * (x + 0.044715 * (x * x * x))))


def _lru_kernel(*refs, batch, reverse):
    if reverse:
        (xc_ref, xp_ref, xn_ref, cw_ref, cb_ref, w_ref, b_ref, lam_ref, hf_ref, xg_ref,
         o_ref, ext_ref, a_ref, bt_ref, h_ref) = refs
    else:
        (xc_ref, xp_ref, xn_ref, cw_ref, cb_ref, w_ref, b_ref, lam_ref,
         o_ref, ext_ref, a_ref, bt_ref, h_ref) = refs
    step = pl.program_id(0)
    nsteps = pl.num_programs(0)
    chunk = (nsteps - 1 - step) if reverse else step
    rows = LRU_STEPS * batch

    @pl.when(step == 0)
    def _():
        h_ref[...] = jnp.zeros_like(h_ref)

    ext_ref[0:2 * batch, :] = jnp.where(chunk > 0, xp_ref[...], 0.0)
    ext_ref[2 * batch:2 * batch + rows, :] = xc_ref[...]
    ext_ref[2 * batch + rows:3 * batch + rows, :] = jnp.where(chunk < nsteps - 1, xn_ref[...], 0.0)
    c = cb_ref[...] + sum(cw_ref[j:j + 1, :] * ext_ref[j * batch:j * batch + rows, :] for j in range(4))
    cb16 = c.astype(BF16)
    lam = lam_ref[...]
    neg_softplus = -(jnp.maximum(-lam, 0.0) + jnp.log1p(jnp.exp(-jnp.abs(lam))))
    for n in range(LRU_BLOCKS):
        sl = slice(n * LRU_BLOCK_DIM, (n + 1) * LRU_BLOCK_DIM)
        pre = _bdot(cb16[:, sl], w_ref[n]) + b_ref[n]
        r = _sigmoid(pre[:, :LRU_BLOCK_DIM])
        ig = _sigmoid(pre[:, LRU_BLOCK_DIM:])
        log_a = LRU_C * r * neg_softplus[:, sl]
        a = jnp.exp(log_a)
        a_ref[:, sl] = a
        one_minus_a2 = -jnp.tanh(log_a) * (a * a + 1.0)
        root = jnp.where(one_minus_a2 > 0.0, one_minus_a2 * lax.rsqrt(one_minus_a2), 0.0)
        bt_ref[:, sl] = root * ig * c[:, sl]

    def scan_step(t, h):
        tt = (LRU_STEPS - 1 - t) if reverse else t
        rs = pl.ds(pl.multiple_of(tt * batch, batch), batch)
        h = a_ref[rs, :] * h + bt_ref[rs, :]
        if reverse:
            o_ref[rs, :] = ((h + hf_ref[rs, :]) * _gelu_tanh(xg_ref[rs, :])).astype(o_ref.dtype)
        else:
            o_ref[rs, :] = h
        return h

    h_ref[...] = lax.fori_loop(0, LRU_STEPS, scan_step, h_ref[...], unroll=4)


def _lru_scan(xr2, conv_w, conv_b, wcat, bcat, lam, *, batch, reverse, h_fwd=None, xg2=None):
    n_rows, width = xr2.shape
    rows = LRU_STEPS * batch
    n = n_rows // rows
    ch = (lambda i: n - 1 - i) if reverse else (lambda i: i)
    tile = pl.BlockSpec((rows, width), lambda i: (ch(i), 0))
    prev = pl.BlockSpec((2 * batch, width), lambda i: (jnp.maximum(ch(i) * (LRU_STEPS // 2) - 1, 0), 0))
    nxt = pl.BlockSpec((batch, width),
                       lambda i: (jnp.minimum((ch(i) + 1) * LRU_STEPS, n * LRU_STEPS - 1), 0))
    full = lambda a: pl.BlockSpec(a.shape, lambda i: (0,) * a.ndim)
    ins = [xr2, xr2, xr2, conv_w, conv_b, wcat, bcat, lam]
    specs = [tile, prev, nxt, full(conv_w), full(conv_b), full(wcat), full(bcat), full(lam)]
    if reverse:
        ins += [h_fwd, xg2]
        specs += [tile, tile]
    return pl.pallas_call(
        functools.partial(_lru_kernel, batch=batch, reverse=reverse),
        grid=(n,),
        in_specs=specs,
        out_specs=tile,
        out_shape=jax.ShapeDtypeStruct((n_rows, width), BF16 if reverse else F32),
        scratch_shapes=[
            pltpu.VMEM((rows + 3 * batch, width), F32),
            pltpu.VMEM((rows, width), F32),
            pltpu.VMEM((rows, width), F32),
            pltpu.VMEM((batch, width), F32),
        ],
        compiler_params=_params("arbitrary"),
        name="lru_bwd" if reverse else "lru_fwd",
    )(*ins)


def _merge_kernel(x_ref, at_ref, yl_ref, ga_ref, gr_ref, wa_ref, wl_ref, wo_ref, gm_ref, wr_ref, br_ref,
                  h_ref, u_ref, idx_ref, gate_ref, cnt_ref):
    y_attn = _bdot(at_ref[...], wa_ref[...])
    y_lru = _bdot(yl_ref[...], wl_ref[...])
    merged = _sigmoid(ga_ref[...]) * y_attn + _sigmoid(gr_ref[...]) * y_lru
    h = x_ref[...] + _bdot(merged.astype(BF16), wo_ref[...])
    h_ref[...] = h
    u = _rms(h, gm_ref[...])
    _store_row_tiles(u_ref, u)
    u_hi = u.astype(BF16)
    u_lo = (u - u_hi.astype(F32)).astype(BF16)
    logits = _bdot(u_hi, wr_ref[0]) + (_bdot(u_lo, wr_ref[0]) + _bdot(u_hi, wr_ref[1]))
    lane = lax.broadcasted_iota(jnp.int32, logits.shape, 1)
    work = jnp.where(lane < N_EXPERTS, logits + br_ref[...], -jnp.inf)
    vals, hits = [], jnp.zeros(logits.shape, F32)
    for kk in range(TOP_K):
        m = jnp.max(work, axis=-1, keepdims=True)
        idx = jnp.min(jnp.where(work == m, lane, V7X_LANES), axis=-1, keepdims=True)
        pick = lane == idx
        work = jnp.where(pick, -jnp.inf, work)
        hits = hits + pick.astype(F32)
        vals.append(m)
        idx_ref[:, kk:kk + 1] = idx
    es = [jnp.exp(vv - vals[0]) for vv in vals]
    inv = 1.0 / sum(es)
    for kk in range(TOP_K):
        gate_ref[:, kk:kk + 1] = es[kk] * inv

    @pl.when((pl.program_id(0) == 0) & (pl.program_id(1) == 0))
    def _():
        cnt_ref[...] = jnp.zeros_like(cnt_ref)

    cnt_ref[0:1, :] += jnp.sum(hits, axis=0, keepdims=True)


def _merge(x, attn, ylru, ga, gr, wa, wl, wo, g_moe, wr, br):
    B, S, D = x.shape
    tm = TOKEN_TILE
    bm = lambda w: pl.BlockSpec((None, tm, w), lambda b, i: (b, i, 0))
    full = lambda a: pl.BlockSpec(a.shape, lambda b, i: (0,) * a.ndim)
    return pl.pallas_call(
        _merge_kernel,
        grid=(B, S // tm),
        in_specs=[bm(D), bm(Q_W), bm(LRU_WIDTH), bm(D), bm(D), full(wa), full(wl), full(wo),
                  full(g_moe), full(wr), full(br)],
        out_specs=[bm(D), pl.BlockSpec((None, tm * ROW_TILE, V7X_LANES), lambda b, i: (b, i, 0)),
                   bm(TOP_K), bm(TOP_K), pl.BlockSpec((8, V7X_LANES), lambda b, i: (0, 0))],
        out_shape=[
            jax.ShapeDtypeStruct((B, S, D), F32),
            jax.ShapeDtypeStruct((B, S * ROW_TILE, V7X_LANES), F32),
            jax.ShapeDtypeStruct((B, S, TOP_K), jnp.int32),
            jax.ShapeDtypeStruct((B, S, TOP_K), F32),
            jax.ShapeDtypeStruct((8, V7X_LANES), F32),
        ],
        compiler_params=_params("arbitrary", "arbitrary"),
        name="merge_router",
    )(x, attn, ylru, ga, gr, wa, wl, wo, g_moe, wr, br)


def _rank_kernel(idx_ref, ps_ref, dest_ref, carry_ref):
    @pl.when(pl.program_id(0) == 0)
    def _():
        carry_ref[...] = jnp.zeros_like(carry_ref)

    tm = idx_ref.shape[0]
    lane = lax.broadcasted_iota(jnp.int32, (tm, V7X_LANES), 1)
    picks = [lane == idx_ref[:, kk:kk + 1] for kk in range(TOP_K)]
    hits = sum(p.astype(F32) for p in picks)
    earlier = (lax.broadcasted_iota(jnp.int32, (tm, tm), 0) >
               lax.broadcasted_iota(jnp.int32, (tm, tm), 1)).astype(BF16)
    base = _bdot(earlier, hits.astype(BF16)) + carry_ref[...] + ps_ref[...]
    for kk in range(TOP_K):
        dest_ref[:, kk:kk + 1] = jnp.sum(jnp.where(picks[kk], base, 0.0), axis=-1,
                                         keepdims=True).astype(jnp.int32)
    carry_ref[...] += jnp.sum(hits, axis=0, keepdims=True)


def _rank(idx, pad_start_row):
    T = idx.shape[0]
    tm = RANK_TILE
    return pl.pallas_call(
        _rank_kernel,
        grid=(T // tm,),
        in_specs=[pl.BlockSpec((tm, TOP_K), lambda i: (i, 0)),
                  pl.BlockSpec((1, V7X_LANES), lambda i: (0, 0))],
        out_specs=pl.BlockSpec((tm, TOP_K), lambda i: (i, 0)),
        out_shape=jax.ShapeDtypeStruct((T, TOP_K), jnp.int32),
        scratch_shapes=[pltpu.VMEM((1, V7X_LANES), F32)],
        compiler_params=_params("arbitrary"),
        name="slot_rank",
    )(idx, pad_start_row)


def _invert_kernel(pad_end_ref, count_ref, dest_vmem, out_vmem, dest_ref, inv_ref, sem, *, n_tokens, chunk):
    step = pl.program_id(0)
    stage_in = pltpu.make_async_copy(dest_vmem, dest_ref, sem)
    stage_in.start()
    stage_in.wait()

    @pl.when(step == 0)
    def _():
        def fill(s, carry):
            inv_ref[s] = TOP_K * n_tokens + (s & (SPARE_ROWS - 1))
            return carry

        for e in range(N_EXPERTS):
            lo = pad_end_ref[e] - ((count_ref[e] + EXPERT_TILE - 1) & -EXPERT_TILE) + count_ref[e]
            lax.fori_loop(lo, pad_end_ref[e], fill, 0)
        lax.fori_loop(pad_end_ref[N_EXPERTS - 1], inv_ref.shape[0], fill, 0)

    group = 4 * TOP_K

    def body(g, carry):
        tok0 = step * (chunk // TOP_K) + g * (group // TOP_K)
        for u in range(group):
            inv_ref[dest_ref[g * group + u]] = tok0 + ((u % TOP_K) * n_tokens + u // TOP_K)
        return carry

    lax.fori_loop(0, chunk // group, body, 0)

    @pl.when(step == pl.num_programs(0) - 1)
    def _():
        stage_out = pltpu.make_async_copy(inv_ref, out_vmem, sem)
        stage_out.start()
        stage_out.wait()


def _invert(pad_end, counts, dest_flat, cap, n_tokens):
    chunk = INVERT_CHUNK
    return pl.pallas_call(
        functools.partial(_invert_kernel, n_tokens=n_tokens, chunk=chunk),
        grid_spec=pltpu.PrefetchScalarGridSpec(
            num_scalar_prefetch=2,
            grid=(dest_flat.shape[0] // chunk,),
            in_specs=[pl.BlockSpec((chunk,), lambda i, pe, ct: (i,))],
            out_specs=pl.BlockSpec((cap,), lambda i, pe, ct: (0,)),
            scratch_shapes=[pltpu.SMEM((chunk,), jnp.int32), pltpu.SMEM((cap,), jnp.int32),
                            pltpu.SemaphoreType.DMA(())],
        ),
        out_shape=jax.ShapeDtypeStruct((cap,), jnp.int32),
        compiler_params=_params("arbitrary"),
        name="slot_invert",
    )(pad_end, counts, dest_flat)


def _expert_kernel(be_ref, nu_ref, invc_ref, invn_ref, u_hbm, wgu_ref, bgu_ref, wdn_ref, bdn_ref,
                   y_hbm, xbuf, ybuf, gsem, ssem, *, n_tokens):
    j = pl.program_id(0)
    n_used = nu_ref[0]
    parity = j & 1

    def tile_rows(row):
        if isinstance(row, int):
            return pl.ds(row * ROW_TILE, ROW_TILE)
        return pl.ds(pl.multiple_of(row * ROW_TILE, ROW_TILE), ROW_TILE)

    def gather(inv_ref, r, buf):
        tok = inv_ref[0, 0, r] & (n_tokens - 1)
        return pltpu.make_async_copy(u_hbm.at[tile_rows(tok)], xbuf.at[buf, tile_rows(r)], gsem.at[buf])

    def scatter(inv_ref, r, buf):
        return pltpu.make_async_copy(ybuf.at[buf, tile_rows(r)], y_hbm.at[tile_rows(inv_ref[0, 0, r])],
                                     ssem.at[buf])

    def wait_gather(buf):
        pltpu.make_async_copy(xbuf.at[buf], xbuf.at[buf], gsem.at[buf]).wait()

    def wait_scatter(buf):
        pltpu.make_async_copy(ybuf.at[buf], ybuf.at[buf], ssem.at[buf]).wait()

    @pl.when(j == 0)
    def _():
        def first(r, carry):
            gather(invc_ref, r, 0).start()
            return carry
        lax.fori_loop(0, EXPERT_TILE, first, 0)
        xbuf[1] = jnp.zeros(xbuf.shape[1:], xbuf.dtype)
        clears = [pltpu.make_async_copy(
            xbuf.at[1], y_hbm.at[pl.ds((TOP_K * n_tokens + c * EXPERT_TILE) * ROW_TILE, EXPERT_TILE * ROW_TILE)],
            ssem.at[1]) for c in range(SPARE_ROWS // EXPERT_TILE)]
        for c in clears:
            c.start()
        for c in clears:
            c.wait()

    def block(cur):
        oth = 1 - cur
        wait_gather(cur)

        @pl.when(j > 1)
        def _():
            wait_scatter(cur)

        for r in range(EXPERT_TILE):
            gather(invn_ref, r, oth).start(priority=r % 2)
        hu = _bdot(_load_row_tiles(xbuf.at[cur]).astype(BF16), wgu_ref[...].astype(BF16)) + bgu_ref[...]
        gate = jnp.minimum(hu[:, :D_EXPERT], SWIGLU_LIMIT)
        up = jnp.clip(hu[:, D_EXPERT:], -SWIGLU_LIMIT, SWIGLU_LIMIT)
        act = gate * _sigmoid(SWIGLU_ALPHA * gate)
        _store_row_tiles(ybuf.at[cur],
                         _bdot(((up + 1.0) * act).astype(BF16), wdn_ref[...].astype(BF16)) + bdn_ref[...])
        for r in range(EXPERT_TILE):
            scatter(invc_ref, r, cur).start(priority=r % 2)

        @pl.when(j == n_used - 1)
        def _():
            wait_gather(oth)
            wait_scatter(cur)

            @pl.when(j > 0)
            def _():
                wait_scatter(oth)

    for cur in range(2):
        @pl.when((j < n_used) & (parity == cur))
        def _():
            block(cur)


def _experts(blk_exp, n_used, inv3, u, w_gu, b_gu, w_dn, b_dn):
    T, D = u.shape[0] // ROW_TILE, D_MODEL
    n_blocks = inv3.shape[0]
    exp3 = lambda j, be, nu: (be[j], 0, 0)
    inv_spec = lambda f: pl.BlockSpec((1, 1, EXPERT_TILE), lambda j, be, nu: (f(j, nu), 0, 0),
                                      memory_space=pltpu.SMEM)
    last = lambda j, nu: jnp.minimum(j, nu[0] - 1)
    return pl.pallas_call(
        functools.partial(_expert_kernel, n_tokens=T),
        grid_spec=pltpu.PrefetchScalarGridSpec(
            num_scalar_prefetch=2,
            grid=(n_blocks,),
            in_specs=[inv_spec(last),
                      inv_spec(lambda j, nu: jnp.minimum(j + 1, nu[0] - 1)),
                      pl.BlockSpec(memory_space=pl.ANY),
                      pl.BlockSpec((None, D, 2 * D_EXPERT), exp3),
                      pl.BlockSpec((None, 1, 2 * D_EXPERT), exp3),
                      pl.BlockSpec((None, D_EXPERT, D), exp3),
                      pl.BlockSpec((None, 1, D), exp3)],
            out_specs=pl.BlockSpec(memory_space=pl.ANY),
            scratch_shapes=[pltpu.VMEM((2, EXPERT_TILE * ROW_TILE, V7X_LANES), F32),
                            pltpu.VMEM((2, EXPERT_TILE * ROW_TILE, V7X_LANES), F32),
                            pltpu.SemaphoreType.DMA((2,)), pltpu.SemaphoreType.DMA((2,))],
        ),
        out_shape=jax.ShapeDtypeStruct(((TOP_K * T + SPARE_ROWS) * ROW_TILE, V7X_LANES), F32),
        compiler_params=_params("arbitrary"),
        name="experts",
    )(blk_exp, n_used, inv3, inv3, u, w_gu, b_gu, w_dn, b_dn)


def _combine_kernel(y0_ref, y1_ref, y2_ref, y3_ref, gate_ref, h_ref, p_ref, gp_ref, wg_ref, wp_ref, o_ref):
    h = h_ref[...]
    for kk, y_ref in enumerate((y0_ref, y1_ref, y2_ref, y3_ref)):
        h = h + gate_ref[:, kk:kk + 1] * _load_row_tiles(y_ref)
    u = _rms(h, gp_ref[...]).astype(BF16)
    o_ref[...] = h + _sigmoid(_bdot(u, wg_ref[...])) * _bdot(p_ref[...].astype(BF16), wp_ref[...])


def _combine(y4, gate, h, p, g_ple, wg, wp):
    T, D = h.shape
    tm = TOKEN_TILE
    full = lambda a: pl.BlockSpec(a.shape, lambda i: (0,) * a.ndim)
    plane = lambda kk: pl.BlockSpec((tm * ROW_TILE, V7X_LANES), lambda i: (kk * (T // tm) + i, 0))
    return pl.pallas_call(
        _combine_kernel,
        grid=(T // tm,),
        in_specs=[plane(0), plane(1), plane(2), plane(3),
                  pl.BlockSpec((tm, TOP_K), lambda i: (i, 0)),
                  pl.BlockSpec((tm, D), lambda i: (i, 0)),
                  pl.BlockSpec((tm, PLE_DIM), lambda i: (i, 0)),
                  full(g_ple), full(wg), full(wp)],
        out_specs=pl.BlockSpec((tm, D), lambda i: (i, 0)),
        out_shape=jax.ShapeDtypeStruct((T, D), F32),
        compiler_params=_params("arbitrary"),
        name="combine_ple",
    )(y4, y4, y4, y4, gate, h, p, g_ple, wg, wp)


def _rope_tables(seq_len):
    pos = jnp.arange(seq_len)
    axis_dim = HEAD_DIM // 2
    inv_freq = ROPE_THETA ** (-jnp.arange(0, axis_dim, 2, dtype=F32) / axis_dim)
    ang_r = (pos // GRID_W).astype(F32)[:, None] * inv_freq[None, :]
    ang_c = (pos % GRID_W).astype(F32)[:, None] * inv_freq[None, :]
    zero = jnp.zeros_like(ang_r)
    cos = jnp.concatenate([jnp.cos(ang_r)] * 2 + [jnp.cos(ang_c)] * 2, axis=-1)
    sina = jnp.concatenate([-jnp.sin(ang_r), zero, -jnp.sin(ang_c), zero], axis=-1)
    sinb = jnp.concatenate([zero, jnp.sin(ang_r), zero, jnp.sin(ang_c)], axis=-1)
    return cos, sina, sinb


def _layer(h, p, g_mix, w_in, q_norm, k_norm, conv_w, conv_b, lru_wa, lru_ba, lru_wi, lru_bi, lru_lam,
           w_attn_br, w_lru_br, w_out, g_moe, w_router, b_router, w_gu, b_gu, w_dn, b_dn,
           g_ple, w_ple_gate, w_ple_proj, rope):
    B, S, D = h.shape
    T = B * S
    row = lambda a: a.reshape(1, -1)
    q, k, v, xr, xg, ga, gr = _in_proj(
        h, row(g_mix), w_in.astype(BF16), row(q_norm) * (HEAD_DIM ** -0.5 * np.log2(np.e)), row(k_norm), *rope)
    attn = _attention(q, k, v)
    xr2 = xr.transpose(1, 0, 2).reshape(S * B, LRU_WIDTH)
    xg2 = xg.transpose(1, 0, 2).reshape(S * B, LRU_WIDTH)
    wcat = jnp.concatenate([lru_wa, lru_wi], axis=-1).astype(BF16)
    bcat = jnp.concatenate([lru_ba.reshape(2, LRU_BLOCKS, 1, LRU_BLOCK_DIM),
                            lru_bi.reshape(2, LRU_BLOCKS, 1, LRU_BLOCK_DIM)], axis=-1)
    h_fwd = _lru_scan(xr2, conv_w, row(conv_b), wcat[0], bcat[0], row(lru_lam[0]), batch=B, reverse=False)
    ylru = _lru_scan(xr2, conv_w, row(conv_b), wcat[1], bcat[1], row(lru_lam[1]), batch=B, reverse=True,
                     h_fwd=h_fwd, xg2=xg2).reshape(S, B, LRU_WIDTH).transpose(1, 0, 2)
    w_router_f = jnp.zeros((D, V7X_LANES), F32).at[:, :N_EXPERTS].set(w_router)
    w_router_hi = w_router_f.astype(BF16)
    w_router_p = jnp.stack([w_router_hi, (w_router_f - w_router_hi.astype(F32)).astype(BF16)])
    b_router_p = jnp.zeros((1, V7X_LANES), F32).at[0, :N_EXPERTS].set(b_router)
    h1, u2, idx, gate, cnt = _merge(h, attn, ylru, ga, gr, w_attn_br.astype(BF16), w_lru_br.astype(BF16),
                                    w_out.astype(BF16), row(g_moe), w_router_p, b_router_p)
    cap = T * TOP_K + N_EXPERTS * EXPERT_TILE
    counts = cnt[0, :N_EXPERTS].astype(jnp.int32)
    padded = ((counts + EXPERT_TILE - 1) // EXPERT_TILE) * EXPERT_TILE
    pad_end = jnp.cumsum(padded)
    pad_start = pad_end - padded
    blk_start = jnp.arange(cap // EXPERT_TILE, dtype=jnp.int32) * EXPERT_TILE
    blk_exp = jnp.minimum(jnp.sum(blk_start[:, None] >= pad_end[None, :], axis=1), N_EXPERTS - 1)
    n_used = (pad_end[-1:] // EXPERT_TILE).astype(jnp.int32)
    ps_row = jnp.zeros((1, V7X_LANES), F32).at[0, :N_EXPERTS].set(pad_start.astype(F32))
    dest = _rank(idx.reshape(T, TOP_K), ps_row).reshape(T * TOP_K)
    assert T & (T - 1) == 0, "token count must be a power of two for the slot encoding"
    inv3 = _invert(pad_end.astype(jnp.int32), counts, dest, cap, T).reshape(cap // EXPERT_TILE, 1, EXPERT_TILE)
    y4 = _experts(blk_exp.astype(jnp.int32), n_used, inv3, u2.reshape(T * ROW_TILE, V7X_LANES), w_gu,
                  b_gu.reshape(N_EXPERTS, 1, -1), w_dn, b_dn.reshape(N_EXPERTS, 1, -1))
    out = _combine(y4, gate.reshape(T, TOP_K), h1.reshape(T, D),
                   p.reshape(T, PLE_DIM), row(g_ple), w_ple_gate.astype(BF16), w_ple_proj.astype(BF16))
    return out.reshape(B, S, D)


def kernel(x, p, g_mix, w_in, q_norm, k_norm, conv_w, conv_b, lru_wa, lru_ba, lru_wi, lru_bi, lru_lam,
           w_attn_br, w_lru_br, w_out, g_moe, w_router, b_router, w_gu, b_gu, w_dn, b_dn,
           g_ple, w_ple_gate, w_ple_proj):
    rope = _rope_tables(x.shape[1])
    h = x
    for l in range(p.shape[0]):
        h = _layer(h, p[l], g_mix[l], w_in[l], q_norm[l], k_norm[l], conv_w[l], conv_b[l], lru_wa[l],
                   lru_ba[l], lru_wi[l], lru_bi[l], lru_lam[l], w_attn_br[l], w_lru_br[l], w_out[l],
                   g_moe[l], w_router[l], b_router[l], w_gu[l], b_gu[l], w_dn[l], b_dn[l],
                   g_ple[l], w_ple_gate[l], w_ple_proj[l], rope)
    return h
```

```python
import functools

import jax
import jax.numpy as jnp
import numpy as np
from jax import lax
from jax.experimental import pallas as pl
from jax.experimental.pallas import tpu as pltpu

D_MODEL = 1024
N_Q_HEADS = 8
N_KV_HEADS = 2
HEAD_DIM = 128
Q_GROUP = N_Q_HEADS // N_KV_HEADS
ROPE_THETA = 10000.0
GRID_W = 64
LRU_WIDTH = 1024
LRU_BLOCKS = 8
LRU_BLOCK_DIM = LRU_WIDTH // LRU_BLOCKS
LRU_C = 8.0
N_EXPERTS = 32
TOP_K = 4
D_EXPERT = 1024
SWIGLU_LIMIT = 7.0
SWIGLU_ALPHA = 1.702
PLE_DIM = 256
EPS = 1e-6
Q_W = N_Q_HEADS * HEAD_DIM
KV_W = N_KV_HEADS * HEAD_DIM
GROUP_W = Q_GROUP * HEAD_DIM
IN_WIDTH = Q_W + 2 * KV_W + 2 * LRU_WIDTH + 2 * D_MODEL
OFF_Q = 0
OFF_K = OFF_Q + Q_W
OFF_V = OFF_K + KV_W
OFF_XR = OFF_V + KV_W
OFF_XG = OFF_XR + LRU_WIDTH
OFF_GA = OFF_XG + LRU_WIDTH
OFF_GR = OFF_GA + D_MODEL

V7X_LANES = 128
V7X_VMEM_LIMIT_BYTES = 56 * 1024 * 1024
TOKEN_TILE = 512
ATTN_Q_TILE = 1024
LRU_STEPS = 32
EXPERT_TILE = 512
RANK_TILE = 512
INVERT_CHUNK = 8192
SPARE_ROWS = 1024

F32 = jnp.float32
BF16 = jnp.bfloat16


def _params(*sem):
    return pltpu.CompilerParams(dimension_semantics=sem, vmem_limit_bytes=V7X_VMEM_LIMIT_BYTES)


def _bdot(a, b):
    return jnp.dot(a, b, preferred_element_type=F32)


def _rms(x, g):
    return x * lax.rsqrt(jnp.mean(x * x, axis=-1, keepdims=True) + EPS) * g


ROW_TILE = 8


def _store_row_tiles(ref, x):
    n = x.shape[0]
    for j in range(ROW_TILE):
        ref[pl.ds(j, n, stride=ROW_TILE), :] = x[:, j * V7X_LANES:(j + 1) * V7X_LANES]


def _load_row_tiles(ref):
    n = ref.shape[0] // ROW_TILE
    return jnp.concatenate([ref[pl.ds(j, n, stride=ROW_TILE), :] for j in range(ROW_TILE)], axis=-1)


def _sigmoid(x):
    return 0.5 * jnp.tanh(0.5 * x) + 0.5


def _in_proj_kernel(x_ref, g_ref, w_ref, qn_ref, kn_ref, cos_ref, sina_ref, sinb_ref,
                    q_ref, k_ref, v_ref, xr_ref, xg_ref, ga_ref, gr_ref):
    u = _rms(x_ref[...], g_ref[...]).astype(BF16)
    cos, sina, sinb = cos_ref[...], sina_ref[...], sinb_ref[...]

    def head(z, gain):
        y = _rms(z, gain)
        return y * cos + pltpu.roll(y, 96, 1) * sina + pltpu.roll(y, 32, 1) * sinb

    zq = _bdot(u, w_ref[:, OFF_Q:OFF_Q + Q_W])
    for h in range(N_Q_HEADS):
        sl = slice(h * HEAD_DIM, (h + 1) * HEAD_DIM)
        q_ref[:, sl] = head(zq[:, sl], qn_ref[...]).astype(BF16)
    zk = _bdot(u, w_ref[:, OFF_K:OFF_K + KV_W])
    for h in range(N_KV_HEADS):
        sl = slice(h * HEAD_DIM, (h + 1) * HEAD_DIM)
        k_ref[:, sl] = head(zk[:, sl], kn_ref[...]).astype(BF16)
    v_ref[...] = _bdot(u, w_ref[:, OFF_V:OFF_V + KV_W]).astype(BF16)
    xr_ref[...] = _bdot(u, w_ref[:, OFF_XR:OFF_XR + LRU_WIDTH])
    xg_ref[...] = _bdot(u, w_ref[:, OFF_XG:OFF_XG + LRU_WIDTH])
    ga_ref[...] = _bdot(u, w_ref[:, OFF_GA:OFF_GA + D_MODEL])
    gr_ref[...] = _bdot(u, w_ref[:, OFF_GR:OFF_GR + D_MODEL])


def _in_proj(x, g_mix, w_in, qn, kn, cos, sina, sinb):
    B, S, D = x.shape
    tm = TOKEN_TILE
    bm = lambda w: pl.BlockSpec((None, tm, w), lambda b, i: (b, i, 0))
    full = lambda a: pl.BlockSpec(a.shape, lambda b, i: (0,) * a.ndim)
    tab = pl.BlockSpec((tm, HEAD_DIM), lambda b, i: (i, 0))
    return pl.pallas_call(
        _in_proj_kernel,
        grid=(B, S // tm),
        in_specs=[bm(D), full(g_mix), full(w_in), full(qn), full(kn), tab, tab, tab],
        out_specs=[bm(Q_W), bm(KV_W), bm(KV_W), bm(LRU_WIDTH), bm(LRU_WIDTH), bm(D), bm(D)],
        out_shape=[
            jax.ShapeDtypeStruct((B, S, Q_W), BF16),
            jax.ShapeDtypeStruct((B, S, KV_W), BF16),
            jax.ShapeDtypeStruct((B, S, KV_W), BF16),
            jax.ShapeDtypeStruct((B, S, LRU_WIDTH), F32),
            jax.ShapeDtypeStruct((B, S, LRU_WIDTH), F32),
            jax.ShapeDtypeStruct((B, S, D), F32),
            jax.ShapeDtypeStruct((B, S, D), F32),
        ],
        compiler_params=_params("arbitrary", "arbitrary"),
        name="in_proj",
    )(x, g_mix, w_in, qn, kn, cos, sina, sinb)


def _attn_kernel(q_ref, k_ref, v_ref, o_ref):
    k = k_ref[...]
    v1 = jnp.concatenate([v_ref[...], jnp.ones(v_ref.shape, v_ref.dtype)], axis=-1)
    for h in range(Q_GROUP):
        sl = slice(h * HEAD_DIM, (h + 1) * HEAD_DIM)
        s = lax.dot_general(q_ref[:, sl], k, (((1,), (1,)), ((), ())), preferred_element_type=F32)
        p = jnp.exp2(s - jnp.max(s, axis=-1, keepdims=True))
        o = _bdot(p.astype(BF16), v1)
        o_ref[:, sl] = (o[:, :HEAD_DIM] * (1.0 / o[:, HEAD_DIM:HEAD_DIM + 1])).astype(BF16)


def _attention(q, k, v):
    B, S, _ = q.shape
    tq = ATTN_Q_TILE
    return pl.pallas_call(
        _attn_kernel,
        grid=(B, N_KV_HEADS, S // tq),
        in_specs=[
            pl.BlockSpec((None, tq, GROUP_W), lambda b, g, i: (b, i, g)),
            pl.BlockSpec((None, S, HEAD_DIM), lambda b, g, i: (b, 0, g)),
            pl.BlockSpec((None, S, HEAD_DIM), lambda b, g, i: (b, 0, g)),
        ],
        out_specs=pl.BlockSpec((None, tq, GROUP_W), lambda b, g, i: (b, i, g)),
        out_shape=jax.ShapeDtypeStruct((B, S, Q_W), BF16),
        compiler_params=_params("arbitrary", "arbitrary", "arbitrary"),
        name="attention",
    )(q, k, v)


def _gelu_tanh(x):
    return 0.5 * x * (1.0 + jnp.tanh(np.sqrt(2.0 / np.pi) * (x + 0.044715 * (x * x * x))))


def _lru_kernel(*refs, batch, reverse):
    if reverse:
        (xc_ref, xp_ref, xn_ref, cw_ref, cb_ref, w_ref, b_ref, lam_ref, hf_ref, xg_ref,
         o_ref, ext_ref, a_ref, bt_ref, h_ref) = refs
    else:
        (xc_ref, xp_ref, xn_ref, cw_ref, cb_ref, w_ref, b_ref, lam_ref,
         o_ref, ext_ref, a_ref, bt_ref, h_ref) = refs
    step = pl.program_id(0)
    nsteps = pl.num_programs(0)
    chunk = (nsteps - 1 - step) if reverse else step
    rows = LRU_STEPS * batch

    @pl.when(step == 0)
    def _():
        h_ref[...] = jnp.zeros_like(h_ref)

    ext_ref[0:2 * batch, :] = jnp.where(chunk > 0, xp_ref[...], 0.0)
    ext_ref[2 * batch:2 * batch + rows, :] = xc_ref[...]
    ext_ref[2 * batch + rows:3 * batch + rows, :] = jnp.where(chunk < nsteps - 1, xn_ref[...], 0.0)
    c = cb_ref[...] + sum(cw_ref[j:j + 1, :] * ext_ref[j * batch:j * batch + rows, :] for j in range(4))
    cb16 = c.astype(BF16)
    lam = lam_ref[...]
    half_scale = (-0.5 * LRU_C) * (jnp.maximum(-lam, 0.0) + jnp.log1p(jnp.exp(-jnp.abs(lam))))
    for n in range(LRU_BLOCKS):
        sl = slice(n * LRU_BLOCK_DIM, (n + 1) * LRU_BLOCK_DIM)
        th = jnp.tanh(_bdot(cb16[:, sl], w_ref[n]) + b_ref[n])
        ig = 0.5 * th[:, LRU_BLOCK_DIM:] + 0.5
        log_a = half_scale[:, sl] * th[:, :LRU_BLOCK_DIM] + half_scale[:, sl]
        a = jnp.exp(log_a)
        a_ref[:, sl] = a
        one_minus_a2 = -jnp.tanh(log_a) * (a * a + 1.0)
        root = jnp.where(one_minus_a2 > 0.0, one_minus_a2 * lax.rsqrt(one_minus_a2), 0.0)
        bt_ref[:, sl] = root * ig * c[:, sl]

    def scan_step(t, h):
        tt = (LRU_STEPS - 1 - t) if reverse else t
        rs = pl.ds(pl.multiple_of(tt * batch, batch), batch)
        h = a_ref[rs, :] * h + bt_ref[rs, :]
        if reverse:
            o_ref[rs, :] = ((h + hf_ref[rs, :]) * _gelu_tanh(xg_ref[rs, :])).astype(o_ref.dtype)
        else:
            o_ref[rs, :] = h
        return h

    h_ref[...] = lax.fori_loop(0, LRU_STEPS, scan_step, h_ref[...], unroll=4)


def _lru_scan(xr2, conv_w, conv_b, wcat, bcat, lam, *, batch, reverse, h_fwd=None, xg2=None):
    n_rows, width = xr2.shape
    rows = LRU_STEPS * batch
    n = n_rows // rows
    ch = (lambda i: n - 1 - i) if reverse else (lambda i: i)
    tile = pl.BlockSpec((rows, width), lambda i: (ch(i), 0))
    prev = pl.BlockSpec((2 * batch, width), lambda i: (jnp.maximum(ch(i) * (LRU_STEPS // 2) - 1, 0), 0))
    nxt = pl.BlockSpec((batch, width),
                       lambda i: (jnp.minimum((ch(i) + 1) * LRU_STEPS, n * LRU_STEPS - 1), 0))
    full = lambda a: pl.BlockSpec(a.shape, lambda i: (0,) * a.ndim)
    ins = [xr2, xr2, xr2, conv_w, conv_b, wcat, bcat, lam]
    specs = [tile, prev, nxt, full(conv_w), full(conv_b), full(wcat), full(bcat), full(lam)]
    if reverse:
        ins += [h_fwd, xg2]
        specs += [tile, tile]
    return pl.pallas_call(
        functools.partial(_lru_kernel, batch=batch, reverse=reverse),
        grid=(n,),
        in_specs=specs,
        out_specs=tile,
        out_shape=jax.ShapeDtypeStruct((n_rows, width), BF16 if reverse else F32),
        scratch_shapes=[
            pltpu.VMEM((rows + 3 * batch, width), F32),
            pltpu.VMEM((rows, width), F32),
            pltpu.VMEM((rows, width), F32),
            pltpu.VMEM((batch, width), F32),
        ],
        compiler_params=_params("arbitrary"),
        name="lru_bwd" if reverse else "lru_fwd",
    )(*ins)


def _merge_kernel(x_ref, at_ref, yl_ref, ga_ref, gr_ref, wa_ref, wl_ref, wo_ref, gm_ref, wr_ref, br_ref,
                  h_ref, u_ref, idx_ref, gate_ref, cnt_ref):
    y_attn = _bdot(at_ref[...], wa_ref[...])
    y_lru = _bdot(yl_ref[...], wl_ref[...])
    merged = _sigmoid(ga_ref[...]) * y_attn + _sigmoid(gr_ref[...]) * y_lru
    h = x_ref[...] + _bdot(merged.astype(BF16), wo_ref[...])
    h_ref[...] = h
    u = _rms(h, gm_ref[...])
    _store_row_tiles(u_ref, u)
    u_hi = u.astype(BF16)
    u_lo = (u - u_hi.astype(F32)).astype(BF16)
    logits = _bdot(u_hi, wr_ref[0]) + (_bdot(u_lo, wr_ref[0]) + _bdot(u_hi, wr_ref[1]))
    lane = lax.broadcasted_iota(jnp.int32, logits.shape, 1)
    work = jnp.where(lane < N_EXPERTS, logits + br_ref[...], -jnp.inf)
    vals, hits = [], jnp.zeros(logits.shape, F32)
    for kk in range(TOP_K):
        m = jnp.max(work, axis=-1, keepdims=True)
        idx = jnp.min(jnp.where(work == m, lane, V7X_LANES), axis=-1, keepdims=True)
        pick = lane == idx
        work = jnp.where(pick, -jnp.inf, work)
        hits = hits + pick.astype(F32)
        vals.append(m)
        idx_ref[:, kk:kk + 1] = idx
    es = [jnp.exp(vv - vals[0]) for vv in vals]
    inv = 1.0 / sum(es)
    for kk in range(TOP_K):
        gate_ref[:, kk:kk + 1] = es[kk] * inv

    @pl.when((pl.program_id(0) == 0) & (pl.program_id(1) == 0))
    def _():
        cnt_ref[...] = jnp.zeros_like(cnt_ref)

    cnt_ref[0:1, :] += jnp.sum(hits, axis=0, keepdims=True)


def _merge(x, attn, ylru, ga, gr, wa, wl, wo, g_moe, wr, br):
    B, S, D = x.shape
    tm = TOKEN_TILE
    bm = lambda w: pl.BlockSpec((None, tm, w), lambda b, i: (b, i, 0))
    full = lambda a: pl.BlockSpec(a.shape, lambda b, i: (0,) * a.ndim)
    return pl.pallas_call(
        _merge_kernel,
        grid=(B, S // tm),
        in_specs=[bm(D), bm(Q_W), bm(LRU_WIDTH), bm(D), bm(D), full(wa), full(wl), full(wo),
                  full(g_moe), full(wr), full(br)],
        out_specs=[bm(D), pl.BlockSpec((None, tm * ROW_TILE, V7X_LANES), lambda b, i: (b, i, 0)),
                   bm(TOP_K), bm(TOP_K), pl.BlockSpec((8, V7X_LANES), lambda b, i: (0, 0))],
        out_shape=[
            jax.ShapeDtypeStruct((B, S, D), F32),
            jax.ShapeDtypeStruct((B, S * ROW_TILE, V7X_LANES), F32),
            jax.ShapeDtypeStruct((B, S, TOP_K), jnp.int32),
            jax.ShapeDtypeStruct((B, S, TOP_K), F32),
            jax.ShapeDtypeStruct((8, V7X_LANES), F32),
        ],
        compiler_params=_params("arbitrary", "arbitrary"),
        name="merge_router",
    )(x, attn, ylru, ga, gr, wa, wl, wo, g_moe, wr, br)


def _rank_kernel(idx_ref, ps_ref, dest_ref, carry_ref):
    @pl.when(pl.program_id(0) == 0)
    def _():
        carry_ref[...] = jnp.zeros_like(carry_ref)

    tm = idx_ref.shape[0]
    lane = lax.broadcasted_iota(jnp.int32, (tm, V7X_LANES), 1)
    picks = [lane == idx_ref[:, kk:kk + 1] for kk in range(TOP_K)]
    hits = sum(p.astype(F32) for p in picks)
    earlier = (lax.broadcasted_iota(jnp.int32, (tm, tm), 0) >
               lax.broadcasted_iota(jnp.int32, (tm, tm), 1)).astype(BF16)
    base = _bdot(earlier, hits.astype(BF16)) + carry_ref[...] + ps_ref[...]
    for kk in range(TOP_K):
        dest_ref[:, kk:kk + 1] = jnp.sum(jnp.where(picks[kk], base, 0.0), axis=-1,
                                         keepdims=True).astype(jnp.int32)
    carry_ref[...] += jnp.sum(hits, axis=0, keepdims=True)


def _rank(idx, pad_start_row):
    T = idx.shape[0]
    tm = RANK_TILE
    return pl.pallas_call(
        _rank_kernel,
        grid=(T // tm,),
        in_specs=[pl.BlockSpec((tm, TOP_K), lambda i: (i, 0)),
                  pl.BlockSpec((1, V7X_LANES), lambda i: (0, 0))],
        out_specs=pl.BlockSpec((tm, TOP_K), lambda i: (i, 0)),
        out_shape=jax.ShapeDtypeStruct((T, TOP_K), jnp.int32),
        scratch_shapes=[pltpu.VMEM((1, V7X_LANES), F32)],
        compiler_params=_params("arbitrary"),
        name="slot_rank",
    )(idx, pad_start_row)


def _invert_kernel(pad_end_ref, count_ref, dest_vmem, out_vmem, dest_ref, inv_ref, sem, *, n_tokens, chunk):
    step = pl.program_id(0)
    stage_in = pltpu.make_async_copy(dest_vmem, dest_ref, sem)
    stage_in.start()
    stage_in.wait()

    @pl.when(step == 0)
    def _():
        def fill(s, carry):
            inv_ref[s] = TOP_K * n_tokens + (s & (SPARE_ROWS - 1))
            return carry

        for e in range(N_EXPERTS):
            lo = pad_end_ref[e] - ((count_ref[e] + EXPERT_TILE - 1) & -EXPERT_TILE) + count_ref[e]
            lax.fori_loop(lo, pad_end_ref[e], fill, 0)
        lax.fori_loop(pad_end_ref[N_EXPERTS - 1], inv_ref.shape[0], fill, 0)

    group = 4 * TOP_K

    def body(g, carry):
        tok0 = step * (chunk // TOP_K) + g * (group // TOP_K)
        for u in range(group):
            inv_ref[dest_ref[g * group + u]] = tok0 + ((u % TOP_K) * n_tokens + u // TOP_K)
        return carry

    lax.fori_loop(0, chunk // group, body, 0)

    @pl.when(step == pl.num_programs(0) - 1)
    def _():
        stage_out = pltpu.make_async_copy(inv_ref, out_vmem, sem)
        stage_out.start()
        stage_out.wait()


def _invert(pad_end, counts, dest_flat, cap, n_tokens):
    chunk = INVERT_CHUNK
    return pl.pallas_call(
        functools.partial(_invert_kernel, n_tokens=n_tokens, chunk=chunk),
        grid_spec=pltpu.PrefetchScalarGridSpec(
            num_scalar_prefetch=2,
            grid=(dest_flat.shape[0] // chunk,),
            in_specs=[pl.BlockSpec((chunk,), lambda i, pe, ct: (i,))],
            out_specs=pl.BlockSpec((cap,), lambda i, pe, ct: (0,)),
            scratch_shapes=[pltpu.SMEM((chunk,), jnp.int32), pltpu.SMEM((cap,), jnp.int32),
                            pltpu.SemaphoreType.DMA(())],
        ),
        out_shape=jax.ShapeDtypeStruct((cap,), jnp.int32),
        compiler_params=_params("arbitrary"),
        name="slot_invert",
    )(pad_end, counts, dest_flat)


def _expert_kernel(be_ref, nu_ref, invc_ref, invn_ref, u_hbm, wgu_ref, bgu_ref, wdn_ref, bdn_ref,
                   y_hbm, xbuf, ybuf, gsem, ssem, *, n_tokens):
    j = pl.program_id(0)
    n_used = nu_ref[0]
    parity = j & 1

    def tile_rows(row):
        if isinstance(row, int):
            return pl.ds(row * ROW_TILE, ROW_TILE)
        return pl.ds(pl.multiple_of(row * ROW_TILE, ROW_TILE), ROW_TILE)

    def gather(inv_ref, r, buf):
        tok = inv_ref[0, 0, r] & (n_tokens - 1)
        return pltpu.make_async_copy(u_hbm.at[tile_rows(tok)], xbuf.at[buf, tile_rows(r)], gsem.at[buf])

    def scatter(inv_ref, r, buf):
        return pltpu.make_async_copy(ybuf.at[buf, tile_rows(r)], y_hbm.at[tile_rows(inv_ref[0, 0, r])],
                                     ssem.at[buf])

    def wait_gather(buf):
        pltpu.make_async_copy(xbuf.at[buf], xbuf.at[buf], gsem.at[buf]).wait()

    def wait_scatter(buf):
        pltpu.make_async_copy(ybuf.at[buf], ybuf.at[buf], ssem.at[buf]).wait()

    @pl.when(j == 0)
    def _():
        def first(r, carry):
            gather(invc_ref, r, 0).start()
            return carry
        lax.fori_loop(0, EXPERT_TILE, first, 0)
        xbuf[1] = jnp.zeros(xbuf.shape[1:], xbuf.dtype)
        clears = [pltpu.make_async_copy(
            xbuf.at[1], y_hbm.at[pl.ds((TOP_K * n_tokens + c * EXPERT_TILE) * ROW_TILE, EXPERT_TILE * ROW_TILE)],
            ssem.at[1]) for c in range(SPARE_ROWS // EXPERT_TILE)]
        for c in clears:
            c.start()
        for c in clears:
            c.wait()

    def block(cur):
        oth = 1 - cur
        wait_gather(cur)

        @pl.when(j > 1)
        def _():
            wait_scatter(cur)

        for r in range(EXPERT_TILE):
            gather(invn_ref, r, oth).start(priority=r % 2)
        hu = _bdot(_load_row_tiles(xbuf.at[cur]).astype(BF16), wgu_ref[...].astype(BF16)) + bgu_ref[...]
        gate = jnp.minimum(hu[:, :D_EXPERT], SWIGLU_LIMIT)
        up = jnp.clip(hu[:, D_EXPERT:], -SWIGLU_LIMIT, SWIGLU_LIMIT)
        act = gate * _sigmoid(SWIGLU_ALPHA * gate)
        _store_row_tiles(ybuf.at[cur],
                         _bdot(((up + 1.0) * act).astype(BF16), wdn_ref[...].astype(BF16)) + bdn_ref[...])
        for r in range(EXPERT_TILE):
            scatter(invc_ref, r, cur).start(priority=r % 2)

        @pl.when(j == n_used - 1)
        def _():
            wait_gather(oth)
            wait_scatter(cur)

            @pl.when(j > 0)
            def _():
                wait_scatter(oth)

    for cur in range(2):
        @pl.when((j < n_used) & (parity == cur))
        def _():
            block(cur)


def _experts(blk_exp, n_used, inv3, u, w_gu, b_gu, w_dn, b_dn):
    T, D = u.shape[0] // ROW_TILE, D_MODEL
    n_blocks = inv3.shape[0]
    exp3 = lambda j, be, nu: (be[j], 0, 0)
    inv_spec = lambda f: pl.BlockSpec((1, 1, EXPERT_TILE), lambda j, be, nu: (f(j, nu), 0, 0),
                                      memory_space=pltpu.SMEM)
    last = lambda j, nu: jnp.minimum(j, nu[0] - 1)
    return pl.pallas_call(
        functools.partial(_expert_kernel, n_tokens=T),
        grid_spec=pltpu.PrefetchScalarGridSpec(
            num_scalar_prefetch=2,
            grid=(n_blocks,),
            in_specs=[inv_spec(last),
                      inv_spec(lambda j, nu: jnp.minimum(j + 1, nu[0] - 1)),
                      pl.BlockSpec(memory_space=pl.ANY),
                      pl.BlockSpec((None, D, 2 * D_EXPERT), exp3),
                      pl.BlockSpec((None, 1, 2 * D_EXPERT), exp3),
                      pl.BlockSpec((None, D_EXPERT, D), exp3),
                      pl.BlockSpec((None, 1, D), exp3)],
            out_specs=pl.BlockSpec(memory_space=pl.ANY),
            scratch_shapes=[pltpu.VMEM((2, EXPERT_TILE * ROW_TILE, V7X_LANES), F32),
                            pltpu.VMEM((2, EXPERT_TILE * ROW_TILE, V7X_LANES), F32),
                            pltpu.SemaphoreType.DMA((2,)), pltpu.SemaphoreType.DMA((2,))],
        ),
        out_shape=jax.ShapeDtypeStruct(((TOP_K * T + SPARE_ROWS) * ROW_TILE, V7X_LANES), F32),
        compiler_params=_params("arbitrary"),
        name="experts",
    )(blk_exp, n_used, inv3, inv3, u, w_gu, b_gu, w_dn, b_dn)


def _combine_kernel(y0_ref, y1_ref, y2_ref, y3_ref, gate_ref, h_ref, p_ref, gp_ref, wg_ref, wp_ref, o_ref):
    h = h_ref[...]
    for kk, y_ref in enumerate((y0_ref, y1_ref, y2_ref, y3_ref)):
        h = h + gate_ref[:, kk:kk + 1] * _load_row_tiles(y_ref)
    u = _rms(h, gp_ref[...]).astype(BF16)
    o_ref[...] = h + _sigmoid(_bdot(u, wg_ref[...])) * _bdot(p_ref[...].astype(BF16), wp_ref[...])


def _combine(y4, gate, h, p, g_ple, wg, wp):
    T, D = h.shape
    tm = TOKEN_TILE
    full = lambda a: pl.BlockSpec(a.shape, lambda i: (0,) * a.ndim)
    plane = lambda kk: pl.BlockSpec((tm * ROW_TILE, V7X_LANES), lambda i: (kk * (T // tm) + i, 0))
    return pl.pallas_call(
        _combine_kernel,
        grid=(T // tm,),
        in_specs=[plane(0), plane(1), plane(2), plane(3),
                  pl.BlockSpec((tm, TOP_K), lambda i: (i, 0)),
                  pl.BlockSpec((tm, D), lambda i: (i, 0)),
                  pl.BlockSpec((tm, PLE_DIM), lambda i: (i, 0)),
                  full(g_ple), full(wg), full(wp)],
        out_specs=pl.BlockSpec((tm, D), lambda i: (i, 0)),
        out_shape=jax.ShapeDtypeStruct((T, D), F32),
        compiler_params=_params("arbitrary"),
        name="combine_ple",
    )(y4, y4, y4, y4, gate, h, p, g_ple, wg, wp)


def _rope_tables(seq_len):
    pos = jnp.arange(seq_len)
    axis_dim = HEAD_DIM // 2
    inv_freq = ROPE_THETA ** (-jnp.arange(0, axis_dim, 2, dtype=F32) / axis_dim)
    ang_r = (pos // GRID_W).astype(F32)[:, None] * inv_freq[None, :]
    ang_c = (pos % GRID_W).astype(F32)[:, None] * inv_freq[None, :]
    zero = jnp.zeros_like(ang_r)
    cos = jnp.concatenate([jnp.cos(ang_r)] * 2 + [jnp.cos(ang_c)] * 2, axis=-1)
    sina = jnp.concatenate([-jnp.sin(ang_r), zero, -jnp.sin(ang_c), zero], axis=-1)
    sinb = jnp.concatenate([zero, jnp.sin(ang_r), zero, jnp.sin(ang_c)], axis=-1)
    return cos, sina, sinb


def _layer(h, p, g_mix, w_in, q_norm, k_norm, conv_w, conv_b, lru_wa, lru_ba, lru_wi, lru_bi, lru_lam,
           w_attn_br, w_lru_br, w_out, g_moe, w_router, b_router, w_gu, b_gu, w_dn, b_dn,
           g_ple, w_ple_gate, w_ple_proj, rope):
    B, S, D = h.shape
    T = B * S
    row = lambda a: a.reshape(1, -1)
    q, k, v, xr, xg, ga, gr = _in_proj(
        h, row(g_mix), w_in.astype(BF16), row(q_norm) * (HEAD_DIM ** -0.5 * np.log2(np.e)), row(k_norm), *rope)
    attn = _attention(q, k, v)
    xr2 = xr.transpose(1, 0, 2).reshape(S * B, LRU_WIDTH)
    xg2 = xg.transpose(1, 0, 2).reshape(S * B, LRU_WIDTH)
    wcat = (0.5 * jnp.concatenate([lru_wa, lru_wi], axis=-1)).astype(BF16)
    bcat = 0.5 * jnp.concatenate([lru_ba.reshape(2, LRU_BLOCKS, 1, LRU_BLOCK_DIM),
                                  lru_bi.reshape(2, LRU_BLOCKS, 1, LRU_BLOCK_DIM)], axis=-1)
    h_fwd = _lru_scan(xr2, conv_w, row(conv_b), wcat[0], bcat[0], row(lru_lam[0]), batch=B, reverse=False)
    ylru = _lru_scan(xr2, conv_w, row(conv_b), wcat[1], bcat[1], row(lru_lam[1]), batch=B, reverse=True,
                     h_fwd=h_fwd, xg2=xg2).reshape(S, B, LRU_WIDTH).transpose(1, 0, 2)
    w_router_f = jnp.zeros((D, V7X_LANES), F32).at[:, :N_EXPERTS].set(w_router)
    w_router_hi = w_router_f.astype(BF16)
    w_router_p = jnp.stack([w_router_hi, (w_router_f - w_router_hi.astype(F32)).astype(BF16)])
    b_router_p = jnp.zeros((1, V7X_LANES), F32).at[0, :N_EXPERTS].set(b_router)
    h1, u2, idx, gate, cnt = _merge(h, attn, ylru, ga, gr, w_attn_br.astype(BF16), w_lru_br.astype(BF16),
                                    w_out.astype(BF16), row(g_moe), w_router_p, b_router_p)
    cap = T * TOP_K + N_EXPERTS * EXPERT_TILE
    counts = cnt[0, :N_EXPERTS].astype(jnp.int32)
    padded = ((counts + EXPERT_TILE - 1) // EXPERT_TILE) * EXPERT_TILE
    pad_end = jnp.cumsum(padded)
    pad_start = pad_end - padded
    blk_start = jnp.arange(cap // EXPERT_TILE, dtype=jnp.int32) * EXPERT_TILE
    blk_exp = jnp.minimum(jnp.sum(blk_start[:, None] >= pad_end[None, :], axis=1), N_EXPERTS - 1)
    n_used = (pad_end[-1:] // EXPERT_TILE).astype(jnp.int32)
    ps_row = jnp.zeros((1, V7X_LANES), F32).at[0, :N_EXPERTS].set(pad_start.astype(F32))
    dest = _rank(idx.reshape(T, TOP_K), ps_row).reshape(T * TOP_K)
    assert T & (T - 1) == 0, "token count must be a power of two for the slot encoding"
    inv3 = _invert(pad_end.astype(jnp.int32), counts, dest, cap, T).reshape(cap // EXPERT_TILE, 1, EXPERT_TILE)
    y4 = _experts(blk_exp.astype(jnp.int32), n_used, inv3, u2.reshape(T * ROW_TILE, V7X_LANES), w_gu,
                  b_gu.reshape(N_EXPERTS, 1, -1), w_dn, b_dn.reshape(N_EXPERTS, 1, -1))
    out = _combine(y4, gate.reshape(T, TOP_K), h1.reshape(T, D),
                   p.reshape(T, PLE_DIM), row(g_ple), w_ple_gate.astype(BF16), w_ple_proj.astype(BF16))
    return out.reshape(B, S, D)


def kernel(x, p, g_mix, w_in, q_norm, k_norm, conv_w, conv_b, lru_wa, lru_ba, lru_wi, lru_bi, lru_lam,
           w_attn_br, w_lru_br, w_out, g_moe, w_router, b_router, w_gu, b_gu, w_dn, b_dn,
           g_ple, w_ple_gate, w_ple_proj):
    rope = _rope_tables(x.shape[1])
    h = x
    for l in range(p.shape[0]):
        h = _layer(h, p[l], g_mix[l], w_in[l], q_norm[l], k_norm[l], conv_w[l], conv_b[l], lru_wa[l],
                   lru_ba[l], lru_wi[l], lru_bi[l], lru_lam[l], w_attn_br[l], w_lru_br[l], w_out[l],
                   g_moe[l], w_router[l], b_router[l], w_gu[l], b_gu[l], w_dn[l], b_dn[l],
                   g_ple[l], w_ple_gate[l], w_ple_proj[l], rope)
    return h
```

```python
import functools

import jax
import jax.numpy as jnp
import numpy as np
from jax import lax
from jax.experimental import pallas as pl
from jax.experimental.pallas import tpu as pltpu

D_MODEL = 1024
N_Q_HEADS = 8
N_KV_HEADS = 2
HEAD_DIM = 128
Q_GROUP = N_Q_HEADS // N_KV_HEADS
ROPE_THETA = 10000.0
GRID_W = 64
LRU_WIDTH = 1024
LRU_BLOCKS = 8
LRU_BLOCK_DIM = LRU_WIDTH // LRU_BLOCKS
LRU_C = 8.0
N_EXPERTS = 32
TOP_K = 4
D_EXPERT = 1024
SWIGLU_LIMIT = 7.0
SWIGLU_ALPHA = 1.702
PLE_DIM = 256
EPS = 1e-6
Q_W = N_Q_HEADS * HEAD_DIM
KV_W = N_KV_HEADS * HEAD_DIM
GROUP_W = Q_GROUP * HEAD_DIM
IN_WIDTH = Q_W + 2 * KV_W + 2 * LRU_WIDTH + 2 * D_MODEL
OFF_Q = 0
OFF_K = OFF_Q + Q_W
OFF_V = OFF_K + KV_W
OFF_XR = OFF_V + KV_W
OFF_XG = OFF_XR + LRU_WIDTH
OFF_GA = OFF_XG + LRU_WIDTH
OFF_GR = OFF_GA + D_MODEL

V7X_LANES = 128
V7X_VMEM_LIMIT_BYTES = 56 * 1024 * 1024
TOKEN_TILE = 512
ATTN_Q_TILE = 1024
LRU_STEPS = 32
EXPERT_TILE = 512
EXPERT_SLICES = 2
RANK_TILE = 512
INVERT_CHUNK = 8192
SPARE_ROWS = 1024

F32 = jnp.float32
BF16 = jnp.bfloat16


def _params(*sem):
    return pltpu.CompilerParams(dimension_semantics=sem, vmem_limit_bytes=V7X_VMEM_LIMIT_BYTES)


def _bdot(a, b):
    return jnp.dot(a, b, preferred_element_type=F32)


def _rms(x, g):
    return x * lax.rsqrt(jnp.mean(x * x, axis=-1, keepdims=True) + EPS) * g


ROW_TILE = 8


def _store_row_tiles(ref, x):
    n = x.shape[0]
    for j in range(ROW_TILE):
        ref[pl.ds(j, n, stride=ROW_TILE), :] = x[:, j * V7X_LANES:(j + 1) * V7X_LANES]


def _load_row_tiles(ref):
    n = ref.shape[0] // ROW_TILE
    return jnp.concatenate([ref[pl.ds(j, n, stride=ROW_TILE), :] for j in range(ROW_TILE)], axis=-1)


def _sigmoid(x):
    return 0.5 * jnp.tanh(0.5 * x) + 0.5


def _in_proj_kernel(x_ref, g_ref, w_ref, qn_ref, kn_ref, cos_ref, sina_ref, sinb_ref,
                    q_ref, k_ref, v_ref, xr_ref, xg_ref, ga_ref, gr_ref):
    u = _rms(x_ref[...], g_ref[...]).astype(BF16)
    cos, sina, sinb = cos_ref[...], sina_ref[...], sinb_ref[...]

    def head(z, gain):
        y = _rms(z, gain)
        return y * cos + pltpu.roll(y, 96, 1) * sina + pltpu.roll(y, 32, 1) * sinb

    zq = _bdot(u, w_ref[:, OFF_Q:OFF_Q + Q_W])
    for h in range(N_Q_HEADS):
        sl = slice(h * HEAD_DIM, (h + 1) * HEAD_DIM)
        q_ref[:, sl] = head(zq[:, sl], qn_ref[...]).astype(BF16)
    zk = _bdot(u, w_ref[:, OFF_K:OFF_K + KV_W])
    for h in range(N_KV_HEADS):
        sl = slice(h * HEAD_DIM, (h + 1) * HEAD_DIM)
        k_ref[:, sl] = head(zk[:, sl], kn_ref[...]).astype(BF16)
    v_ref[...] = _bdot(u, w_ref[:, OFF_V:OFF_V + KV_W]).astype(BF16)
    xr_ref[...] = _bdot(u, w_ref[:, OFF_XR:OFF_XR + LRU_WIDTH])
    xg_ref[...] = _bdot(u, w_ref[:, OFF_XG:OFF_XG + LRU_WIDTH])
    ga_ref[...] = _bdot(u, w_ref[:, OFF_GA:OFF_GA + D_MODEL])
    gr_ref[...] = _bdot(u, w_ref[:, OFF_GR:OFF_GR + D_MODEL])


def _in_proj(x, g_mix, w_in, qn, kn, cos, sina, sinb):
    B, S, D = x.shape
    tm = TOKEN_TILE
    bm = lambda w: pl.BlockSpec((None, tm, w), lambda b, i: (b, i, 0))
    full = lambda a: pl.BlockSpec(a.shape, lambda b, i: (0,) * a.ndim)
    tab = pl.BlockSpec((tm, HEAD_DIM), lambda b, i: (i, 0))
    return pl.pallas_call(
        _in_proj_kernel,
        grid=(B, S // tm),
        in_specs=[bm(D), full(g_mix), full(w_in), full(qn), full(kn), tab, tab, tab],
        out_specs=[bm(Q_W), bm(KV_W), bm(KV_W), bm(LRU_WIDTH), bm(LRU_WIDTH), bm(D), bm(D)],
        out_shape=[
            jax.ShapeDtypeStruct((B, S, Q_W), BF16),
            jax.ShapeDtypeStruct((B, S, KV_W), BF16),
            jax.ShapeDtypeStruct((B, S, KV_W), BF16),
            jax.ShapeDtypeStruct((B, S, LRU_WIDTH), F32),
            jax.ShapeDtypeStruct((B, S, LRU_WIDTH), F32),
            jax.ShapeDtypeStruct((B, S, D), F32),
            jax.ShapeDtypeStruct((B, S, D), F32),
        ],
        compiler_params=_params("arbitrary", "arbitrary"),
        name="in_proj",
    )(x, g_mix, w_in, qn, kn, cos, sina, sinb)


def _attn_kernel(q_ref, k_ref, v_ref, o_ref):
    k = k_ref[...]
    v1 = jnp.concatenate([v_ref[...], jnp.ones(v_ref.shape, v_ref.dtype)], axis=-1)
    for h in range(Q_GROUP):
        sl = slice(h * HEAD_DIM, (h + 1) * HEAD_DIM)
        s = lax.dot_general(q_ref[:, sl], k, (((1,), (1,)), ((), ())), preferred_element_type=F32)
        p = jnp.exp2(s - jnp.max(s, axis=-1, keepdims=True))
        o = _bdot(p.astype(BF16), v1)
        o_ref[:, sl] = (o[:, :HEAD_DIM] * (1.0 / o[:, HEAD_DIM:HEAD_DIM + 1])).astype(BF16)


def _attention(q, k, v):
    B, S, _ = q.shape
    tq = ATTN_Q_TILE
    return pl.pallas_call(
        _attn_kernel,
        grid=(B, N_KV_HEADS, S // tq),
        in_specs=[
            pl.BlockSpec((None, tq, GROUP_W), lambda b, g, i: (b, i, g)),
            pl.BlockSpec((None, S, HEAD_DIM), lambda b, g, i: (b, 0, g)),
            pl.BlockSpec((None, S, HEAD_DIM), lambda b, g, i: (b, 0, g)),
        ],
        out_specs=pl.BlockSpec((None, tq, GROUP_W), lambda b, g, i: (b, i, g)),
        out_shape=jax.ShapeDtypeStruct((B, S, Q_W), BF16),
        compiler_params=_params("arbitrary", "arbitrary", "arbitrary"),
        name="attention",
    )(q, k, v)


def _gelu_tanh(x):
    return 0.5 * x * (1.0 + jnp.tanh(np.sqrt(2.0 / np.pi) * (x + 0.044715 * (x * x * x))))


def _lru_kernel(*refs, batch, reverse):
    if reverse:
        (xc_ref, xp_ref, xn_ref, cw_ref, cb_ref, w_ref, b_ref, lam_ref, hf_ref, xg_ref,
         o_ref, ext_ref, a_ref, bt_ref, h_ref) = refs
    else:
        (xc_ref, xp_ref, xn_ref, cw_ref, cb_ref, w_ref, b_ref, lam_ref,
         o_ref, ext_ref, a_ref, bt_ref, h_ref) = refs
    step = pl.program_id(0)
    nsteps = pl.num_programs(0)
    chunk = (nsteps - 1 - step) if reverse else step
    rows = LRU_STEPS * batch

    @pl.when(step == 0)
    def _():
        h_ref[...] = jnp.zeros_like(h_ref)

    ext_ref[0:2 * batch, :] = jnp.where(chunk > 0, xp_ref[...], 0.0)
    ext_ref[2 * batch:2 * batch + rows, :] = xc_ref[...]
    ext_ref[2 * batch + rows:3 * batch + rows, :] = jnp.where(chunk < nsteps - 1, xn_ref[...], 0.0)
    c = cb_ref[...] + sum(cw_ref[j:j + 1, :] * ext_ref[j * batch:j * batch + rows, :] for j in range(4))
    cb16 = c.astype(BF16)
    lam = lam_ref[...]
    half_scale = (-0.5 * LRU_C) * (jnp.maximum(-lam, 0.0) + jnp.log1p(jnp.exp(-jnp.abs(lam))))
    for n in range(LRU_BLOCKS):
        sl = slice(n * LRU_BLOCK_DIM, (n + 1) * LRU_BLOCK_DIM)
        th = jnp.tanh(_bdot(cb16[:, sl], w_ref[n]) + b_ref[n])
        ig = 0.5 * th[:, LRU_BLOCK_DIM:] + 0.5
        log_a = half_scale[:, sl] * th[:, :LRU_BLOCK_DIM] + half_scale[:, sl]
        a = jnp.exp(log_a)
        a_ref[:, sl] = a
        one_minus_a2 = -jnp.tanh(log_a) * (a * a + 1.0)
        root = jnp.where(one_minus_a2 > 0.0, one_minus_a2 * lax.rsqrt(one_minus_a2), 0.0)
        bt_ref[:, sl] = root * ig * c[:, sl]

    def scan_step(t, h):
        tt = (LRU_STEPS - 1 - t) if reverse else t
        rs = pl.ds(pl.multiple_of(tt * batch, batch), batch)
        h = a_ref[rs, :] * h + bt_ref[rs, :]
        if reverse:
            o_ref[rs, :] = ((h + hf_ref[rs, :]) * _gelu_tanh(xg_ref[rs, :])).astype(o_ref.dtype)
        else:
            o_ref[rs, :] = h
        return h

    h_ref[...] = lax.fori_loop(0, LRU_STEPS, scan_step, h_ref[...], unroll=4)


def _lru_scan(xr2, conv_w, conv_b, wcat, bcat, lam, *, batch, reverse, h_fwd=None, xg2=None):
    n_rows, width = xr2.shape
    rows = LRU_STEPS * batch
    n = n_rows // rows
    ch = (lambda i: n - 1 - i) if reverse else (lambda i: i)
    tile = pl.BlockSpec((rows, width), lambda i: (ch(i), 0))
    prev = pl.BlockSpec((2 * batch, width), lambda i: (jnp.maximum(ch(i) * (LRU_STEPS // 2) - 1, 0), 0))
    nxt = pl.BlockSpec((batch, width),
                       lambda i: (jnp.minimum((ch(i) + 1) * LRU_STEPS, n * LRU_STEPS - 1), 0))
    full = lambda a: pl.BlockSpec(a.shape, lambda i: (0,) * a.ndim)
    ins = [xr2, xr2, xr2, conv_w, conv_b, wcat, bcat, lam]
    specs = [tile, prev, nxt, full(conv_w), full(conv_b), full(wcat), full(bcat), full(lam)]
    if reverse:
        ins += [h_fwd, xg2]
        specs += [tile, tile]
    return pl.pallas_call(
        functools.partial(_lru_kernel, batch=batch, reverse=reverse),
        grid=(n,),
        in_specs=specs,
        out_specs=tile,
        out_shape=jax.ShapeDtypeStruct((n_rows, width), BF16 if reverse else F32),
        scratch_shapes=[
            pltpu.VMEM((rows + 3 * batch, width), F32),
            pltpu.VMEM((rows, width), F32),
            pltpu.VMEM((rows, width), F32),
            pltpu.VMEM((batch, width), F32),
        ],
        compiler_params=_params("arbitrary"),
        name="lru_bwd" if reverse else "lru_fwd",
    )(*ins)


def _merge_kernel(x_ref, at_ref, yl_ref, ga_ref, gr_ref, wa_ref, wl_ref, wo_ref, gm_ref, wr_ref, br_ref,
                  h_ref, u_ref, idx_ref, gate_ref, cnt_ref):
    y_attn = _bdot(at_ref[...], wa_ref[...])
    y_lru = _bdot(yl_ref[...], wl_ref[...])
    merged = _sigmoid(ga_ref[...]) * y_attn + _sigmoid(gr_ref[...]) * y_lru
    h = x_ref[...] + _bdot(merged.astype(BF16), wo_ref[...])
    h_ref[...] = h
    u = _rms(h, gm_ref[...])
    _store_row_tiles(u_ref, u)
    u_hi = u.astype(BF16)
    u_lo = (u - u_hi.astype(F32)).astype(BF16)
    logits = _bdot(u_hi, wr_ref[0]) + (_bdot(u_lo, wr_ref[0]) + _bdot(u_hi, wr_ref[1]))
    lane = lax.broadcasted_iota(jnp.int32, logits.shape, 1)
    work = jnp.where(lane < N_EXPERTS, logits + br_ref[...], -jnp.inf)
    vals, hits = [], jnp.zeros(logits.shape, F32)
    for kk in range(TOP_K):
        m = jnp.max(work, axis=-1, keepdims=True)
        idx = jnp.min(jnp.where(work == m, lane, V7X_LANES), axis=-1, keepdims=True)
        pick = lane == idx
        work = jnp.where(pick, -jnp.inf, work)
        hits = hits + pick.astype(F32)
        vals.append(m)
        idx_ref[:, kk:kk + 1] = idx
    es = [jnp.exp(vv - vals[0]) for vv in vals]
    inv = 1.0 / sum(es)
    for kk in range(TOP_K):
        gate_ref[:, kk:kk + 1] = es[kk] * inv

    @pl.when((pl.program_id(0) == 0) & (pl.program_id(1) == 0))
    def _():
        cnt_ref[...] = jnp.zeros_like(cnt_ref)

    cnt_ref[0:1, :] += jnp.sum(hits, axis=0, keepdims=True)


def _merge(x, attn, ylru, ga, gr, wa, wl, wo, g_moe, wr, br):
    B, S, D = x.shape
    tm = TOKEN_TILE
    bm = lambda w: pl.BlockSpec((None, tm, w), lambda b, i: (b, i, 0))
    full = lambda a: pl.BlockSpec(a.shape, lambda b, i: (0,) * a.ndim)
    return pl.pallas_call(
        _merge_kernel,
        grid=(B, S // tm),
        in_specs=[bm(D), bm(Q_W), bm(LRU_WIDTH), bm(D), bm(D), full(wa), full(wl), full(wo),
                  full(g_moe), full(wr), full(br)],
        out_specs=[bm(D), pl.BlockSpec((None, tm * ROW_TILE, V7X_LANES), lambda b, i: (b, i, 0)),
                   bm(TOP_K), bm(TOP_K), pl.BlockSpec((8, V7X_LANES), lambda b, i: (0, 0))],
        out_shape=[
            jax.ShapeDtypeStruct((B, S, D), F32),
            jax.ShapeDtypeStruct((B, S * ROW_TILE, V7X_LANES), F32),
            jax.ShapeDtypeStruct((B, S, TOP_K), jnp.int32),
            jax.ShapeDtypeStruct((B, S, TOP_K), F32),
            jax.ShapeDtypeStruct((8, V7X_LANES), F32),
        ],
        compiler_params=_params("arbitrary", "arbitrary"),
        name="merge_router",
    )(x, attn, ylru, ga, gr, wa, wl, wo, g_moe, wr, br)


def _rank_kernel(idx_ref, ps_ref, dest_ref, carry_ref):
    @pl.when(pl.program_id(0) == 0)
    def _():
        carry_ref[...] = jnp.zeros_like(carry_ref)

    tm = idx_ref.shape[0]
    lane = lax.broadcasted_iota(jnp.int32, (tm, V7X_LANES), 1)
    picks = [lane == idx_ref[:, kk:kk + 1] for kk in range(TOP_K)]
    hits = sum(p.astype(F32) for p in picks)
    earlier = (lax.broadcasted_iota(jnp.int32, (tm, tm), 0) >
               lax.broadcasted_iota(jnp.int32, (tm, tm), 1)).astype(BF16)
    base = _bdot(earlier, hits.astype(BF16)) + carry_ref[...] + ps_ref[...]
    for kk in range(TOP_K):
        dest_ref[:, kk:kk + 1] = jnp.sum(jnp.where(picks[kk], base, 0.0), axis=-1,
                                         keepdims=True).astype(jnp.int32)
    carry_ref[...] += jnp.sum(hits, axis=0, keepdims=True)


def _rank(idx, pad_start_row):
    T = idx.shape[0]
    tm = RANK_TILE
    return pl.pallas_call(
        _rank_kernel,
        grid=(T // tm,),
        in_specs=[pl.BlockSpec((tm, TOP_K), lambda i: (i, 0)),
                  pl.BlockSpec((1, V7X_LANES), lambda i: (0, 0))],
        out_specs=pl.BlockSpec((tm, TOP_K), lambda i: (i, 0)),
        out_shape=jax.ShapeDtypeStruct((T, TOP_K), jnp.int32),
        scratch_shapes=[pltpu.VMEM((1, V7X_LANES), F32)],
        compiler_params=_params("arbitrary"),
        name="slot_rank",
    )(idx, pad_start_row)


def _invert_kernel(pad_end_ref, count_ref, dest_vmem, out_vmem, dest_ref, inv_ref, sem, *, n_tokens, chunk):
    step = pl.program_id(0)
    stage_in = pltpu.make_async_copy(dest_vmem, dest_ref, sem)
    stage_in.start()
    stage_in.wait()

    @pl.when(step == 0)
    def _():
        def fill(s, carry):
            inv_ref[s] = TOP_K * n_tokens + (s & (SPARE_ROWS - 1))
            return carry

        for e in range(N_EXPERTS):
            lo = pad_end_ref[e] - ((count_ref[e] + EXPERT_TILE - 1) & -EXPERT_TILE) + count_ref[e]
            lax.fori_loop(lo, pad_end_ref[e], fill, 0)
        lax.fori_loop(pad_end_ref[N_EXPERTS - 1], inv_ref.shape[0], fill, 0)

    group = 4 * TOP_K

    def body(g, carry):
        tok0 = step * (chunk // TOP_K) + g * (group // TOP_K)
        for u in range(group):
            inv_ref[dest_ref[g * group + u]] = tok0 + ((u % TOP_K) * n_tokens + u // TOP_K)
        return carry

    lax.fori_loop(0, chunk // group, body, 0)

    @pl.when(step == pl.num_programs(0) - 1)
    def _():
        stage_out = pltpu.make_async_copy(inv_ref, out_vmem, sem)
        stage_out.start()
        stage_out.wait()


def _invert(pad_end, counts, dest_flat, cap, n_tokens):
    chunk = INVERT_CHUNK
    return pl.pallas_call(
        functools.partial(_invert_kernel, n_tokens=n_tokens, chunk=chunk),
        grid_spec=pltpu.PrefetchScalarGridSpec(
            num_scalar_prefetch=2,
            grid=(dest_flat.shape[0] // chunk,),
            in_specs=[pl.BlockSpec((chunk,), lambda i, pe, ct: (i,))],
            out_specs=pl.BlockSpec((cap,), lambda i, pe, ct: (0,)),
            scratch_shapes=[pltpu.SMEM((chunk,), jnp.int32), pltpu.SMEM((cap,), jnp.int32),
                            pltpu.SemaphoreType.DMA(())],
        ),
        out_shape=jax.ShapeDtypeStruct((cap,), jnp.int32),
        compiler_params=_params("arbitrary"),
        name="slot_invert",
    )(pad_end, counts, dest_flat)


def _expert_kernel(be_ref, nu_ref, invp_ref, invc_ref, invn_ref, u_hbm, wgu_ref, bgu_ref, wdn_ref, bdn_ref,
                   y_hbm, xbuf0, xbuf1, ybuf0, ybuf1, gsem, ssem, *, n_tokens):
    xbuf, ybuf = (xbuf0, xbuf1), (ybuf0, ybuf1)
    j = pl.program_id(0)
    n_used = nu_ref[0]
    parity = j & 1

    def tile_rows(row):
        if isinstance(row, int):
            return pl.ds(row * ROW_TILE, ROW_TILE)
        return pl.ds(pl.multiple_of(row * ROW_TILE, ROW_TILE), ROW_TILE)

    def gather(inv_ref, r, buf, dep=0):
        tok = (inv_ref[0, 0, r] & (n_tokens - 1)) + dep
        return pltpu.make_async_copy(u_hbm.at[tile_rows(tok)], xbuf[buf].at[tile_rows(r)], gsem.at[buf])

    def scatter(inv_ref, r, buf, dep=0):
        return pltpu.make_async_copy(ybuf[buf].at[tile_rows(r)], y_hbm.at[tile_rows(inv_ref[0, 0, r] + dep)],
                                     ssem.at[buf])

    def wait_gather(buf):
        pltpu.make_async_copy(xbuf[buf], xbuf[buf], gsem.at[buf]).wait()

    def wait_scatter(buf):
        pltpu.make_async_copy(ybuf[buf], ybuf[buf], ssem.at[buf]).wait()

    @pl.when(j == 0)
    def _():
        def first(r, carry):
            gather(invc_ref, r, 0).start()
            return carry
        lax.fori_loop(0, EXPERT_TILE, first, 0)
        xbuf[1][...] = jnp.zeros(xbuf[1].shape, xbuf[1].dtype)
        ybuf[1][...] = jnp.zeros(ybuf[1].shape, ybuf[1].dtype)
        clears = [pltpu.make_async_copy(
            xbuf[1], y_hbm.at[pl.ds((TOP_K * n_tokens + c * EXPERT_TILE) * ROW_TILE, EXPERT_TILE * ROW_TILE)],
            ssem.at[1]) for c in range(SPARE_ROWS // EXPERT_TILE)]
        for c in clears:
            c.start()
        for c in clears:
            c.wait()

    def block(cur):
        oth = 1 - cur
        wait_gather(cur)

        @pl.when(j > 0)
        def _():
            wait_scatter(cur)

        x = _load_row_tiles(xbuf[cur]).astype(BF16)
        per = EXPERT_TILE // EXPERT_SLICES
        fc = D_EXPERT // EXPERT_SLICES
        acc = None
        dep = 0
        for c in range(EXPERT_SLICES):
            for r in range(c * per, (c + 1) * per):
                gather(invn_ref, r, oth, dep).start(priority=r % 2)
            for r in range(c * per, (c + 1) * per):
                scatter(invp_ref, r, oth, dep).start(priority=r % 2)
            gs = slice(c * fc, (c + 1) * fc)
            us = slice(D_EXPERT + c * fc, D_EXPERT + (c + 1) * fc)
            gate = jnp.minimum(_bdot(x, wgu_ref[:, gs].astype(BF16)) + bgu_ref[:, gs], SWIGLU_LIMIT)
            up = jnp.clip(_bdot(x, wgu_ref[:, us].astype(BF16)) + bgu_ref[:, us], -SWIGLU_LIMIT, SWIGLU_LIMIT)
            act = gate * _sigmoid(SWIGLU_ALPHA * gate)
            part = _bdot(((up + 1.0) * act).astype(BF16), wdn_ref[gs, :].astype(BF16))
            acc = part if acc is None else acc + part
            bits = pltpu.bitcast(part[0:8, 0:V7X_LANES], jnp.uint32)[0, 0]
            dep = lax.shift_right_logical(lax.shift_right_logical(bits, jnp.uint32(16)),
                                          jnp.uint32(16)).astype(jnp.int32)
        _store_row_tiles(ybuf[cur], acc + bdn_ref[...])

        @pl.when(j == n_used - 1)
        def _():
            wait_scatter(oth)
            wait_gather(oth)

            def last(r, carry):
                scatter(invc_ref, r, cur).start()
                return carry
            lax.fori_loop(0, EXPERT_TILE, last, 0)
            wait_scatter(cur)

    for cur in range(2):
        @pl.when((j < n_used) & (parity == cur))
        def _():
            block(cur)


def _experts(blk_exp, n_used, inv3, u, w_gu, b_gu, w_dn, b_dn):
    T, D = u.shape[0] // ROW_TILE, D_MODEL
    n_blocks = inv3.shape[0]
    exp3 = lambda j, be, nu: (be[j], 0, 0)
    inv_spec = lambda f: pl.BlockSpec((1, 1, EXPERT_TILE), lambda j, be, nu: (f(j, nu), 0, 0),
                                      memory_space=pltpu.SMEM)
    last = lambda j, nu: jnp.minimum(j, nu[0] - 1)
    return pl.pallas_call(
        functools.partial(_expert_kernel, n_tokens=T),
        grid_spec=pltpu.PrefetchScalarGridSpec(
            num_scalar_prefetch=2,
            grid=(n_blocks,),
            in_specs=[inv_spec(lambda j, nu: jnp.maximum(last(j, nu) - 1, 0)),
                      inv_spec(last),
                      inv_spec(lambda j, nu: jnp.minimum(j + 1, nu[0] - 1)),
                      pl.BlockSpec(memory_space=pl.ANY),
                      pl.BlockSpec((None, D, 2 * D_EXPERT), exp3),
                      pl.BlockSpec((None, 1, 2 * D_EXPERT), exp3),
                      pl.BlockSpec((None, D_EXPERT, D), exp3),
                      pl.BlockSpec((None, 1, D), exp3)],
            out_specs=pl.BlockSpec(memory_space=pl.ANY),
            scratch_shapes=[pltpu.VMEM((EXPERT_TILE * ROW_TILE, V7X_LANES), F32)] * 4 + [
                pltpu.SemaphoreType.DMA((2,)), pltpu.SemaphoreType.DMA((2,))],
        ),
        out_shape=jax.ShapeDtypeStruct(((TOP_K * T + SPARE_ROWS) * ROW_TILE, V7X_LANES), F32),
        compiler_params=_params("arbitrary"),
        name="experts",
    )(blk_exp, n_used, inv3, inv3, inv3, u, w_gu, b_gu, w_dn, b_dn)


def _combine_kernel(y0_ref, y1_ref, y2_ref, y3_ref, gate_ref, h_ref, p_ref, gp_ref, wg_ref, wp_ref, o_ref):
    h = h_ref[...]
    for kk, y_ref in enumerate((y0_ref, y1_ref, y2_ref, y3_ref)):
        h = h + gate_ref[:, kk:kk + 1] * _load_row_tiles(y_ref)
    u = _rms(h, gp_ref[...]).astype(BF16)
    o_ref[...] = h + _sigmoid(_bdot(u, wg_ref[...])) * _bdot(p_ref[...].astype(BF16), wp_ref[...])


def _combine(y4, gate, h, p, g_ple, wg, wp):
    T, D = h.shape
    tm = TOKEN_TILE
    full = lambda a: pl.BlockSpec(a.shape, lambda i: (0,) * a.ndim)
    plane = lambda kk: pl.BlockSpec((tm * ROW_TILE, V7X_LANES), lambda i: (kk * (T // tm) + i, 0))
    return pl.pallas_call(
        _combine_kernel,
        grid=(T // tm,),
        in_specs=[plane(0), plane(1), plane(2), plane(3),
                  pl.BlockSpec((tm, TOP_K), lambda i: (i, 0)),
                  pl.BlockSpec((tm, D), lambda i: (i, 0)),
                  pl.BlockSpec((tm, PLE_DIM), lambda i: (i, 0)),
                  full(g_ple), full(wg), full(wp)],
        out_specs=pl.BlockSpec((tm, D), lambda i: (i, 0)),
        out_shape=jax.ShapeDtypeStruct((T, D), F32),
        compiler_params=_params("arbitrary"),
        name="combine_ple",
    )(y4, y4, y4, y4, gate, h, p, g_ple, wg, wp)


def _rope_tables(seq_len):
    pos = np.arange(seq_len)
    axis_dim = HEAD_DIM // 2
    inv_freq = (ROPE_THETA ** (-np.arange(0, axis_dim, 2, dtype=np.float32) / axis_dim)).astype(np.float32)
    ang_r = (pos // GRID_W).astype(np.float32)[:, None] * inv_freq[None, :]
    ang_c = (pos % GRID_W).astype(np.float32)[:, None] * inv_freq[None, :]
    zero = np.zeros_like(ang_r)
    cos = np.concatenate([np.cos(ang_r)] * 2 + [np.cos(ang_c)] * 2, axis=-1)
    sina = np.concatenate([-np.sin(ang_r), zero, -np.sin(ang_c), zero], axis=-1)
    sinb = np.concatenate([zero, np.sin(ang_r), zero, np.sin(ang_c)], axis=-1)
    return jnp.asarray(cos, F32), jnp.asarray(sina, F32), jnp.asarray(sinb, F32)


def _layer(h, p, g_mix, w_in, q_norm, k_norm, conv_w, conv_b, lru_wa, lru_ba, lru_wi, lru_bi, lru_lam,
           w_attn_br, w_lru_br, w_out, g_moe, w_router, b_router, w_gu, b_gu, w_dn, b_dn,
           g_ple, w_ple_gate, w_ple_proj, rope):
    B, S, D = h.shape
    T = B * S
    assert S % ATTN_Q_TILE == 0 and S % TOKEN_TILE == 0 and S % LRU_STEPS == 0 and B % 8 == 0
    assert T % RANK_TILE == 0 and (T * TOP_K) % INVERT_CHUNK == 0
    row = lambda a: a.reshape(1, -1)
    q, k, v, xr, xg, ga, gr = _in_proj(
        h, row(g_mix), w_in.astype(BF16), row(q_norm) * (HEAD_DIM ** -0.5 * np.log2(np.e)), row(k_norm), *rope)
    attn = _attention(q, k, v)
    xr2 = xr.transpose(1, 0, 2).reshape(S * B, LRU_WIDTH)
    xg2 = xg.transpose(1, 0, 2).reshape(S * B, LRU_WIDTH)
    wcat = (0.5 * jnp.concatenate([lru_wa, lru_wi], axis=-1)).astype(BF16)
    bcat = 0.5 * jnp.concatenate([lru_ba.reshape(2, LRU_BLOCKS, 1, LRU_BLOCK_DIM),
                                  lru_bi.reshape(2, LRU_BLOCKS, 1, LRU_BLOCK_DIM)], axis=-1)
    h_fwd = _lru_scan(xr2, conv_w, row(conv_b), wcat[0], bcat[0], row(lru_lam[0]), batch=B, reverse=False)
    ylru = _lru_scan(xr2, conv_w, row(conv_b), wcat[1], bcat[1], row(lru_lam[1]), batch=B, reverse=True,
                     h_fwd=h_fwd, xg2=xg2).reshape(S, B, LRU_WIDTH).transpose(1, 0, 2)
    w_router_f = jnp.zeros((D, V7X_LANES), F32).at[:, :N_EXPERTS].set(w_router)
    w_router_hi = w_router_f.astype(BF16)
    w_router_p = jnp.stack([w_router_hi, (w_router_f - w_router_hi.astype(F32)).astype(BF16)])
    b_router_p = jnp.zeros((1, V7X_LANES), F32).at[0, :N_EXPERTS].set(b_router)
    h1, u2, idx, gate, cnt = _merge(h, attn, ylru, ga, gr, w_attn_br.astype(BF16), w_lru_br.astype(BF16),
                                    w_out.astype(BF16), row(g_moe), w_router_p, b_router_p)
    cap = T * TOP_K + N_EXPERTS * EXPERT_TILE
    counts = cnt[0, :N_EXPERTS].astype(jnp.int32)
    padded = ((counts + EXPERT_TILE - 1) // EXPERT_TILE) * EXPERT_TILE
    pad_end = jnp.cumsum(padded)
    pad_start = pad_end - padded
    blk_start = jnp.arange(cap // EXPERT_TILE, dtype=jnp.int32) * EXPERT_TILE
    blk_exp = jnp.minimum(jnp.sum(blk_start[:, None] >= pad_end[None, :], axis=1), N_EXPERTS - 1)
    n_used = (pad_end[-1:] // EXPERT_TILE).astype(jnp.int32)
    ps_row = jnp.zeros((1, V7X_LANES), F32).at[0, :N_EXPERTS].set(pad_start.astype(F32))
    dest = _rank(idx.reshape(T, TOP_K), ps_row).reshape(T * TOP_K)
    assert T & (T - 1) == 0, "token count must be a power of two for the slot encoding"
    inv3 = _invert(pad_end.astype(jnp.int32), counts, dest, cap, T).reshape(cap // EXPERT_TILE, 1, EXPERT_TILE)
    y4 = _experts(blk_exp.astype(jnp.int32), n_used, inv3, u2.reshape(T * ROW_TILE, V7X_LANES), w_gu,
                  b_gu.reshape(N_EXPERTS, 1, -1), w_dn, b_dn.reshape(N_EXPERTS, 1, -1))
    out = _combine(y4, gate.reshape(T, TOP_K), h1.reshape(T, D),
                   p.reshape(T, PLE_DIM), row(g_ple), w_ple_gate.astype(BF16), w_ple_proj.astype(BF16))
    return out.reshape(B, S, D)


def kernel(x, p, g_mix, w_in, q_norm, k_norm, conv_w, conv_b, lru_wa, lru_ba, lru_wi, lru_bi, lru_lam,
           w_attn_br, w_lru_br, w_out, g_moe, w_router, b_router, w_gu, b_gu, w_dn, b_dn,
           g_ple, w_ple_gate, w_ple_proj):
    rope = _rope_tables(x.shape[1])
    h = x
    for l in range(p.shape[0]):
        h = _layer(h, p[l], g_mix[l], w_in[l], q_norm[l], k_norm[l], conv_w[l], conv_b[l], lru_wa[l],
                   lru_ba[l], lru_wi[l], lru_bi[l], lru_lam[l], w_attn_br[l], w_lru_br[l], w_out[l],
                   g_moe[l], w_router[l], b_router[l], w_gu[l], b_gu[l], w_dn[l], b_dn[l],
                   g_ple[l], w_ple_gate[l], w_ple_proj[l], rope)
    return h
```

```python
import functools

import jax
import jax.numpy as jnp
import numpy as np
from jax import lax
from jax.experimental import pallas as pl
from jax.experimental.pallas import tpu as pltpu

D_MODEL = 1024
N_Q_HEADS = 8
N_KV_HEADS = 2
HEAD_DIM = 128
Q_GROUP = N_Q_HEADS // N_KV_HEADS
ROPE_THETA = 10000.0
GRID_W = 64
LRU_WIDTH = 1024
LRU_BLOCKS = 8
LRU_BLOCK_DIM = LRU_WIDTH // LRU_BLOCKS
LRU_C = 8.0
N_EXPERTS = 32
TOP_K = 4
D_EXPERT = 1024
SWIGLU_LIMIT = 7.0
SWIGLU_ALPHA = 1.702
PLE_DIM = 256
EPS = 1e-6
Q_W = N_Q_HEADS * HEAD_DIM
KV_W = N_KV_HEADS * HEAD_DIM
IN_WIDTH = Q_W + 2 * KV_W + 2 * LRU_WIDTH + 2 * D_MODEL
OFF_Q = 0
OFF_K = OFF_Q + Q_W
OFF_V = OFF_K + KV_W
OFF_XR = OFF_V + KV_W
OFF_XG = OFF_XR + LRU_WIDTH
OFF_GA = OFF_XG + LRU_WIDTH
OFF_GR = OFF_GA + D_MODEL

V7X_LANES = 128
V7X_VMEM_LIMIT_BYTES = 56 * 1024 * 1024
TOKEN_TILE = 512
ATTN_Q_TILE = 1024
LRU_STEPS = 32
EXPERT_TILE = 512
EXPERT_SLICES = 2
RANK_TILE = 512
INVERT_CHUNK = 8192
SPARE_ROWS = 1024

F32 = jnp.float32
BF16 = jnp.bfloat16


def _params(*sem):
    return pltpu.CompilerParams(dimension_semantics=sem, vmem_limit_bytes=V7X_VMEM_LIMIT_BYTES)


def _bdot(a, b):
    return jnp.dot(a, b, preferred_element_type=F32)


def _rms(x, g):
    return x * lax.rsqrt(jnp.mean(x * x, axis=-1, keepdims=True) + EPS) * g


ROW_TILE = 8


def _store_row_tiles(ref, x):
    n = x.shape[0]
    for j in range(ROW_TILE):
        ref[pl.ds(j, n, stride=ROW_TILE), :] = x[:, j * V7X_LANES:(j + 1) * V7X_LANES]


def _load_row_tiles(ref):
    n = ref.shape[0] // ROW_TILE
    return jnp.concatenate([ref[pl.ds(j, n, stride=ROW_TILE), :] for j in range(ROW_TILE)], axis=-1)


def _sigmoid(x):
    return 0.5 * jnp.tanh(0.5 * x) + 0.5


def _in_proj_kernel(x_ref, g_ref, w_ref, qn_ref, kn_ref, cos_ref, sina_ref, sinb_ref,
                    q_ref, k_ref, v_ref, xr_ref, xg_ref, ga_ref, gr_ref):
    u = _rms(x_ref[...], g_ref[...]).astype(BF16)
    cos, sina, sinb = cos_ref[...], sina_ref[...], sinb_ref[...]

    def head(z, gain):
        y = _rms(z, gain)
        return y * cos + pltpu.roll(y, 96, 1) * sina + pltpu.roll(y, 32, 1) * sinb

    zq = _bdot(u, w_ref[:, OFF_Q:OFF_Q + Q_W])
    for h in range(N_Q_HEADS):
        sl = slice(h * HEAD_DIM, (h + 1) * HEAD_DIM)
        q_ref[:, sl] = head(zq[:, sl], qn_ref[...]).astype(BF16)
    zk = _bdot(u, w_ref[:, OFF_K:OFF_K + KV_W])
    for h in range(N_KV_HEADS):
        sl = slice(h * HEAD_DIM, (h + 1) * HEAD_DIM)
        k_ref[:, sl] = head(zk[:, sl], kn_ref[...]).astype(BF16)
    v_ref[...] = _bdot(u, w_ref[:, OFF_V:OFF_V + KV_W]).astype(BF16)
    xr_ref[...] = _bdot(u, w_ref[:, OFF_XR:OFF_XR + LRU_WIDTH])
    xg_ref[...] = _bdot(u, w_ref[:, OFF_XG:OFF_XG + LRU_WIDTH])
    ga_ref[...] = _bdot(u, w_ref[:, OFF_GA:OFF_GA + D_MODEL])
    gr_ref[...] = _bdot(u, w_ref[:, OFF_GR:OFF_GR + D_MODEL])


def _in_proj(x, g_mix, w_in, qn, kn, cos, sina, sinb):
    B, S, D = x.shape
    tm = TOKEN_TILE
    bm = lambda w: pl.BlockSpec((None, tm, w), lambda b, i: (b, i, 0))
    full = lambda a: pl.BlockSpec(a.shape, lambda b, i: (0,) * a.ndim)
    tab = pl.BlockSpec((tm, HEAD_DIM), lambda b, i: (i, 0))
    return pl.pallas_call(
        _in_proj_kernel,
        grid=(B, S // tm),
        in_specs=[bm(D), full(g_mix), full(w_in), full(qn), full(kn), tab, tab, tab],
        out_specs=[bm(Q_W), bm(KV_W), bm(KV_W), bm(LRU_WIDTH), bm(LRU_WIDTH), bm(D), bm(D)],
        out_shape=[
            jax.ShapeDtypeStruct((B, S, Q_W), BF16),
            jax.ShapeDtypeStruct((B, S, KV_W), BF16),
            jax.ShapeDtypeStruct((B, S, KV_W), BF16),
            jax.ShapeDtypeStruct((B, S, LRU_WIDTH), F32),
            jax.ShapeDtypeStruct((B, S, LRU_WIDTH), F32),
            jax.ShapeDtypeStruct((B, S, D), F32),
            jax.ShapeDtypeStruct((B, S, D), F32),
        ],
        compiler_params=_params("arbitrary", "arbitrary"),
        name="in_proj",
    )(x, g_mix, w_in, qn, kn, cos, sina, sinb)


def _attn_kernel(q_ref, k_ref, v_ref, o_ref):
    for g in range(N_KV_HEADS):
        ks = slice(g * HEAD_DIM, (g + 1) * HEAD_DIM)
        k = k_ref[:, ks]
        v1 = jnp.concatenate([v_ref[:, ks], jnp.ones((v_ref.shape[0], HEAD_DIM), v_ref.dtype)], axis=-1)
        for h in range(g * Q_GROUP, (g + 1) * Q_GROUP):
            sl = slice(h * HEAD_DIM, (h + 1) * HEAD_DIM)
            s = lax.dot_general(q_ref[:, sl], k, (((1,), (1,)), ((), ())), preferred_element_type=F32)
            p = jnp.exp2(s - jnp.max(s, axis=-1, keepdims=True))
            o = _bdot(p.astype(BF16), v1)
            o_ref[:, sl] = (o[:, :HEAD_DIM] * (1.0 / o[:, HEAD_DIM:HEAD_DIM + 1])).astype(BF16)


def _attention(q, k, v):
    B, S, _ = q.shape
    tq = ATTN_Q_TILE
    return pl.pallas_call(
        _attn_kernel,
        grid=(B, S // tq),
        in_specs=[
            pl.BlockSpec((None, tq, Q_W), lambda b, i: (b, i, 0)),
            pl.BlockSpec((None, S, KV_W), lambda b, i: (b, 0, 0)),
            pl.BlockSpec((None, S, KV_W), lambda b, i: (b, 0, 0)),
        ],
        out_specs=pl.BlockSpec((None, tq, Q_W), lambda b, i: (b, i, 0)),
        out_shape=jax.ShapeDtypeStruct((B, S, Q_W), BF16),
        compiler_params=_params("arbitrary", "arbitrary"),
        name="attention",
    )(q, k, v)


def _gelu_tanh(x):
    return 0.5 * x * (1.0 + jnp.tanh(np.sqrt(2.0 / np.pi) * (x + 0.044715 * (x * x * x))))


def _lru_kernel(*refs, batch, reverse):
    if reverse:
        (xc_ref, xp_ref, xn_ref, cw_ref, cb_ref, w_ref, b_ref, lam_ref, hf_ref, xg_ref,
         o_ref, ext_ref, a_ref, bt_ref, h_ref) = refs
    else:
        (xc_ref, xp_ref, xn_ref, cw_ref, cb_ref, w_ref, b_ref, lam_ref,
         o_ref, ext_ref, a_ref, bt_ref, h_ref) = refs
    step = pl.program_id(0)
    nsteps = pl.num_programs(0)
    chunk = (nsteps - 1 - step) if reverse else step
    rows = LRU_STEPS * batch

    @pl.when(step == 0)
    def _():
        h_ref[...] = jnp.zeros_like(h_ref)

    ext_ref[0:2 * batch, :] = jnp.where(chunk > 0, xp_ref[...], 0.0)
    ext_ref[2 * batch:2 * batch + rows, :] = xc_ref[...]
    ext_ref[2 * batch + rows:3 * batch + rows, :] = jnp.where(chunk < nsteps - 1, xn_ref[...], 0.0)
    c = cb_ref[...] + sum(cw_ref[j:j + 1, :] * ext_ref[j * batch:j * batch + rows, :] for j in range(4))
    cb16 = c.astype(BF16)
    lam = lam_ref[...]
    half_scale = (-0.5 * LRU_C) * (jnp.maximum(-lam, 0.0) + jnp.log1p(jnp.exp(-jnp.abs(lam))))
    for n in range(LRU_BLOCKS):
        sl = slice(n * LRU_BLOCK_DIM, (n + 1) * LRU_BLOCK_DIM)
        th = jnp.tanh(_bdot(cb16[:, sl], w_ref[n]) + b_ref[n])
        ig = 0.5 * th[:, LRU_BLOCK_DIM:] + 0.5
        log_a = half_scale[:, sl] * th[:, :LRU_BLOCK_DIM] + half_scale[:, sl]
        a = jnp.exp(log_a)
        a_ref[:, sl] = a
        one_minus_a2 = -jnp.tanh(log_a) * (a * a + 1.0)
        root = jnp.where(one_minus_a2 > 0.0, one_minus_a2 * lax.rsqrt(one_minus_a2), 0.0)
        bt_ref[:, sl] = root * ig * c[:, sl]

    def scan_step(t, h):
        tt = (LRU_STEPS - 1 - t) if reverse else t
        rs = pl.ds(pl.multiple_of(tt * batch, batch), batch)
        h = a_ref[rs, :] * h + bt_ref[rs, :]
        if reverse:
            o_ref[rs, :] = ((h + hf_ref[rs, :]) * _gelu_tanh(xg_ref[rs, :])).astype(o_ref.dtype)
        else:
            o_ref[rs, :] = h
        return h

    h_ref[...] = lax.fori_loop(0, LRU_STEPS, scan_step, h_ref[...], unroll=4)


def _lru_scan(xr2, conv_w, conv_b, wcat, bcat, lam, *, batch, reverse, h_fwd=None, xg2=None):
    n_rows, width = xr2.shape
    rows = LRU_STEPS * batch
    n = n_rows // rows
    ch = (lambda i: n - 1 - i) if reverse else (lambda i: i)
    tile = pl.BlockSpec((rows, width), lambda i: (ch(i), 0))
    prev = pl.BlockSpec((2 * batch, width), lambda i: (jnp.maximum(ch(i) * (LRU_STEPS // 2) - 1, 0), 0))
    nxt = pl.BlockSpec((batch, width),
                       lambda i: (jnp.minimum((ch(i) + 1) * LRU_STEPS, n * LRU_STEPS - 1), 0))
    full = lambda a: pl.BlockSpec(a.shape, lambda i: (0,) * a.ndim)
    ins = [xr2, xr2, xr2, conv_w, conv_b, wcat, bcat, lam]
    specs = [tile, prev, nxt, full(conv_w), full(conv_b), full(wcat), full(bcat), full(lam)]
    if reverse:
        ins += [h_fwd, xg2]
        specs += [tile, tile]
    return pl.pallas_call(
        functools.partial(_lru_kernel, batch=batch, reverse=reverse),
        grid=(n,),
        in_specs=specs,
        out_specs=tile,
        out_shape=jax.ShapeDtypeStruct((n_rows, width), BF16 if reverse else F32),
        scratch_shapes=[
            pltpu.VMEM((rows + 3 * batch, width), F32),
            pltpu.VMEM((rows, width), F32),
            pltpu.VMEM((rows, width), F32),
            pltpu.VMEM((batch, width), F32),
        ],
        compiler_params=_params("arbitrary"),
        name="lru_bwd" if reverse else "lru_fwd",
    )(*ins)


def _merge_kernel(x_ref, at_ref, yl_ref, ga_ref, gr_ref, wa_ref, wl_ref, wo_ref, gm_ref, wr_ref, br_ref,
                  h_ref, u_ref, idx_ref, gate_ref, cnt_ref):
    y_attn = _bdot(at_ref[...], wa_ref[...])
    y_lru = _bdot(yl_ref[...], wl_ref[...])
    merged = _sigmoid(ga_ref[...]) * y_attn + _sigmoid(gr_ref[...]) * y_lru
    h = x_ref[...] + _bdot(merged.astype(BF16), wo_ref[...])
    h_ref[...] = h
    u = _rms(h, gm_ref[...])
    _store_row_tiles(u_ref, u)
    u_hi = u.astype(BF16)
    u_lo = (u - u_hi.astype(F32)).astype(BF16)
    logits = _bdot(u_hi, wr_ref[0]) + (_bdot(u_lo, wr_ref[0]) + _bdot(u_hi, wr_ref[1]))
    lane = lax.broadcasted_iota(jnp.int32, logits.shape, 1)
    work = jnp.where(lane < N_EXPERTS, logits + br_ref[...], -jnp.inf)
    vals, hits = [], jnp.zeros(logits.shape, F32)
    for kk in range(TOP_K):
        m = jnp.max(work, axis=-1, keepdims=True)
        idx = jnp.min(jnp.where(work == m, lane, V7X_LANES), axis=-1, keepdims=True)
        pick = lane == idx
        work = jnp.where(pick, -jnp.inf, work)
        hits = hits + pick.astype(F32)
        vals.append(m)
        idx_ref[:, kk:kk + 1] = idx
    es = [jnp.exp(vv - vals[0]) for vv in vals]
    inv = 1.0 / sum(es)
    for kk in range(TOP_K):
        gate_ref[:, kk:kk + 1] = es[kk] * inv

    @pl.when((pl.program_id(0) == 0) & (pl.program_id(1) == 0))
    def _():
        cnt_ref[...] = jnp.zeros_like(cnt_ref)

    cnt_ref[0:1, :] += jnp.sum(hits, axis=0, keepdims=True)


def _merge(x, attn, ylru, ga, gr, wa, wl, wo, g_moe, wr, br):
    B, S, D = x.shape
    tm = TOKEN_TILE
    bm = lambda w: pl.BlockSpec((None, tm, w), lambda b, i: (b, i, 0))
    full = lambda a: pl.BlockSpec(a.shape, lambda b, i: (0,) * a.ndim)
    return pl.pallas_call(
        _merge_kernel,
        grid=(B, S // tm),
        in_specs=[bm(D), bm(Q_W), bm(LRU_WIDTH), bm(D), bm(D), full(wa), full(wl), full(wo),
                  full(g_moe), full(wr), full(br)],
        out_specs=[bm(D), pl.BlockSpec((None, tm * ROW_TILE, V7X_LANES), lambda b, i: (b, i, 0)),
                   bm(TOP_K), bm(TOP_K), pl.BlockSpec((8, V7X_LANES), lambda b, i: (0, 0))],
        out_shape=[
            jax.ShapeDtypeStruct((B, S, D), F32),
            jax.ShapeDtypeStruct((B, S * ROW_TILE, V7X_LANES), F32),
            jax.ShapeDtypeStruct((B, S, TOP_K), jnp.int32),
            jax.ShapeDtypeStruct((B, S, TOP_K), F32),
            jax.ShapeDtypeStruct((8, V7X_LANES), F32),
        ],
        compiler_params=_params("arbitrary", "arbitrary"),
        name="merge_router",
    )(x, attn, ylru, ga, gr, wa, wl, wo, g_moe, wr, br)


def _rank_kernel(idx_ref, ps_ref, dest_ref, carry_ref):
    @pl.when(pl.program_id(0) == 0)
    def _():
        carry_ref[...] = jnp.zeros_like(carry_ref)

    tm = idx_ref.shape[0]
    lane = lax.broadcasted_iota(jnp.int32, (tm, V7X_LANES), 1)
    picks = [lane == idx_ref[:, kk:kk + 1] for kk in range(TOP_K)]
    hits = sum(p.astype(F32) for p in picks)
    earlier = (lax.broadcasted_iota(jnp.int32, (tm, tm), 0) >
               lax.broadcasted_iota(jnp.int32, (tm, tm), 1)).astype(BF16)
    base = _bdot(earlier, hits.astype(BF16)) + carry_ref[...] + ps_ref[...]
    for kk in range(TOP_K):
        dest_ref[:, kk:kk + 1] = jnp.sum(jnp.where(picks[kk], base, 0.0), axis=-1,
                                         keepdims=True).astype(jnp.int32)
    carry_ref[...] += jnp.sum(hits, axis=0, keepdims=True)


def _rank(idx, pad_start_row):
    T = idx.shape[0]
    tm = RANK_TILE
    return pl.pallas_call(
        _rank_kernel,
        grid=(T // tm,),
        in_specs=[pl.BlockSpec((tm, TOP_K), lambda i: (i, 0)),
                  pl.BlockSpec((1, V7X_LANES), lambda i: (0, 0))],
        out_specs=pl.BlockSpec((tm, TOP_K), lambda i: (i, 0)),
        out_shape=jax.ShapeDtypeStruct((T, TOP_K), jnp.int32),
        scratch_shapes=[pltpu.VMEM((1, V7X_LANES), F32)],
        compiler_params=_params("arbitrary"),
        name="slot_rank",
    )(idx, pad_start_row)


def _invert_kernel(pad_end_ref, count_ref, dest_vmem, out_vmem, dest_ref, inv_ref, sem, *, n_tokens, chunk):
    step = pl.program_id(0)
    stage_in = pltpu.make_async_copy(dest_vmem, dest_ref, sem)
    stage_in.start()
    stage_in.wait()

    @pl.when(step == 0)
    def _():
        def fill(s, carry):
            inv_ref[s] = TOP_K * n_tokens + (s & (SPARE_ROWS - 1))
            return carry

        for e in range(N_EXPERTS):
            lo = pad_end_ref[e] - ((count_ref[e] + EXPERT_TILE - 1) & -EXPERT_TILE) + count_ref[e]
            lax.fori_loop(lo, pad_end_ref[e], fill, 0)
        lax.fori_loop(pad_end_ref[N_EXPERTS - 1], inv_ref.shape[0], fill, 0)

    group = 4 * TOP_K

    def body(g, carry):
        tok0 = step * (chunk // TOP_K) + g * (group // TOP_K)
        for u in range(group):
            inv_ref[dest_ref[g * group + u]] = tok0 + ((u % TOP_K) * n_tokens + u // TOP_K)
        return carry

    lax.fori_loop(0, chunk // group, body, 0)

    @pl.when(step == pl.num_programs(0) - 1)
    def _():
        stage_out = pltpu.make_async_copy(inv_ref, out_vmem, sem)
        stage_out.start()
        stage_out.wait()


def _invert(pad_end, counts, dest_flat, cap, n_tokens):
    chunk = INVERT_CHUNK
    return pl.pallas_call(
        functools.partial(_invert_kernel, n_tokens=n_tokens, chunk=chunk),
        grid_spec=pltpu.PrefetchScalarGridSpec(
            num_scalar_prefetch=2,
            grid=(dest_flat.shape[0] // chunk,),
            in_specs=[pl.BlockSpec((chunk,), lambda i, pe, ct: (i,))],
            out_specs=pl.BlockSpec((cap,), lambda i, pe, ct: (0,)),
            scratch_shapes=[pltpu.SMEM((chunk,), jnp.int32), pltpu.SMEM((cap,), jnp.int32),
                            pltpu.SemaphoreType.DMA(())],
        ),
        out_shape=jax.ShapeDtypeStruct((cap,), jnp.int32),
        compiler_params=_params("arbitrary"),
        name="slot_invert",
    )(pad_end, counts, dest_flat)


def _expert_kernel(be_ref, nu_ref, invp_ref, invc_ref, invn_ref, u_hbm, wgu_ref, bgu_ref, wdn_ref, bdn_ref,
                   y_hbm, xbuf0, xbuf1, ybuf0, ybuf1, gsem, ssem, *, n_tokens):
    xbuf, ybuf = (xbuf0, xbuf1), (ybuf0, ybuf1)
    j = pl.program_id(0)
    n_used = nu_ref[0]
    parity = j & 1

    def tile_rows(row):
        if isinstance(row, int):
            return pl.ds(row * ROW_TILE, ROW_TILE)
        return pl.ds(pl.multiple_of(row * ROW_TILE, ROW_TILE), ROW_TILE)

    def gather(inv_ref, r, buf, dep=0):
        tok = (inv_ref[0, 0, r] & (n_tokens - 1)) + dep
        return pltpu.make_async_copy(u_hbm.at[tile_rows(tok)], xbuf[buf].at[tile_rows(r)], gsem.at[buf])

    def scatter(inv_ref, r, buf, dep=0):
        return pltpu.make_async_copy(ybuf[buf].at[tile_rows(r)], y_hbm.at[tile_rows(inv_ref[0, 0, r] + dep)],
                                     ssem.at[buf])

    def wait_gather(buf):
        pltpu.make_async_copy(xbuf[buf], xbuf[buf], gsem.at[buf]).wait()

    def wait_scatter(buf):
        pltpu.make_async_copy(ybuf[buf], ybuf[buf], ssem.at[buf]).wait()

    @pl.when(j == 0)
    def _():
        def first(r, carry):
            gather(invc_ref, r, 0).start()
            return carry
        lax.fori_loop(0, EXPERT_TILE, first, 0)
        xbuf[1][...] = jnp.zeros(xbuf[1].shape, xbuf[1].dtype)
        ybuf[1][...] = jnp.zeros(ybuf[1].shape, ybuf[1].dtype)
        clears = [pltpu.make_async_copy(
            xbuf[1], y_hbm.at[pl.ds((TOP_K * n_tokens + c * EXPERT_TILE) * ROW_TILE, EXPERT_TILE * ROW_TILE)],
            ssem.at[1]) for c in range(SPARE_ROWS // EXPERT_TILE)]
        for c in clears:
            c.start()
        for c in clears:
            c.wait()

    def block(cur):
        oth = 1 - cur
        wait_gather(cur)

        @pl.when(j > 0)
        def _():
            wait_scatter(cur)

        x = _load_row_tiles(xbuf[cur]).astype(BF16)
        per = EXPERT_TILE // EXPERT_SLICES
        fc = D_EXPERT // EXPERT_SLICES
        acc = None
        dep = 0
        for c in range(EXPERT_SLICES):
            for r in range(c * per, (c + 1) * per):
                gather(invn_ref, r, oth, dep).start(priority=r % 2)
            for r in range(c * per, (c + 1) * per):
                scatter(invp_ref, r, oth, dep).start(priority=r % 2)
            gs = slice(c * fc, (c + 1) * fc)
            us = slice(D_EXPERT + c * fc, D_EXPERT + (c + 1) * fc)
            gate = jnp.minimum(_bdot(x, wgu_ref[:, gs].astype(BF16)) + bgu_ref[:, gs], SWIGLU_LIMIT)
            up = jnp.clip(_bdot(x, wgu_ref[:, us].astype(BF16)) + bgu_ref[:, us], -SWIGLU_LIMIT, SWIGLU_LIMIT)
            act = gate * _sigmoid(SWIGLU_ALPHA * gate)
            part = _bdot(((up + 1.0) * act).astype(BF16), wdn_ref[gs, :].astype(BF16))
            acc = part if acc is None else acc + part
            bits = pltpu.bitcast(part[0:8, 0:V7X_LANES], jnp.uint32)[0, 0]
            dep = lax.shift_right_logical(lax.shift_right_logical(bits, jnp.uint32(16)),
                                          jnp.uint32(16)).astype(jnp.int32)
        _store_row_tiles(ybuf[cur], acc + bdn_ref[...])

        @pl.when(j == n_used - 1)
        def _():
            wait_scatter(oth)
            wait_gather(oth)

            def last(r, carry):
                scatter(invc_ref, r, cur).start()
                return carry
            lax.fori_loop(0, EXPERT_TILE, last, 0)
            wait_scatter(cur)

    for cur in range(2):
        @pl.when((j < n_used) & (parity == cur))
        def _():
            block(cur)


def _experts(blk_exp, n_used, inv3, u, w_gu, b_gu, w_dn, b_dn):
    T, D = u.shape[0] // ROW_TILE, D_MODEL
    n_blocks = inv3.shape[0]
    exp3 = lambda j, be, nu: (be[j], 0, 0)
    inv_spec = lambda f: pl.BlockSpec((1, 1, EXPERT_TILE), lambda j, be, nu: (f(j, nu), 0, 0),
                                      memory_space=pltpu.SMEM)
    last = lambda j, nu: jnp.minimum(j, nu[0] - 1)
    return pl.pallas_call(
        functools.partial(_expert_kernel, n_tokens=T),
        grid_spec=pltpu.PrefetchScalarGridSpec(
            num_scalar_prefetch=2,
            grid=(n_blocks,),
            in_specs=[inv_spec(lambda j, nu: jnp.maximum(last(j, nu) - 1, 0)),
                      inv_spec(last),
                      inv_spec(lambda j, nu: jnp.minimum(j + 1, nu[0] - 1)),
                      pl.BlockSpec(memory_space=pl.ANY),
                      pl.BlockSpec((None, D, 2 * D_EXPERT), exp3),
                      pl.BlockSpec((None, 1, 2 * D_EXPERT), exp3),
                      pl.BlockSpec((None, D_EXPERT, D), exp3),
                      pl.BlockSpec((None, 1, D), exp3)],
            out_specs=pl.BlockSpec(memory_space=pl.ANY),
            scratch_shapes=[pltpu.VMEM((EXPERT_TILE * ROW_TILE, V7X_LANES), F32)] * 4 + [
                pltpu.SemaphoreType.DMA((2,)), pltpu.SemaphoreType.DMA((2,))],
        ),
        out_shape=jax.ShapeDtypeStruct(((TOP_K * T + SPARE_ROWS) * ROW_TILE, V7X_LANES), F32),
        compiler_params=_params("arbitrary"),
        name="experts",
    )(blk_exp, n_used, inv3, inv3, inv3, u, w_gu, b_gu, w_dn, b_dn)


def _combine_kernel(y0_ref, y1_ref, y2_ref, y3_ref, gate_ref, h_ref, p_ref, gp_ref, wg_ref, wp_ref, o_ref):
    h = h_ref[...]
    for kk, y_ref in enumerate((y0_ref, y1_ref, y2_ref, y3_ref)):
        h = h + gate_ref[:, kk:kk + 1] * _load_row_tiles(y_ref)
    u = _rms(h, gp_ref[...]).astype(BF16)
    o_ref[...] = h + _sigmoid(_bdot(u, wg_ref[...])) * _bdot(p_ref[...].astype(BF16), wp_ref[...])


def _combine(y4, gate, h, p, g_ple, wg, wp):
    T, D = h.shape
    tm = TOKEN_TILE
    full = lambda a: pl.BlockSpec(a.shape, lambda i: (0,) * a.ndim)
    plane = lambda kk: pl.BlockSpec((tm * ROW_TILE, V7X_LANES), lambda i: (kk * (T // tm) + i, 0))
    return pl.pallas_call(
        _combine_kernel,
        grid=(T // tm,),
        in_specs=[plane(0), plane(1), plane(2), plane(3),
                  pl.BlockSpec((tm, TOP_K), lambda i: (i, 0)),
                  pl.BlockSpec((tm, D), lambda i: (i, 0)),
                  pl.BlockSpec((tm, PLE_DIM), lambda i: (i, 0)),
                  full(g_ple), full(wg), full(wp)],
        out_specs=pl.BlockSpec((tm, D), lambda i: (i, 0)),
        out_shape=jax.ShapeDtypeStruct((T, D), F32),
        compiler_params=_params("arbitrary"),
        name="combine_ple",
    )(y4, y4, y4, y4, gate, h, p, g_ple, wg, wp)


def _rope_tables(seq_len):
    pos = np.arange(seq_len)
    axis_dim = HEAD_DIM // 2
    inv_freq = (ROPE_THETA ** (-np.arange(0, axis_dim, 2, dtype=np.float32) / axis_dim)).astype(np.float32)
    ang_r = (pos // GRID_W).astype(np.float32)[:, None] * inv_freq[None, :]
    ang_c = (pos % GRID_W).astype(np.float32)[:, None] * inv_freq[None, :]
    zero = np.zeros_like(ang_r)
    cos = np.concatenate([np.cos(ang_r)] * 2 + [np.cos(ang_c)] * 2, axis=-1)
    sina = np.concatenate([-np.sin(ang_r), zero, -np.sin(ang_c), zero], axis=-1)
    sinb = np.concatenate([zero, np.sin(ang_r), zero, np.sin(ang_c)], axis=-1)
    return jnp.asarray(cos, F32), jnp.asarray(sina, F32), jnp.asarray(sinb, F32)


def _layer(h, p, g_mix, w_in, q_norm, k_norm, conv_w, conv_b, lru_wa, lru_ba, lru_wi, lru_bi, lru_lam,
           w_attn_br, w_lru_br, w_out, g_moe, w_router, b_router, w_gu, b_gu, w_dn, b_dn,
           g_ple, w_ple_gate, w_ple_proj, rope):
    B, S, D = h.shape
    T = B * S
    assert S % ATTN_Q_TILE == 0 and S % TOKEN_TILE == 0 and S % LRU_STEPS == 0 and B % 8 == 0
    assert T % RANK_TILE == 0 and (T * TOP_K) % INVERT_CHUNK == 0
    row = lambda a: a.reshape(1, -1)
    q, k, v, xr, xg, ga, gr = _in_proj(
        h, row(g_mix), w_in.astype(BF16), row(q_norm) * (HEAD_DIM ** -0.5 * np.log2(np.e)), row(k_norm), *rope)
    attn = _attention(q, k, v)
    xr2 = xr.transpose(1, 0, 2).reshape(S * B, LRU_WIDTH)
    xg2 = xg.transpose(1, 0, 2).reshape(S * B, LRU_WIDTH)
    wcat = (0.5 * jnp.concatenate([lru_wa, lru_wi], axis=-1)).astype(BF16)
    bcat = 0.5 * jnp.concatenate([lru_ba.reshape(2, LRU_BLOCKS, 1, LRU_BLOCK_DIM),
                                  lru_bi.reshape(2, LRU_BLOCKS, 1, LRU_BLOCK_DIM)], axis=-1)
    h_fwd = _lru_scan(xr2, conv_w, row(conv_b), wcat[0], bcat[0], row(lru_lam[0]), batch=B, reverse=False)
    ylru = _lru_scan(xr2, conv_w, row(conv_b), wcat[1], bcat[1], row(lru_lam[1]), batch=B, reverse=True,
                     h_fwd=h_fwd, xg2=xg2).reshape(S, B, LRU_WIDTH).transpose(1, 0, 2)
    w_router_f = jnp.zeros((D, V7X_LANES), F32).at[:, :N_EXPERTS].set(w_router)
    w_router_hi = w_router_f.astype(BF16)
    w_router_p = jnp.stack([w_router_hi, (w_router_f - w_router_hi.astype(F32)).astype(BF16)])
    b_router_p = jnp.zeros((1, V7X_LANES), F32).at[0, :N_EXPERTS].set(b_router)
    h1, u2, idx, gate, cnt = _merge(h, attn, ylru, ga, gr, w_attn_br.astype(BF16), w_lru_br.astype(BF16),
                                    w_out.astype(BF16), row(g_moe), w_router_p, b_router_p)
    cap = T * TOP_K + N_EXPERTS * EXPERT_TILE
    counts = cnt[0, :N_EXPERTS].astype(jnp.int32)
    padded = ((counts + EXPERT_TILE - 1) // EXPERT_TILE) * EXPERT_TILE
    pad_end = jnp.cumsum(padded)
    pad_start = pad_end - padded
    blk_start = jnp.arange(cap // EXPERT_TILE, dtype=jnp.int32) * EXPERT_TILE
    blk_exp = jnp.minimum(jnp.sum(blk_start[:, None] >= pad_end[None, :], axis=1), N_EXPERTS - 1)
    n_used = (pad_end[-1:] // EXPERT_TILE).astype(jnp.int32)
    ps_row = jnp.zeros((1, V7X_LANES), F32).at[0, :N_EXPERTS].set(pad_start.astype(F32))
    dest = _rank(idx.reshape(T, TOP_K), ps_row).reshape(T * TOP_K)
    assert T & (T - 1) == 0, "token count must be a power of two for the slot encoding"
    inv3 = _invert(pad_end.astype(jnp.int32), counts, dest, cap, T).reshape(cap // EXPERT_TILE, 1, EXPERT_TILE)
    y4 = _experts(blk_exp.astype(jnp.int32), n_used, inv3, u2.reshape(T * ROW_TILE, V7X_LANES), w_gu,
                  b_gu.reshape(N_EXPERTS, 1, -1), w_dn, b_dn.reshape(N_EXPERTS, 1, -1))
    out = _combine(y4, gate.reshape(T, TOP_K), h1.reshape(T, D),
                   p.reshape(T, PLE_DIM), row(g_ple), w_ple_gate.astype(BF16), w_ple_proj.astype(BF16))
    return out.reshape(B, S, D)


def kernel(x, p, g_mix, w_in, q_norm, k_norm, conv_w, conv_b, lru_wa, lru_ba, lru_wi, lru_bi, lru_lam,
           w_attn_br, w_lru_br, w_out, g_moe, w_router, b_router, w_gu, b_gu, w_dn, b_dn,
           g_ple, w_ple_gate, w_ple_proj):
    rope = _rope_tables(x.shape[1])
    h = x
    for l in range(p.shape[0]):
        h = _layer(h, p[l], g_mix[l], w_in[l], q_norm[l], k_norm[l], conv_w[l], conv_b[l], lru_wa[l],
                   lru_ba[l], lru_wi[l], lru_bi[l], lru_lam[l], w_attn_br[l], w_lru_br[l], w_out[l],
                   g_moe[l], w_router[l], b_router[l], w_gu[l], b_gu[l], w_dn[l], b_dn[l],
                   g_ple[l], w_ple_gate[l], w_ple_proj[l], rope)
    return h
```

```python
import functools

import jax
import jax.numpy as jnp
import numpy as np
from jax import lax
from jax.experimental import pallas as pl
from jax.experimental.pallas import tpu as pltpu

D_MODEL = 1024
N_Q_HEADS = 8
N_KV_HEADS = 2
HEAD_DIM = 128
Q_GROUP = N_Q_HEADS // N_KV_HEADS
ROPE_THETA = 10000.0
GRID_W = 64
LRU_WIDTH = 1024
LRU_BLOCKS = 8
LRU_BLOCK_DIM = LRU_WIDTH // LRU_BLOCKS
LRU_C = 8.0
N_EXPERTS = 32
TOP_K = 4
D_EXPERT = 1024
SWIGLU_LIMIT = 7.0
SWIGLU_ALPHA = 1.702
PLE_DIM = 256
EPS = 1e-6
Q_W = N_Q_HEADS * HEAD_DIM
KV_W = N_KV_HEADS * HEAD_DIM
IN_WIDTH = Q_W + 2 * KV_W + 2 * LRU_WIDTH + 2 * D_MODEL
OFF_Q = 0
OFF_K = OFF_Q + Q_W
OFF_V = OFF_K + KV_W
OFF_XR = OFF_V + KV_W
OFF_XG = OFF_XR + LRU_WIDTH
OFF_GA = OFF_XG + LRU_WIDTH
OFF_GR = OFF_GA + D_MODEL

V7X_LANES = 128
V7X_VMEM_LIMIT_BYTES = 56 * 1024 * 1024
TOKEN_TILE = 512
ATTN_Q_TILE = 1024
LRU_STEPS = 32
EXPERT_TILE = 512
EXPERT_SLICES = 2
RANK_TILE = 512
INVERT_CHUNK = 8192
SPARE_ROWS = 1024

F32 = jnp.float32
BF16 = jnp.bfloat16


def _params(*sem):
    return pltpu.CompilerParams(dimension_semantics=sem, vmem_limit_bytes=V7X_VMEM_LIMIT_BYTES)


def _bdot(a, b):
    return jnp.dot(a, b, preferred_element_type=F32)


def _rms(x, g):
    return x * lax.rsqrt(jnp.mean(x * x, axis=-1, keepdims=True) + EPS) * g


ROW_TILE = 8


def _store_row_tiles(ref, x):
    n = x.shape[0]
    for j in range(ROW_TILE):
        ref[pl.ds(j, n, stride=ROW_TILE), :] = x[:, j * V7X_LANES:(j + 1) * V7X_LANES]


def _load_row_tiles(ref):
    n = ref.shape[0] // ROW_TILE
    return jnp.concatenate([ref[pl.ds(j, n, stride=ROW_TILE), :] for j in range(ROW_TILE)], axis=-1)


def _sigmoid(x):
    return 0.5 * jnp.tanh(0.5 * x) + 0.5


def _in_proj_kernel(x_ref, g_ref, w_ref, qn_ref, kn_ref, cos_ref, sina_ref, sinb_ref,
                    q_ref, k_ref, v_ref, xr_ref, xg_ref, ga_ref, gr_ref):
    u = _rms(x_ref[...], g_ref[...]).astype(BF16)
    cos, sina, sinb = cos_ref[...], sina_ref[...], sinb_ref[...]

    def head(z, gain):
        y = _rms(z, gain)
        return y * cos + pltpu.roll(y, 96, 1) * sina + pltpu.roll(y, 32, 1) * sinb

    zq = _bdot(u, w_ref[:, OFF_Q:OFF_Q + Q_W])
    for h in range(N_Q_HEADS):
        sl = slice(h * HEAD_DIM, (h + 1) * HEAD_DIM)
        q_ref[:, sl] = head(zq[:, sl], qn_ref[...]).astype(BF16)
    zk = _bdot(u, w_ref[:, OFF_K:OFF_K + KV_W])
    for h in range(N_KV_HEADS):
        sl = slice(h * HEAD_DIM, (h + 1) * HEAD_DIM)
        k_ref[:, sl] = head(zk[:, sl], kn_ref[...]).astype(BF16)
    v_ref[...] = _bdot(u, w_ref[:, OFF_V:OFF_V + KV_W]).astype(BF16)
    xr_ref[...] = _bdot(u, w_ref[:, OFF_XR:OFF_XR + LRU_WIDTH])
    xg_ref[...] = _bdot(u, w_ref[:, OFF_XG:OFF_XG + LRU_WIDTH])
    ga_ref[...] = _bdot(u, w_ref[:, OFF_GA:OFF_GA + D_MODEL])
    gr_ref[...] = _bdot(u, w_ref[:, OFF_GR:OFF_GR + D_MODEL])


def _in_proj(x, g_mix, w_in, qn, kn, cos, sina, sinb):
    B, S, D = x.shape
    tm = TOKEN_TILE
    bm = lambda w: pl.BlockSpec((None, tm, w), lambda b, i: (b, i, 0))
    full = lambda a: pl.BlockSpec(a.shape, lambda b, i: (0,) * a.ndim)
    tab = pl.BlockSpec((tm, HEAD_DIM), lambda b, i: (i, 0))
    return pl.pallas_call(
        _in_proj_kernel,
        grid=(B, S // tm),
        in_specs=[bm(D), full(g_mix), full(w_in), full(qn), full(kn), tab, tab, tab],
        out_specs=[bm(Q_W), bm(KV_W), bm(KV_W), bm(LRU_WIDTH), bm(LRU_WIDTH), bm(D), bm(D)],
        out_shape=[
            jax.ShapeDtypeStruct((B, S, Q_W), BF16),
            jax.ShapeDtypeStruct((B, S, KV_W), BF16),
            jax.ShapeDtypeStruct((B, S, KV_W), BF16),
            jax.ShapeDtypeStruct((B, S, LRU_WIDTH), F32),
            jax.ShapeDtypeStruct((B, S, LRU_WIDTH), F32),
            jax.ShapeDtypeStruct((B, S, D), F32),
            jax.ShapeDtypeStruct((B, S, D), F32),
        ],
        compiler_params=_params("arbitrary", "arbitrary"),
        name="in_proj",
    )(x, g_mix, w_in, qn, kn, cos, sina, sinb)


def _attn_kernel(q_ref, k_ref, v_ref, o_ref):
    for g in range(N_KV_HEADS):
        ks = slice(g * HEAD_DIM, (g + 1) * HEAD_DIM)
        k = k_ref[:, ks]
        v1 = jnp.concatenate([v_ref[:, ks], jnp.ones((v_ref.shape[0], HEAD_DIM), v_ref.dtype)], axis=-1)
        for h in range(g * Q_GROUP, (g + 1) * Q_GROUP):
            sl = slice(h * HEAD_DIM, (h + 1) * HEAD_DIM)
            s = lax.dot_general(q_ref[:, sl], k, (((1,), (1,)), ((), ())), preferred_element_type=F32)
            p = jnp.exp2(s - jnp.max(s, axis=-1, keepdims=True))
            o = _bdot(p.astype(BF16), v1)
            o_ref[:, sl] = (o[:, :HEAD_DIM] * (1.0 / o[:, HEAD_DIM:HEAD_DIM + 1])).astype(BF16)


def _attention(q, k, v):
    B, S, _ = q.shape
    tq = ATTN_Q_TILE
    return pl.pallas_call(
        _attn_kernel,
        grid=(B, S // tq),
        in_specs=[
            pl.BlockSpec((None, tq, Q_W), lambda b, i: (b, i, 0)),
            pl.BlockSpec((None, S, KV_W), lambda b, i: (b, 0, 0)),
            pl.BlockSpec((None, S, KV_W), lambda b, i: (b, 0, 0)),
        ],
        out_specs=pl.BlockSpec((None, tq, Q_W), lambda b, i: (b, i, 0)),
        out_shape=jax.ShapeDtypeStruct((B, S, Q_W), BF16),
        compiler_params=_params("arbitrary", "arbitrary"),
        name="attention",
    )(q, k, v)


def _gelu_tanh(x):
    return 0.5 * x * (1.0 + jnp.tanh(np.sqrt(2.0 / np.pi) * (x + 0.044715 * (x * x * x))))


def _lru_kernel(*refs, batch, reverse):
    if reverse:
        c_ref, w_ref, b_ref, lam_ref, hf_ref, xg_ref, o_ref, a_ref, bt_ref, h_ref = refs
    else:
        (xc_ref, xp_ref, xn_ref, cw_ref, cb_ref, w_ref, b_ref, lam_ref,
         o_ref, c_ref, ext_ref, a_ref, bt_ref, h_ref) = refs
    step = pl.program_id(0)
    nsteps = pl.num_programs(0)
    chunk = (nsteps - 1 - step) if reverse else step
    rows = LRU_STEPS * batch

    @pl.when(step == 0)
    def _():
        h_ref[...] = jnp.zeros_like(h_ref)

    if reverse:
        c = c_ref[...]
    else:
        ext_ref[0:2 * batch, :] = jnp.where(chunk > 0, xp_ref[...], 0.0)
        ext_ref[2 * batch:2 * batch + rows, :] = xc_ref[...]
        ext_ref[2 * batch + rows:3 * batch + rows, :] = jnp.where(chunk < nsteps - 1, xn_ref[...], 0.0)
        c = cb_ref[...] + sum(cw_ref[j:j + 1, :] * ext_ref[j * batch:j * batch + rows, :] for j in range(4))
        c_ref[...] = c
    cb16 = c.astype(BF16)
    lam = lam_ref[...]
    half_scale = (-0.5 * LRU_C) * (jnp.maximum(-lam, 0.0) + jnp.log1p(jnp.exp(-jnp.abs(lam))))
    for n in range(LRU_BLOCKS):
        sl = slice(n * LRU_BLOCK_DIM, (n + 1) * LRU_BLOCK_DIM)
        th = jnp.tanh(_bdot(cb16[:, sl], w_ref[n]) + b_ref[n])
        ig = 0.5 * th[:, LRU_BLOCK_DIM:] + 0.5
        log_a = half_scale[:, sl] * th[:, :LRU_BLOCK_DIM] + half_scale[:, sl]
        a = jnp.exp(log_a)
        a_ref[:, sl] = a
        one_minus_a2 = -jnp.tanh(log_a) * (a * a + 1.0)
        root = jnp.where(one_minus_a2 > 0.0, one_minus_a2 * lax.rsqrt(one_minus_a2), 0.0)
        bt_ref[:, sl] = root * ig * c[:, sl]

    def scan_step(t, h):
        tt = (LRU_STEPS - 1 - t) if reverse else t
        rs = pl.ds(pl.multiple_of(tt * batch, batch), batch)
        h = a_ref[rs, :] * h + bt_ref[rs, :]
        if reverse:
            o_ref[rs, :] = ((h + hf_ref[rs, :]) * _gelu_tanh(xg_ref[rs, :])).astype(o_ref.dtype)
        else:
            o_ref[rs, :] = h
        return h

    h_ref[...] = lax.fori_loop(0, LRU_STEPS, scan_step, h_ref[...], unroll=4)


def _lru_scan(xr2, conv_w, conv_b, wcat, bcat, lam, *, batch, reverse, h_fwd=None, xg2=None):
    n_rows, width = xr2.shape
    rows = LRU_STEPS * batch
    n = n_rows // rows
    ch = (lambda i: n - 1 - i) if reverse else (lambda i: i)
    tile = pl.BlockSpec((rows, width), lambda i: (ch(i), 0))
    prev = pl.BlockSpec((2 * batch, width), lambda i: (jnp.maximum(ch(i) * (LRU_STEPS // 2) - 1, 0), 0))
    nxt = pl.BlockSpec((batch, width),
                       lambda i: (jnp.minimum((ch(i) + 1) * LRU_STEPS, n * LRU_STEPS - 1), 0))
    full = lambda a: pl.BlockSpec(a.shape, lambda i: (0,) * a.ndim)
    state = jax.ShapeDtypeStruct((n_rows, width), F32)
    if reverse:
        ins = [xr2, wcat, bcat, lam, h_fwd, xg2]
        specs = [tile, full(wcat), full(bcat), full(lam), tile, tile]
        out_specs, out_shape, halo = tile, jax.ShapeDtypeStruct((n_rows, width), BF16), []
    else:
        ins = [xr2, xr2, xr2, conv_w, conv_b, wcat, bcat, lam]
        specs = [tile, prev, nxt, full(conv_w), full(conv_b), full(wcat), full(bcat), full(lam)]
        out_specs, out_shape, halo = [tile, tile], [state, state], [pltpu.VMEM((rows + 3 * batch, width), F32)]
    return pl.pallas_call(
        functools.partial(_lru_kernel, batch=batch, reverse=reverse),
        grid=(n,),
        in_specs=specs,
        out_specs=out_specs,
        out_shape=out_shape,
        scratch_shapes=halo + [
            pltpu.VMEM((rows, width), F32),
            pltpu.VMEM((rows, width), F32),
            pltpu.VMEM((batch, width), F32),
        ],
        compiler_params=_params("arbitrary"),
        name="lru_bwd" if reverse else "lru_fwd",
    )(*ins)


def _merge_kernel(x_ref, at_ref, yl_ref, ga_ref, gr_ref, wa_ref, wl_ref, wo_ref, gm_ref, wr_ref, br_ref,
                  h_ref, u_ref, idx_ref, gate_ref, cnt_ref):
    y_attn = _bdot(at_ref[...], wa_ref[...])
    y_lru = _bdot(yl_ref[...], wl_ref[...])
    merged = _sigmoid(ga_ref[...]) * y_attn + _sigmoid(gr_ref[...]) * y_lru
    h = x_ref[...] + _bdot(merged.astype(BF16), wo_ref[...])
    h_ref[...] = h
    u = _rms(h, gm_ref[...])
    _store_row_tiles(u_ref, u)
    u_hi = u.astype(BF16)
    u_lo = (u - u_hi.astype(F32)).astype(BF16)
    logits = _bdot(u_hi, wr_ref[0]) + (_bdot(u_lo, wr_ref[0]) + _bdot(u_hi, wr_ref[1]))
    lane = lax.broadcasted_iota(jnp.int32, logits.shape, 1)
    work = jnp.where(lane < N_EXPERTS, logits + br_ref[...], -jnp.inf)
    vals, hits = [], jnp.zeros(logits.shape, F32)
    for kk in range(TOP_K):
        m = jnp.max(work, axis=-1, keepdims=True)
        idx = jnp.min(jnp.where(work == m, lane, V7X_LANES), axis=-1, keepdims=True)
        pick = lane == idx
        work = jnp.where(pick, -jnp.inf, work)
        hits = hits + pick.astype(F32)
        vals.append(m)
        idx_ref[:, kk:kk + 1] = idx
    es = [jnp.exp(vv - vals[0]) for vv in vals]
    inv = 1.0 / sum(es)
    for kk in range(TOP_K):
        gate_ref[:, kk:kk + 1] = es[kk] * inv

    @pl.when((pl.program_id(0) == 0) & (pl.program_id(1) == 0))
    def _():
        cnt_ref[...] = jnp.zeros_like(cnt_ref)

    cnt_ref[0:1, :] += jnp.sum(hits, axis=0, keepdims=True)


def _merge(x, attn, ylru, ga, gr, wa, wl, wo, g_moe, wr, br):
    B, S, D = x.shape
    tm = TOKEN_TILE
    bm = lambda w: pl.BlockSpec((None, tm, w), lambda b, i: (b, i, 0))
    full = lambda a: pl.BlockSpec(a.shape, lambda b, i: (0,) * a.ndim)
    return pl.pallas_call(
        _merge_kernel,
        grid=(B, S // tm),
        in_specs=[bm(D), bm(Q_W), bm(LRU_WIDTH), bm(D), bm(D), full(wa), full(wl), full(wo),
                  full(g_moe), full(wr), full(br)],
        out_specs=[bm(D), pl.BlockSpec((None, tm * ROW_TILE, V7X_LANES), lambda b, i: (b, i, 0)),
                   bm(TOP_K), bm(TOP_K), pl.BlockSpec((8, V7X_LANES), lambda b, i: (0, 0))],
        out_shape=[
            jax.ShapeDtypeStruct((B, S, D), F32),
            jax.ShapeDtypeStruct((B, S * ROW_TILE, V7X_LANES), F32),
            jax.ShapeDtypeStruct((B, S, TOP_K), jnp.int32),
            jax.ShapeDtypeStruct((B, S, TOP_K), F32),
            jax.ShapeDtypeStruct((8, V7X_LANES), F32),
        ],
        compiler_params=_params("arbitrary", "arbitrary"),
        name="merge_router",
    )(x, attn, ylru, ga, gr, wa, wl, wo, g_moe, wr, br)


def _rank_kernel(idx_ref, ps_ref, dest_ref, carry_ref):
    @pl.when(pl.program_id(0) == 0)
    def _():
        carry_ref[...] = jnp.zeros_like(carry_ref)

    tm = idx_ref.shape[0]
    lane = lax.broadcasted_iota(jnp.int32, (tm, V7X_LANES), 1)
    picks = [lane == idx_ref[:, kk:kk + 1] for kk in range(TOP_K)]
    hits = sum(p.astype(F32) for p in picks)
    earlier = (lax.broadcasted_iota(jnp.int32, (tm, tm), 0) >
               lax.broadcasted_iota(jnp.int32, (tm, tm), 1)).astype(BF16)
    base = _bdot(earlier, hits.astype(BF16)) + carry_ref[...] + ps_ref[...]
    for kk in range(TOP_K):
        dest_ref[:, kk:kk + 1] = jnp.sum(jnp.where(picks[kk], base, 0.0), axis=-1,
                                         keepdims=True).astype(jnp.int32)
    carry_ref[...] += jnp.sum(hits, axis=0, keepdims=True)


def _rank(idx, pad_start_row):
    T = idx.shape[0]
    tm = RANK_TILE
    return pl.pallas_call(
        _rank_kernel,
        grid=(T // tm,),
        in_specs=[pl.BlockSpec((tm, TOP_K), lambda i: (i, 0)),
                  pl.BlockSpec((1, V7X_LANES), lambda i: (0, 0))],
        out_specs=pl.BlockSpec((tm, TOP_K), lambda i: (i, 0)),
        out_shape=jax.ShapeDtypeStruct((T, TOP_K), jnp.int32),
        scratch_shapes=[pltpu.VMEM((1, V7X_LANES), F32)],
        compiler_params=_params("arbitrary"),
        name="slot_rank",
    )(idx, pad_start_row)


def _invert_kernel(pad_end_ref, count_ref, dest_vmem, out_vmem, dest_ref, inv_ref, sem, *, n_tokens, chunk):
    step = pl.program_id(0)
    stage_in = pltpu.make_async_copy(dest_vmem, dest_ref, sem)
    stage_in.start()
    stage_in.wait()

    @pl.when(step == 0)
    def _():
        def fill(s, carry):
            inv_ref[s] = TOP_K * n_tokens + (s & (SPARE_ROWS - 1))
            return carry

        for e in range(N_EXPERTS):
            lo = pad_end_ref[e] - ((count_ref[e] + EXPERT_TILE - 1) & -EXPERT_TILE) + count_ref[e]
            lax.fori_loop(lo, pad_end_ref[e], fill, 0)
        lax.fori_loop(pad_end_ref[N_EXPERTS - 1], inv_ref.shape[0], fill, 0)

    group = 4 * TOP_K

    def body(g, carry):
        tok0 = step * (chunk // TOP_K) + g * (group // TOP_K)
        for u in range(group):
            inv_ref[dest_ref[g * group + u]] = tok0 + ((u % TOP_K) * n_tokens + u // TOP_K)
        return carry

    lax.fori_loop(0, chunk // group, body, 0)

    @pl.when(step == pl.num_programs(0) - 1)
    def _():
        stage_out = pltpu.make_async_copy(inv_ref, out_vmem, sem)
        stage_out.start()
        stage_out.wait()


def _invert(pad_end, counts, dest_flat, cap, n_tokens):
    chunk = INVERT_CHUNK
    return pl.pallas_call(
        functools.partial(_invert_kernel, n_tokens=n_tokens, chunk=chunk),
        grid_spec=pltpu.PrefetchScalarGridSpec(
            num_scalar_prefetch=2,
            grid=(dest_flat.shape[0] // chunk,),
            in_specs=[pl.BlockSpec((chunk,), lambda i, pe, ct: (i,))],
            out_specs=pl.BlockSpec((cap,), lambda i, pe, ct: (0,)),
            scratch_shapes=[pltpu.SMEM((chunk,), jnp.int32), pltpu.SMEM((cap,), jnp.int32),
                            pltpu.SemaphoreType.DMA(())],
        ),
        out_shape=jax.ShapeDtypeStruct((cap,), jnp.int32),
        compiler_params=_params("arbitrary"),
        name="slot_invert",
    )(pad_end, counts, dest_flat)


def _expert_kernel(be_ref, nu_ref, invp_ref, invc_ref, invn_ref, u_hbm, wgu_ref, bgu_ref, wdn_ref, bdn_ref,
                   y_hbm, xbuf0, xbuf1, ybuf0, ybuf1, gsem, ssem, *, n_tokens):
    xbuf, ybuf = (xbuf0, xbuf1), (ybuf0, ybuf1)
    j = pl.program_id(0)
    n_used = nu_ref[0]
    parity = j & 1

    def tile_rows(row):
        if isinstance(row, int):
            return pl.ds(row * ROW_TILE, ROW_TILE)
        return pl.ds(pl.multiple_of(row * ROW_TILE, ROW_TILE), ROW_TILE)

    def gather(inv_ref, r, buf, dep=0):
        tok = (inv_ref[0, 0, r] & (n_tokens - 1)) + dep
        return pltpu.make_async_copy(u_hbm.at[tile_rows(tok)], xbuf[buf].at[tile_rows(r)], gsem.at[buf])

    def scatter(inv_ref, r, buf, dep=0):
        return pltpu.make_async_copy(ybuf[buf].at[tile_rows(r)], y_hbm.at[tile_rows(inv_ref[0, 0, r] + dep)],
                                     ssem.at[buf])

    def wait_gather(buf):
        pltpu.make_async_copy(xbuf[buf], xbuf[buf], gsem.at[buf]).wait()

    def wait_scatter(buf):
        pltpu.make_async_copy(ybuf[buf], ybuf[buf], ssem.at[buf]).wait()

    @pl.when(j == 0)
    def _():
        def first(r, carry):
            gather(invc_ref, r, 0).start()
            return carry
        lax.fori_loop(0, EXPERT_TILE, first, 0)
        xbuf[1][...] = jnp.zeros(xbuf[1].shape, xbuf[1].dtype)
        ybuf[1][...] = jnp.zeros(ybuf[1].shape, ybuf[1].dtype)
        clears = [pltpu.make_async_copy(
            xbuf[1], y_hbm.at[pl.ds((TOP_K * n_tokens + c * EXPERT_TILE) * ROW_TILE, EXPERT_TILE * ROW_TILE)],
            ssem.at[1]) for c in range(SPARE_ROWS // EXPERT_TILE)]
        for c in clears:
            c.start()
        for c in clears:
            c.wait()

    def block(cur):
        oth = 1 - cur
        wait_gather(cur)

        @pl.when(j > 0)
        def _():
            wait_scatter(cur)

        x = _load_row_tiles(xbuf[cur]).astype(BF16)
        per = EXPERT_TILE // EXPERT_SLICES
        fc = D_EXPERT // EXPERT_SLICES
        acc = None
        dep = 0
        for c in range(EXPERT_SLICES):
            for r in range(c * per, (c + 1) * per):
                gather(invn_ref, r, oth, dep).start(priority=r % 2)
            for r in range(c * per, (c + 1) * per):
                scatter(invp_ref, r, oth, dep).start(priority=r % 2)
            gs = slice(c * fc, (c + 1) * fc)
            us = slice(D_EXPERT + c * fc, D_EXPERT + (c + 1) * fc)
            gate = jnp.minimum(_bdot(x, wgu_ref[:, gs].astype(BF16)) + bgu_ref[:, gs], SWIGLU_LIMIT)
            up = jnp.clip(_bdot(x, wgu_ref[:, us].astype(BF16)) + bgu_ref[:, us], -SWIGLU_LIMIT, SWIGLU_LIMIT)
            act = gate * _sigmoid(SWIGLU_ALPHA * gate)
            part = _bdot(((up + 1.0) * act).astype(BF16), wdn_ref[gs, :].astype(BF16))
            acc = part if acc is None else acc + part
            bits = pltpu.bitcast(part[0:8, 0:V7X_LANES], jnp.uint32)[0, 0]
            dep = lax.shift_right_logical(lax.shift_right_logical(bits, jnp.uint32(16)),
                                          jnp.uint32(16)).astype(jnp.int32)
        _store_row_tiles(ybuf[cur], acc + bdn_ref[...])

        @pl.when(j == n_used - 1)
        def _():
            wait_scatter(oth)
            wait_gather(oth)

            def last(r, carry):
                scatter(invc_ref, r, cur).start()
                return carry
            lax.fori_loop(0, EXPERT_TILE, last, 0)
            wait_scatter(cur)

    for cur in range(2):
        @pl.when((j < n_used) & (parity == cur))
        def _():
            block(cur)


def _experts(blk_exp, n_used, inv3, u, w_gu, b_gu, w_dn, b_dn):
    T, D = u.shape[0] // ROW_TILE, D_MODEL
    n_blocks = inv3.shape[0]
    exp3 = lambda j, be, nu: (be[j], 0, 0)
    inv_spec = lambda f: pl.BlockSpec((1, 1, EXPERT_TILE), lambda j, be, nu: (f(j, nu), 0, 0),
                                      memory_space=pltpu.SMEM)
    last = lambda j, nu: jnp.minimum(j, nu[0] - 1)
    return pl.pallas_call(
        functools.partial(_expert_kernel, n_tokens=T),
        grid_spec=pltpu.PrefetchScalarGridSpec(
            num_scalar_prefetch=2,
            grid=(n_blocks,),
            in_specs=[inv_spec(lambda j, nu: jnp.maximum(last(j, nu) - 1, 0)),
                      inv_spec(last),
                      inv_spec(lambda j, nu: jnp.minimum(j + 1, nu[0] - 1)),
                      pl.BlockSpec(memory_space=pl.ANY),
                      pl.BlockSpec((None, D, 2 * D_EXPERT), exp3),
                      pl.BlockSpec((None, 1, 2 * D_EXPERT), exp3),
                      pl.BlockSpec((None, D_EXPERT, D), exp3),
                      pl.BlockSpec((None, 1, D), exp3)],
            out_specs=pl.BlockSpec(memory_space=pl.ANY),
            scratch_shapes=[pltpu.VMEM((EXPERT_TILE * ROW_TILE, V7X_LANES), F32)] * 4 + [
                pltpu.SemaphoreType.DMA((2,)), pltpu.SemaphoreType.DMA((2,))],
        ),
        out_shape=jax.ShapeDtypeStruct(((TOP_K * T + SPARE_ROWS) * ROW_TILE, V7X_LANES), F32),
        compiler_params=_params("arbitrary"),
        name="experts",
    )(blk_exp, n_used, inv3, inv3, inv3, u, w_gu, b_gu, w_dn, b_dn)


def _combine_kernel(y0_ref, y1_ref, y2_ref, y3_ref, gate_ref, h_ref, p_ref, gp_ref, wg_ref, wp_ref, o_ref):
    h = h_ref[...]
    for kk, y_ref in enumerate((y0_ref, y1_ref, y2_ref, y3_ref)):
        h = h + gate_ref[:, kk:kk + 1] * _load_row_tiles(y_ref)
    u = _rms(h, gp_ref[...]).astype(BF16)
    o_ref[...] = h + _sigmoid(_bdot(u, wg_ref[...])) * _bdot(p_ref[...].astype(BF16), wp_ref[...])


def _combine(y4, gate, h, p, g_ple, wg, wp):
    T, D = h.shape
    tm = TOKEN_TILE
    full = lambda a: pl.BlockSpec(a.shape, lambda i: (0,) * a.ndim)
    plane = lambda kk: pl.BlockSpec((tm * ROW_TILE, V7X_LANES), lambda i: (kk * (T // tm) + i, 0))
    return pl.pallas_call(
        _combine_kernel,
        grid=(T // tm,),
        in_specs=[plane(0), plane(1), plane(2), plane(3),
                  pl.BlockSpec((tm, TOP_K), lambda i: (i, 0)),
                  pl.BlockSpec((tm, D), lambda i: (i, 0)),
                  pl.BlockSpec((tm, PLE_DIM), lambda i: (i, 0)),
                  full(g_ple), full(wg), full(wp)],
        out_specs=pl.BlockSpec((tm, D), lambda i: (i, 0)),
        out_shape=jax.ShapeDtypeStruct((T, D), F32),
        compiler_params=_params("arbitrary"),
        name="combine_ple",
    )(y4, y4, y4, y4, gate, h, p, g_ple, wg, wp)


def _rope_tables(seq_len):
    pos = np.arange(seq_len)
    axis_dim = HEAD_DIM // 2
    inv_freq = (ROPE_THETA ** (-np.arange(0, axis_dim, 2, dtype=np.float32) / axis_dim)).astype(np.float32)
    ang_r = (pos // GRID_W).astype(np.float32)[:, None] * inv_freq[None, :]
    ang_c = (pos % GRID_W).astype(np.float32)[:, None] * inv_freq[None, :]
    zero = np.zeros_like(ang_r)
    cos = np.concatenate([np.cos(ang_r)] * 2 + [np.cos(ang_c)] * 2, axis=-1)
    sina = np.concatenate([-np.sin(ang_r), zero, -np.sin(ang_c), zero], axis=-1)
    sinb = np.concatenate([zero, np.sin(ang_r), zero, np.sin(ang_c)], axis=-1)
    return jnp.asarray(cos, F32), jnp.asarray(sina, F32), jnp.asarray(sinb, F32)


def _layer(h, p, g_mix, w_in, q_norm, k_norm, conv_w, conv_b, lru_wa, lru_ba, lru_wi, lru_bi, lru_lam,
           w_attn_br, w_lru_br, w_out, g_moe, w_router, b_router, w_gu, b_gu, w_dn, b_dn,
           g_ple, w_ple_gate, w_ple_proj, rope):
    B, S, D = h.shape
    T = B * S
    assert S % ATTN_Q_TILE == 0 and S % TOKEN_TILE == 0 and S % LRU_STEPS == 0 and B % 8 == 0
    assert T % RANK_TILE == 0 and (T * TOP_K) % INVERT_CHUNK == 0
    row = lambda a: a.reshape(1, -1)
    q, k, v, xr, xg, ga, gr = _in_proj(
        h, row(g_mix), w_in.astype(BF16), row(q_norm) * (HEAD_DIM ** -0.5 * np.log2(np.e)), row(k_norm), *rope)
    attn = _attention(q, k, v)
    xr2 = xr.transpose(1, 0, 2).reshape(S * B, LRU_WIDTH)
    xg2 = xg.transpose(1, 0, 2).reshape(S * B, LRU_WIDTH)
    wcat = (0.5 * jnp.concatenate([lru_wa, lru_wi], axis=-1)).astype(BF16)
    bcat = 0.5 * jnp.concatenate([lru_ba.reshape(2, LRU_BLOCKS, 1, LRU_BLOCK_DIM),
                                  lru_bi.reshape(2, LRU_BLOCKS, 1, LRU_BLOCK_DIM)], axis=-1)
    h_fwd, conv_out = _lru_scan(xr2, conv_w, row(conv_b), wcat[0], bcat[0], row(lru_lam[0]), batch=B,
                                reverse=False)
    ylru = _lru_scan(conv_out, None, None, wcat[1], bcat[1], row(lru_lam[1]), batch=B, reverse=True,
                     h_fwd=h_fwd, xg2=xg2).reshape(S, B, LRU_WIDTH).transpose(1, 0, 2)
    w_router_f = jnp.zeros((D, V7X_LANES), F32).at[:, :N_EXPERTS].set(w_router)
    w_router_hi = w_router_f.astype(BF16)
    w_router_p = jnp.stack([w_router_hi, (w_router_f - w_router_hi.astype(F32)).astype(BF16)])
    b_router_p = jnp.zeros((1, V7X_LANES), F32).at[0, :N_EXPERTS].set(b_router)
    h1, u2, idx, gate, cnt = _merge(h, attn, ylru, ga, gr, w_attn_br.astype(BF16), w_lru_br.astype(BF16),
                                    w_out.astype(BF16), row(g_moe), w_router_p, b_router_p)
    cap = T * TOP_K + N_EXPERTS * EXPERT_TILE
    counts = cnt[0, :N_EXPERTS].astype(jnp.int32)
    padded = ((counts + EXPERT_TILE - 1) // EXPERT_TILE) * EXPERT_TILE
    pad_end = jnp.cumsum(padded)
    pad_start = pad_end - padded
    blk_start = jnp.arange(cap // EXPERT_TILE, dtype=jnp.int32) * EXPERT_TILE
    blk_exp = jnp.minimum(jnp.sum(blk_start[:, None] >= pad_end[None, :], axis=1), N_EXPERTS - 1)
    n_used = (pad_end[-1:] // EXPERT_TILE).astype(jnp.int32)
    ps_row = jnp.zeros((1, V7X_LANES), F32).at[0, :N_EXPERTS].set(pad_start.astype(F32))
    dest = _rank(idx.reshape(T, TOP_K), ps_row).reshape(T * TOP_K)
    assert T & (T - 1) == 0, "token count must be a power of two for the slot encoding"
    inv3 = _invert(pad_end.astype(jnp.int32), counts, dest, cap, T).reshape(cap // EXPERT_TILE, 1, EXPERT_TILE)
    y4 = _experts(blk_exp.astype(jnp.int32), n_used, inv3, u2.reshape(T * ROW_TILE, V7X_LANES), w_gu,
                  b_gu.reshape(N_EXPERTS, 1, -1), w_dn, b_dn.reshape(N_EXPERTS, 1, -1))
    out = _combine(y4, gate.reshape(T, TOP_K), h1.reshape(T, D),
                   p.reshape(T, PLE_DIM), row(g_ple), w_ple_gate.astype(BF16), w_ple_proj.astype(BF16))
    return out.reshape(B, S, D)


def kernel(x, p, g_mix, w_in, q_norm, k_norm, conv_w, conv_b, lru_wa, lru_ba, lru_wi, lru_bi, lru_lam,
           w_attn_br, w_lru_br, w_out, g_moe, w_router, b_router, w_gu, b_gu, w_dn, b_dn,
           g_ple, w_ple_gate, w_ple_proj):
    rope = _rope_tables(x.shape[1])
    h = x
    for l in range(p.shape[0]):
        h = _layer(h, p[l], g_mix[l], w_in[l], q_norm[l], k_norm[l], conv_w[l], conv_b[l], lru_wa[l],
                   lru_ba[l], lru_wi[l], lru_bi[l], lru_lam[l], w_attn_br[l], w_lru_br[l], w_out[l],
                   g_moe[l], w_router[l], b_router[l], w_gu[l], b_gu[l], w_dn[l], b_dn[l],
                   g_ple[l], w_ple_gate[l], w_ple_proj[l], rope)
    return h
```

```python
import functools

import jax
import jax.numpy as jnp
import numpy as np
from jax import lax
from jax.experimental import pallas as pl
from jax.experimental.pallas import tpu as pltpu

D_MODEL = 1024
N_Q_HEADS = 8
N_KV_HEADS = 2
HEAD_DIM = 128
Q_GROUP = N_Q_HEADS // N_KV_HEADS
ROPE_THETA = 10000.0
GRID_W = 64
LRU_WIDTH = 1024
LRU_BLOCKS = 8
LRU_BLOCK_DIM = LRU_WIDTH // LRU_BLOCKS
LRU_C = 8.0
N_EXPERTS = 32
TOP_K = 4
D_EXPERT = 1024
SWIGLU_LIMIT = 7.0
SWIGLU_ALPHA = 1.702
PLE_DIM = 256
EPS = 1e-6
Q_W = N_Q_HEADS * HEAD_DIM
KV_W = N_KV_HEADS * HEAD_DIM
IN_WIDTH = Q_W + 2 * KV_W + 2 * LRU_WIDTH + 2 * D_MODEL
OFF_Q = 0
OFF_K = OFF_Q + Q_W
OFF_V = OFF_K + KV_W
OFF_XR = OFF_V + KV_W
OFF_XG = OFF_XR + LRU_WIDTH
OFF_GA = OFF_XG + LRU_WIDTH
OFF_GR = OFF_GA + D_MODEL

V7X_LANES = 128
V7X_VMEM_LIMIT_BYTES = 56 * 1024 * 1024
TOKEN_TILE = 512
ATTN_Q_TILE = 1024
LRU_STEPS = 32
EXPERT_TILE = 512
EXPERT_SLICES = 2
RANK_TILE = 512
INVERT_CHUNK = 8192
SPARE_ROWS = 1024

F32 = jnp.float32
BF16 = jnp.bfloat16


def _params(*sem):
    return pltpu.CompilerParams(dimension_semantics=sem, vmem_limit_bytes=V7X_VMEM_LIMIT_BYTES)


def _bdot(a, b):
    return jnp.dot(a, b, preferred_element_type=F32)


def _rms(x, g):
    return x * lax.rsqrt(jnp.mean(x * x, axis=-1, keepdims=True) + EPS) * g


ROW_TILE = 8


def _store_row_tiles(ref, x):
    n = x.shape[0]
    for j in range(ROW_TILE):
        ref[pl.ds(j, n, stride=ROW_TILE), :] = x[:, j * V7X_LANES:(j + 1) * V7X_LANES]


def _load_row_tiles(ref):
    n = ref.shape[0] // ROW_TILE
    return jnp.concatenate([ref[pl.ds(j, n, stride=ROW_TILE), :] for j in range(ROW_TILE)], axis=-1)


def _sigmoid(x):
    return 0.5 * jnp.tanh(0.5 * x) + 0.5


def _in_proj_kernel(x_ref, g_ref, w_ref, qn_ref, kn_ref, cos_ref, sina_ref, sinb_ref,
                    q_ref, k_ref, v_ref, xr_ref, xg_ref, ga_ref, gr_ref):
    u = _rms(x_ref[...], g_ref[...]).astype(BF16)
    cos, sina, sinb = cos_ref[...], sina_ref[...], sinb_ref[...]

    def head(z, gain):
        y = _rms(z, gain)
        return y * cos + pltpu.roll(y, 96, 1) * sina + pltpu.roll(y, 32, 1) * sinb

    zq = _bdot(u, w_ref[:, OFF_Q:OFF_Q + Q_W])
    for h in range(N_Q_HEADS):
        sl = slice(h * HEAD_DIM, (h + 1) * HEAD_DIM)
        q_ref[:, sl] = head(zq[:, sl], qn_ref[...]).astype(BF16)
    zk = _bdot(u, w_ref[:, OFF_K:OFF_K + KV_W])
    for h in range(N_KV_HEADS):
        sl = slice(h * HEAD_DIM, (h + 1) * HEAD_DIM)
        k_ref[:, sl] = head(zk[:, sl], kn_ref[...]).astype(BF16)
    v_ref[...] = _bdot(u, w_ref[:, OFF_V:OFF_V + KV_W]).astype(BF16)
    xr_ref[...] = _bdot(u, w_ref[:, OFF_XR:OFF_XR + LRU_WIDTH])
    xg_ref[...] = _bdot(u, w_ref[:, OFF_XG:OFF_XG + LRU_WIDTH])
    ga_ref[...] = _bdot(u, w_ref[:, OFF_GA:OFF_GA + D_MODEL])
    gr_ref[...] = _bdot(u, w_ref[:, OFF_GR:OFF_GR + D_MODEL])


def _in_proj(x, g_mix, w_in, qn, kn, cos, sina, sinb):
    B, S, D = x.shape
    tm = TOKEN_TILE
    bm = lambda w: pl.BlockSpec((None, tm, w), lambda b, i: (b, i, 0))
    full = lambda a: pl.BlockSpec(a.shape, lambda b, i: (0,) * a.ndim)
    tab = pl.BlockSpec((tm, HEAD_DIM), lambda b, i: (i, 0))
    return pl.pallas_call(
        _in_proj_kernel,
        grid=(B, S // tm),
        in_specs=[bm(D), full(g_mix), full(w_in), full(qn), full(kn), tab, tab, tab],
        out_specs=[bm(Q_W), bm(KV_W), bm(KV_W), bm(LRU_WIDTH), bm(LRU_WIDTH), bm(D), bm(D)],
        out_shape=[
            jax.ShapeDtypeStruct((B, S, Q_W), BF16),
            jax.ShapeDtypeStruct((B, S, KV_W), BF16),
            jax.ShapeDtypeStruct((B, S, KV_W), BF16),
            jax.ShapeDtypeStruct((B, S, LRU_WIDTH), F32),
            jax.ShapeDtypeStruct((B, S, LRU_WIDTH), F32),
            jax.ShapeDtypeStruct((B, S, D), F32),
            jax.ShapeDtypeStruct((B, S, D), F32),
        ],
        compiler_params=_params("arbitrary", "arbitrary"),
        name="in_proj",
    )(x, g_mix, w_in, qn, kn, cos, sina, sinb)


def _attn_kernel(q_ref, k_ref, v_ref, o_ref):
    for g in range(N_KV_HEADS):
        ks = slice(g * HEAD_DIM, (g + 1) * HEAD_DIM)
        k = k_ref[:, ks]
        v1 = jnp.concatenate([v_ref[:, ks], jnp.ones((v_ref.shape[0], HEAD_DIM), v_ref.dtype)], axis=-1)
        for h in range(g * Q_GROUP, (g + 1) * Q_GROUP):
            sl = slice(h * HEAD_DIM, (h + 1) * HEAD_DIM)
            s = lax.dot_general(q_ref[:, sl], k, (((1,), (1,)), ((), ())), preferred_element_type=F32)
            p = jnp.exp2(s - jnp.max(s, axis=-1, keepdims=True))
            o = _bdot(p.astype(BF16), v1)
            o_ref[:, sl] = (o[:, :HEAD_DIM] * (1.0 / o[:, HEAD_DIM:HEAD_DIM + 1])).astype(BF16)


def _attention(q, k, v):
    B, S, _ = q.shape
    tq = ATTN_Q_TILE
    return pl.pallas_call(
        _attn_kernel,
        grid=(B, S // tq),
        in_specs=[
            pl.BlockSpec((None, tq, Q_W), lambda b, i: (b, i, 0)),
            pl.BlockSpec((None, S, KV_W), lambda b, i: (b, 0, 0)),
            pl.BlockSpec((None, S, KV_W), lambda b, i: (b, 0, 0)),
        ],
        out_specs=pl.BlockSpec((None, tq, Q_W), lambda b, i: (b, i, 0)),
        out_shape=jax.ShapeDtypeStruct((B, S, Q_W), BF16),
        compiler_params=_params("arbitrary", "arbitrary"),
        name="attention",
    )(q, k, v)


def _gelu_tanh(x):
    return 0.5 * x * (1.0 + jnp.tanh(np.sqrt(2.0 / np.pi) * (x + 0.044715 * (x * x * x))))


def _lru_kernel(*refs, batch, reverse):
    if reverse:
        (xc_ref, xp_ref, xn_ref, cw_ref, cb_ref, w_ref, b_ref, lam_ref, hf_ref, xg_ref,
         o_ref, ext_ref, a_ref, bt_ref, h_ref) = refs
    else:
        (xc_ref, xp_ref, xn_ref, cw_ref, cb_ref, w_ref, b_ref, lam_ref,
         o_ref, ext_ref, a_ref, bt_ref, h_ref) = refs
    step = pl.program_id(0)
    nsteps = pl.num_programs(0)
    chunk = (nsteps - 1 - step) if reverse else step
    rows = LRU_STEPS * batch

    @pl.when(step == 0)
    def _():
        h_ref[...] = jnp.zeros_like(h_ref)

    ext_ref[0:2 * batch, :] = jnp.where(chunk > 0, xp_ref[...], 0.0)
    ext_ref[2 * batch:2 * batch + rows, :] = xc_ref[...]
    ext_ref[2 * batch + rows:3 * batch + rows, :] = jnp.where(chunk < nsteps - 1, xn_ref[...], 0.0)
    c = cb_ref[...] + sum(cw_ref[j:j + 1, :] * ext_ref[j * batch:j * batch + rows, :] for j in range(4))
    cb16 = c.astype(BF16)
    lam = lam_ref[...]
    half_scale = (-0.5 * LRU_C) * (jnp.maximum(-lam, 0.0) + jnp.log1p(jnp.exp(-jnp.abs(lam))))
    for n in range(LRU_BLOCKS):
        sl = slice(n * LRU_BLOCK_DIM, (n + 1) * LRU_BLOCK_DIM)
        th = jnp.tanh(_bdot(cb16[:, sl], w_ref[n]) + b_ref[n])
        ig = 0.5 * th[:, LRU_BLOCK_DIM:] + 0.5
        log_a = half_scale[:, sl] * th[:, :LRU_BLOCK_DIM] + half_scale[:, sl]
        a = jnp.exp(log_a)
        a_ref[:, sl] = a
        one_minus_a2 = -jnp.tanh(log_a) * (a * a + 1.0)
        root = jnp.where(one_minus_a2 > 0.0, one_minus_a2 * lax.rsqrt(one_minus_a2), 0.0)
        bt_ref[:, sl] = root * ig * c[:, sl]

    def scan_step(t, h):
        tt = (LRU_STEPS - 1 - t) if reverse else t
        rs = pl.ds(pl.multiple_of(tt * batch, batch), batch)
        h = a_ref[rs, :] * h + bt_ref[rs, :]
        if reverse:
            o_ref[rs, :] = ((h + hf_ref[rs, :]) * _gelu_tanh(xg_ref[rs, :])).astype(o_ref.dtype)
        else:
            o_ref[rs, :] = h
        return h

    h_ref[...] = lax.fori_loop(0, LRU_STEPS, scan_step, h_ref[...], unroll=4)


def _lru_scan(xr2, conv_w, conv_b, wcat, bcat, lam, *, batch, reverse, h_fwd=None, xg2=None):
    n_rows, width = xr2.shape
    rows = LRU_STEPS * batch
    n = n_rows // rows
    ch = (lambda i: n - 1 - i) if reverse else (lambda i: i)
    tile = pl.BlockSpec((rows, width), lambda i: (ch(i), 0))
    prev = pl.BlockSpec((2 * batch, width), lambda i: (jnp.maximum(ch(i) * (LRU_STEPS // 2) - 1, 0), 0))
    nxt = pl.BlockSpec((batch, width),
                       lambda i: (jnp.minimum((ch(i) + 1) * LRU_STEPS, n * LRU_STEPS - 1), 0))
    full = lambda a: pl.BlockSpec(a.shape, lambda i: (0,) * a.ndim)
    ins = [xr2, xr2, xr2, conv_w, conv_b, wcat, bcat, lam]
    specs = [tile, prev, nxt, full(conv_w), full(conv_b), full(wcat), full(bcat), full(lam)]
    if reverse:
        ins += [h_fwd, xg2]
        specs += [tile, tile]
    return pl.pallas_call(
        functools.partial(_lru_kernel, batch=batch, reverse=reverse),
        grid=(n,),
        in_specs=specs,
        out_specs=tile,
        out_shape=jax.ShapeDtypeStruct((n_rows, width), BF16 if reverse else F32),
        scratch_shapes=[
            pltpu.VMEM((rows + 3 * batch, width), F32),
            pltpu.VMEM((rows, width), F32),
            pltpu.VMEM((rows, width), F32),
            pltpu.VMEM((batch, width), F32),
        ],
        compiler_params=_params("arbitrary"),
        name="lru_bwd" if reverse else "lru_fwd",
    )(*ins)


def _merge_kernel(x_ref, at_ref, yl_ref, ga_ref, gr_ref, wa_ref, wl_ref, wo_ref, gm_ref, wr_ref, br_ref,
                  h_ref, u_ref, idx_ref, gate_ref, cnt_ref):
    y_attn = _bdot(at_ref[...], wa_ref[...])
    y_lru = _bdot(yl_ref[...], wl_ref[...])
    merged = _sigmoid(ga_ref[...]) * y_attn + _sigmoid(gr_ref[...]) * y_lru
    h = x_ref[...] + _bdot(merged.astype(BF16), wo_ref[...])
    h_ref[...] = h
    u = _rms(h, gm_ref[...])
    _store_row_tiles(u_ref, u)
    u_hi = u.astype(BF16)
    u_lo = (u - u_hi.astype(F32)).astype(BF16)
    logits = _bdot(u_hi, wr_ref[0]) + (_bdot(u_lo, wr_ref[0]) + _bdot(u_hi, wr_ref[1]))
    lane = lax.broadcasted_iota(jnp.int32, logits.shape, 1)
    work = jnp.where(lane < N_EXPERTS, logits + br_ref[...], -jnp.inf)
    vals, hits = [], jnp.zeros(logits.shape, F32)
    for kk in range(TOP_K):
        m = jnp.max(work, axis=-1, keepdims=True)
        idx = jnp.min(jnp.where(work == m, lane, V7X_LANES), axis=-1, keepdims=True)
        pick = lane == idx
        work = jnp.where(pick, -jnp.inf, work)
        hits = hits + pick.astype(F32)
        vals.append(m)
        idx_ref[:, kk:kk + 1] = idx
    es = [jnp.exp(vv - vals[0]) for vv in vals]
    inv = 1.0 / sum(es)
    for kk in range(TOP_K):
        gate_ref[:, kk:kk + 1] = es[kk] * inv

    @pl.when((pl.program_id(0) == 0) & (pl.program_id(1) == 0))
    def _():
        cnt_ref[...] = jnp.zeros_like(cnt_ref)

    cnt_ref[0:1, :] += jnp.sum(hits, axis=0, keepdims=True)


def _merge(x, attn, ylru, ga, gr, wa, wl, wo, g_moe, wr, br):
    B, S, D = x.shape
    tm = TOKEN_TILE
    bm = lambda w: pl.BlockSpec((None, tm, w), lambda b, i: (b, i, 0))
    full = lambda a: pl.BlockSpec(a.shape, lambda b, i: (0,) * a.ndim)
    return pl.pallas_call(
        _merge_kernel,
        grid=(B, S // tm),
        in_specs=[bm(D), bm(Q_W), bm(LRU_WIDTH), bm(D), bm(D), full(wa), full(wl), full(wo),
                  full(g_moe), full(wr), full(br)],
        out_specs=[bm(D), pl.BlockSpec((None, tm * ROW_TILE, V7X_LANES), lambda b, i: (b, i, 0)),
                   bm(TOP_K), bm(TOP_K), pl.BlockSpec((8, V7X_LANES), lambda b, i: (0, 0))],
        out_shape=[
            jax.ShapeDtypeStruct((B, S, D), F32),
            jax.ShapeDtypeStruct((B, S * ROW_TILE, V7X_LANES), F32),
            jax.ShapeDtypeStruct((B, S, TOP_K), jnp.int32),
            jax.ShapeDtypeStruct((B, S, TOP_K), F32),
            jax.ShapeDtypeStruct((8, V7X_LANES), F32),
        ],
        compiler_params=_params("arbitrary", "arbitrary"),
        name="merge_router",
    )(x, attn, ylru, ga, gr, wa, wl, wo, g_moe, wr, br)


def _rank_kernel(idx_ref, ps_ref, dest_ref, carry_ref):
    @pl.when(pl.program_id(0) == 0)
    def _():
        carry_ref[...] = jnp.zeros_like(carry_ref)

    tm = idx_ref.shape[0]
    lane = lax.broadcasted_iota(jnp.int32, (tm, V7X_LANES), 1)
    picks = [lane == idx_ref[:, kk:kk + 1] for kk in range(TOP_K)]
    hits = sum(p.astype(F32) for p in picks)
    earlier = (lax.broadcasted_iota(jnp.int32, (tm, tm), 0) >
               lax.broadcasted_iota(jnp.int32, (tm, tm), 1)).astype(BF16)
    base = _bdot(earlier, hits.astype(BF16)) + carry_ref[...] + ps_ref[...]
    for kk in range(TOP_K):
        dest_ref[:, kk:kk + 1] = jnp.sum(jnp.where(picks[kk], base, 0.0), axis=-1,
                                         keepdims=True).astype(jnp.int32)
    carry_ref[...] += jnp.sum(hits, axis=0, keepdims=True)


def _rank(idx, pad_start_row):
    T = idx.shape[0]
    tm = RANK_TILE
    return pl.pallas_call(
        _rank_kernel,
        grid=(T // tm,),
        in_specs=[pl.BlockSpec((tm, TOP_K), lambda i: (i, 0)),
                  pl.BlockSpec((1, V7X_LANES), lambda i: (0, 0))],
        out_specs=pl.BlockSpec((tm, TOP_K), lambda i: (i, 0)),
        out_shape=jax.ShapeDtypeStruct((T, TOP_K), jnp.int32),
        scratch_shapes=[pltpu.VMEM((1, V7X_LANES), F32)],
        compiler_params=_params("arbitrary"),
        name="slot_rank",
    )(idx, pad_start_row)


def _invert_kernel(pad_end_ref, count_ref, dest_vmem, out_vmem, dest_ref, inv_ref, sems, *, n_tokens, chunk):
    n_chunks = dest_vmem.shape[0] // chunk

    def stage_in(c):
        return pltpu.make_async_copy(dest_vmem.at[pl.ds(c * chunk, chunk)], dest_ref.at[c % 2], sems.at[c % 2])

    stage_in(0).start()

    def fill(s, carry):
        inv_ref[s] = TOP_K * n_tokens + (s & (SPARE_ROWS - 1))
        return carry

    for e in range(N_EXPERTS):
        lo = pad_end_ref[e] - ((count_ref[e] + EXPERT_TILE - 1) & -EXPERT_TILE) + count_ref[e]
        lax.fori_loop(lo, pad_end_ref[e], fill, 0)
    lax.fori_loop(pad_end_ref[N_EXPERTS - 1], inv_ref.shape[0], fill, 0)

    group = 4 * TOP_K
    for c in range(n_chunks):
        stage_in(c).wait()
        if c + 1 < n_chunks:
            stage_in(c + 1).start()

        def body(g, carry, c=c):
            tok0 = c * (chunk // TOP_K) + g * (group // TOP_K)
            for u in range(group):
                inv_ref[dest_ref[c % 2, g * group + u]] = tok0 + ((u % TOP_K) * n_tokens + u // TOP_K)
            return carry

        lax.fori_loop(0, chunk // group, body, 0)

    stage_out = pltpu.make_async_copy(inv_ref, out_vmem, sems.at[0])
    stage_out.start()
    stage_out.wait()


def _invert(pad_end, counts, dest_flat, cap, n_tokens):
    chunk = INVERT_CHUNK
    n = dest_flat.shape[0]
    return pl.pallas_call(
        functools.partial(_invert_kernel, n_tokens=n_tokens, chunk=chunk),
        grid_spec=pltpu.PrefetchScalarGridSpec(
            num_scalar_prefetch=2,
            grid=(1,),
            in_specs=[pl.BlockSpec((n,), lambda i, pe, ct: (0,))],
            out_specs=pl.BlockSpec((cap,), lambda i, pe, ct: (0,)),
            scratch_shapes=[pltpu.SMEM((2, chunk), jnp.int32), pltpu.SMEM((cap,), jnp.int32),
                            pltpu.SemaphoreType.DMA((2,))],
        ),
        out_shape=jax.ShapeDtypeStruct((cap,), jnp.int32),
        compiler_params=_params("arbitrary"),
        name="slot_invert",
    )(pad_end, counts, dest_flat)


def _expert_kernel(be_ref, nu_ref, invp_ref, invc_ref, invn_ref, u_hbm, wgu_ref, bgu_ref, wdn_ref, bdn_ref,
                   y_hbm, xbuf0, xbuf1, ybuf0, ybuf1, gsem, ssem, *, n_tokens):
    xbuf, ybuf = (xbuf0, xbuf1), (ybuf0, ybuf1)
    j = pl.program_id(0)
    n_used = nu_ref[0]
    parity = j & 1

    def tile_rows(row):
        if isinstance(row, int):
            return pl.ds(row * ROW_TILE, ROW_TILE)
        return pl.ds(pl.multiple_of(row * ROW_TILE, ROW_TILE), ROW_TILE)

    def gather(inv_ref, r, buf, dep=0):
        tok = (inv_ref[0, 0, r] & (n_tokens - 1)) + dep
        return pltpu.make_async_copy(u_hbm.at[tile_rows(tok)], xbuf[buf].at[tile_rows(r)], gsem.at[buf])

    def scatter(inv_ref, r, buf, dep=0):
        return pltpu.make_async_copy(ybuf[buf].at[tile_rows(r)], y_hbm.at[tile_rows(inv_ref[0, 0, r] + dep)],
                                     ssem.at[buf])

    def wait_gather(buf):
        pltpu.make_async_copy(xbuf[buf], xbuf[buf], gsem.at[buf]).wait()

    def wait_scatter(buf):
        pltpu.make_async_copy(ybuf[buf], ybuf[buf], ssem.at[buf]).wait()

    @pl.when(j == 0)
    def _():
        def first(r, carry):
            gather(invc_ref, r, 0).start()
            return carry
        lax.fori_loop(0, EXPERT_TILE, first, 0)
        xbuf[1][...] = jnp.zeros(xbuf[1].shape, xbuf[1].dtype)
        ybuf[1][...] = jnp.zeros(ybuf[1].shape, ybuf[1].dtype)
        clears = [pltpu.make_async_copy(
            xbuf[1], y_hbm.at[pl.ds((TOP_K * n_tokens + c * EXPERT_TILE) * ROW_TILE, EXPERT_TILE * ROW_TILE)],
            ssem.at[1]) for c in range(SPARE_ROWS // EXPERT_TILE)]
        for c in clears:
            c.start()
        for c in clears:
            c.wait()

    def block(cur):
        oth = 1 - cur
        wait_gather(cur)

        @pl.when(j > 0)
        def _():
            wait_scatter(cur)

        x = _load_row_tiles(xbuf[cur]).astype(BF16)
        per = EXPERT_TILE // EXPERT_SLICES
        fc = D_EXPERT // EXPERT_SLICES
        acc = None
        dep = 0
        for c in range(EXPERT_SLICES):
            for r in range(c * per, (c + 1) * per):
                gather(invn_ref, r, oth, dep).start(priority=r % 2)
            for r in range(c * per, (c + 1) * per):
                scatter(invp_ref, r, oth, dep).start(priority=r % 2)
            gs = slice(c * fc, (c + 1) * fc)
            us = slice(D_EXPERT + c * fc, D_EXPERT + (c + 1) * fc)
            gate = jnp.minimum(_bdot(x, wgu_ref[:, gs].astype(BF16)) + bgu_ref[:, gs], SWIGLU_LIMIT)
            up = jnp.clip(_bdot(x, wgu_ref[:, us].astype(BF16)) + bgu_ref[:, us], -SWIGLU_LIMIT, SWIGLU_LIMIT)
            act = gate * _sigmoid(SWIGLU_ALPHA * gate)
            part = _bdot(((up + 1.0) * act).astype(BF16), wdn_ref[gs, :].astype(BF16))
            acc = part if acc is None else acc + part
            bits = pltpu.bitcast(part[0:8, 0:V7X_LANES], jnp.uint32)[0, 0]
            dep = lax.shift_right_logical(lax.shift_right_logical(bits, jnp.uint32(16)),
                                          jnp.uint32(16)).astype(jnp.int32)
        _store_row_tiles(ybuf[cur], acc + bdn_ref[...])

        @pl.when(j == n_used - 1)
        def _():
            wait_scatter(oth)
            wait_gather(oth)

            def last(r, carry):
                scatter(invc_ref, r, cur).start()
                return carry
            lax.fori_loop(0, EXPERT_TILE, last, 0)
            wait_scatter(cur)

    for cur in range(2):
        @pl.when((j < n_used) & (parity == cur))
        def _():
            block(cur)


def _experts(blk_exp, n_used, inv3, u, w_gu, b_gu, w_dn, b_dn):
    T, D = u.shape[0] // ROW_TILE, D_MODEL
    n_blocks = inv3.shape[0]
    exp3 = lambda j, be, nu: (be[j], 0, 0)
    inv_spec = lambda f: pl.BlockSpec((1, 1, EXPERT_TILE), lambda j, be, nu: (f(j, nu), 0, 0),
                                      memory_space=pltpu.SMEM)
    last = lambda j, nu: jnp.minimum(j, nu[0] - 1)
    return pl.pallas_call(
        functools.partial(_expert_kernel, n_tokens=T),
        grid_spec=pltpu.PrefetchScalarGridSpec(
            num_scalar_prefetch=2,
            grid=(n_blocks,),
            in_specs=[inv_spec(lambda j, nu: jnp.maximum(last(j, nu) - 1, 0)),
                      inv_spec(last),
                      inv_spec(lambda j, nu: jnp.minimum(j + 1, nu[0] - 1)),
                      pl.BlockSpec(memory_space=pl.ANY),
                      pl.BlockSpec((None, D, 2 * D_EXPERT), exp3),
                      pl.BlockSpec((None, 1, 2 * D_EXPERT), exp3),
                      pl.BlockSpec((None, D_EXPERT, D), exp3),
                      pl.BlockSpec((None, 1, D), exp3)],
            out_specs=pl.BlockSpec(memory_space=pl.ANY),
            scratch_shapes=[pltpu.VMEM((EXPERT_TILE * ROW_TILE, V7X_LANES), F32)] * 4 + [
                pltpu.SemaphoreType.DMA((2,)), pltpu.SemaphoreType.DMA((2,))],
        ),
        out_shape=jax.ShapeDtypeStruct(((TOP_K * T + SPARE_ROWS) * ROW_TILE, V7X_LANES), F32),
        compiler_params=_params("arbitrary"),
        name="experts",
    )(blk_exp, n_used, inv3, inv3, inv3, u, w_gu, b_gu, w_dn, b_dn)


def _combine_kernel(y0_ref, y1_ref, y2_ref, y3_ref, gate_ref, h_ref, p_ref, gp_ref, wg_ref, wp_ref, o_ref):
    h = h_ref[...]
    for kk, y_ref in enumerate((y0_ref, y1_ref, y2_ref, y3_ref)):
        h = h + gate_ref[:, kk:kk + 1] * _load_row_tiles(y_ref)
    u = _rms(h, gp_ref[...]).astype(BF16)
    o_ref[...] = h + _sigmoid(_bdot(u, wg_ref[...])) * _bdot(p_ref[...].astype(BF16), wp_ref[...])


def _combine(y4, gate, h, p, g_ple, wg, wp):
    T, D = h.shape
    tm = TOKEN_TILE
    full = lambda a: pl.BlockSpec(a.shape, lambda i: (0,) * a.ndim)
    plane = lambda kk: pl.BlockSpec((tm * ROW_TILE, V7X_LANES), lambda i: (kk * (T // tm) + i, 0))
    return pl.pallas_call(
        _combine_kernel,
        grid=(T // tm,),
        in_specs=[plane(0), plane(1), plane(2), plane(3),
                  pl.BlockSpec((tm, TOP_K), lambda i: (i, 0)),
                  pl.BlockSpec((tm, D), lambda i: (i, 0)),
                  pl.BlockSpec((tm, PLE_DIM), lambda i: (i, 0)),
                  full(g_ple), full(wg), full(wp)],
        out_specs=pl.BlockSpec((tm, D), lambda i: (i, 0)),
        out_shape=jax.ShapeDtypeStruct((T, D), F32),
        compiler_params=_params("arbitrary"),
        name="combine_ple",
    )(y4, y4, y4, y4, gate, h, p, g_ple, wg, wp)


def _rope_tables(seq_len):
    pos = np.arange(seq_len)
    axis_dim = HEAD_DIM // 2
    inv_freq = (ROPE_THETA ** (-np.arange(0, axis_dim, 2, dtype=np.float32) / axis_dim)).astype(np.float32)
    ang_r = (pos // GRID_W).astype(np.float32)[:, None] * inv_freq[None, :]
    ang_c = (pos % GRID_W).astype(np.float32)[:, None] * inv_freq[None, :]
    zero = np.zeros_like(ang_r)
    cos = np.concatenate([np.cos(ang_r)] * 2 + [np.cos(ang_c)] * 2, axis=-1)
    sina = np.concatenate([-np.sin(ang_r), zero, -np.sin(ang_c), zero], axis=-1)
    sinb = np.concatenate([zero, np.sin(ang_r), zero, np.sin(ang_c)], axis=-1)
    return jnp.asarray(cos, F32), jnp.asarray(sina, F32), jnp.asarray(sinb, F32)


def _layer(h, p, g_mix, w_in, q_norm, k_norm, conv_w, conv_b, lru_wa, lru_ba, lru_wi, lru_bi, lru_lam,
           w_attn_br, w_lru_br, w_out, g_moe, w_router, b_router, w_gu, b_gu, w_dn, b_dn,
           g_ple, w_ple_gate, w_ple_proj, rope):
    B, S, D = h.shape
    T = B * S
    assert S % ATTN_Q_TILE == 0 and S % TOKEN_TILE == 0 and S % LRU_STEPS == 0 and B % 8 == 0
    assert T % RANK_TILE == 0 and (T * TOP_K) % INVERT_CHUNK == 0
    row = lambda a: a.reshape(1, -1)
    q, k, v, xr, xg, ga, gr = _in_proj(
        h, row(g_mix), w_in.astype(BF16), row(q_norm) * (HEAD_DIM ** -0.5 * np.log2(np.e)), row(k_norm), *rope)
    attn = _attention(q, k, v)
    xr2 = xr.transpose(1, 0, 2).reshape(S * B, LRU_WIDTH)
    xg2 = xg.transpose(1, 0, 2).reshape(S * B, LRU_WIDTH)
    wcat = (0.5 * jnp.concatenate([lru_wa, lru_wi], axis=-1)).astype(BF16)
    bcat = 0.5 * jnp.concatenate([lru_ba.reshape(2, LRU_BLOCKS, 1, LRU_BLOCK_DIM),
                                  lru_bi.reshape(2, LRU_BLOCKS, 1, LRU_BLOCK_DIM)], axis=-1)
    h_fwd = _lru_scan(xr2, conv_w, row(conv_b), wcat[0], bcat[0], row(lru_lam[0]), batch=B, reverse=False)
    ylru = _lru_scan(xr2, conv_w, row(conv_b), wcat[1], bcat[1], row(lru_lam[1]), batch=B, reverse=True,
                     h_fwd=h_fwd, xg2=xg2).reshape(S, B, LRU_WIDTH).transpose(1, 0, 2)
    w_router_f = jnp.zeros((D, V7X_LANES), F32).at[:, :N_EXPERTS].set(w_router)
    w_router_hi = w_router_f.astype(BF16)
    w_router_p = jnp.stack([w_router_hi, (w_router_f - w_router_hi.astype(F32)).astype(BF16)])
    b_router_p = jnp.zeros((1, V7X_LANES), F32).at[0, :N_EXPERTS].set(b_router)
    h1, u2, idx, gate, cnt = _merge(h, attn, ylru, ga, gr, w_attn_br.astype(BF16), w_lru_br.astype(BF16),
                                    w_out.astype(BF16), row(g_moe), w_router_p, b_router_p)
    cap = T * TOP_K + N_EXPERTS * EXPERT_TILE
    counts = cnt[0, :N_EXPERTS].astype(jnp.int32)
    padded = ((counts + EXPERT_TILE - 1) // EXPERT_TILE) * EXPERT_TILE
    pad_end = jnp.cumsum(padded)
    pad_start = pad_end - padded
    blk_start = jnp.arange(cap // EXPERT_TILE, dtype=jnp.int32) * EXPERT_TILE
    blk_exp = jnp.minimum(jnp.sum(blk_start[:, None] >= pad_end[None, :], axis=1), N_EXPERTS - 1)
    n_used = (pad_end[-1:] // EXPERT_TILE).astype(jnp.int32)
    ps_row = jnp.zeros((1, V7X_LANES), F32).at[0, :N_EXPERTS].set(pad_start.astype(F32))
    dest = _rank(idx.reshape(T, TOP_K), ps_row).reshape(T * TOP_K)
    assert T & (T - 1) == 0, "token count must be a power of two for the slot encoding"
    inv3 = _invert(pad_end.astype(jnp.int32), counts, dest, cap, T).reshape(cap // EXPERT_TILE, 1, EXPERT_TILE)
    y4 = _experts(blk_exp.astype(jnp.int32), n_used, inv3, u2.reshape(T * ROW_TILE, V7X_LANES), w_gu,
                  b_gu.reshape(N_EXPERTS, 1, -1), w_dn, b_dn.reshape(N_EXPERTS, 1, -1))
    out = _combine(y4, gate.reshape(T, TOP_K), h1.reshape(T, D),
                   p.reshape(T, PLE_DIM), row(g_ple), w_ple_gate.astype(BF16), w_ple_proj.astype(BF16))
    return out.reshape(B, S, D)


def kernel(x, p, g_mix, w_in, q_norm, k_norm, conv_w, conv_b, lru_wa, lru_ba, lru_wi, lru_bi, lru_lam,
           w_attn_br, w_lru_br, w_out, g_moe, w_router, b_router, w_gu, b_gu, w_dn, b_dn,
           g_ple, w_ple_gate, w_ple_proj):
    rope = _rope_tables(x.shape[1])
    h = x
    for l in range(p.shape[0]):
        h = _layer(h, p[l], g_mix[l], w_in[l], q_norm[l], k_norm[l], conv_w[l], conv_b[l], lru_wa[l],
                   lru_ba[l], lru_wi[l], lru_bi[l], lru_lam[l], w_attn_br[l], w_lru_br[l], w_out[l],
                   g_moe[l], w_router[l], b_router[l], w_gu[l], b_gu[l], w_dn[l], b_dn[l],
                   g_ple[l], w_ple_gate[l], w_ple_proj[l], rope)
    return h
```

```python
import functools

import jax
import jax.numpy as jnp
import numpy as np
from jax import lax
from jax.experimental import pallas as pl
from jax.experimental.pallas import tpu as pltpu

D_MODEL = 1024
N_Q_HEADS = 8
N_KV_HEADS = 2
HEAD_DIM = 128
Q_GROUP = N_Q_HEADS // N_KV_HEADS
ROPE_THETA = 10000.0
GRID_W = 64
LRU_WIDTH = 1024
LRU_BLOCKS = 8
LRU_BLOCK_DIM = LRU_WIDTH // LRU_BLOCKS
LRU_C = 8.0
N_EXPERTS = 32
TOP_K = 4
D_EXPERT = 1024
SWIGLU_LIMIT = 7.0
SWIGLU_ALPHA = 1.702
PLE_DIM = 256
EPS = 1e-6
Q_W = N_Q_HEADS * HEAD_DIM
KV_W = N_KV_HEADS * HEAD_DIM
IN_WIDTH = Q_W + 2 * KV_W + 2 * LRU_WIDTH + 2 * D_MODEL
OFF_Q = 0
OFF_K = OFF_Q + Q_W
OFF_V = OFF_K + KV_W
OFF_XR = OFF_V + KV_W
OFF_XG = OFF_XR + LRU_WIDTH
OFF_GA = OFF_XG + LRU_WIDTH
OFF_GR = OFF_GA + D_MODEL

V7X_LANES = 128
V7X_VMEM_LIMIT_BYTES = 56 * 1024 * 1024
TOKEN_TILE = 512
ATTN_Q_TILE = 1024
LRU_STEPS = 32
EXPERT_TILE = 512
EXPERT_SLICES = 2
RANK_TILE = 512
INVERT_CHUNK = 32768
SPARE_ROWS = 1024

F32 = jnp.float32
BF16 = jnp.bfloat16


def _params(*sem):
    return pltpu.CompilerParams(dimension_semantics=sem, vmem_limit_bytes=V7X_VMEM_LIMIT_BYTES)


def _bdot(a, b):
    return jnp.dot(a, b, preferred_element_type=F32)


def _rms(x, g):
    return x * lax.rsqrt(jnp.mean(x * x, axis=-1, keepdims=True) + EPS) * g


ROW_TILE = 8


def _store_row_tiles(ref, x):
    n = x.shape[0]
    for j in range(ROW_TILE):
        ref[pl.ds(j, n, stride=ROW_TILE), :] = x[:, j * V7X_LANES:(j + 1) * V7X_LANES]


def _load_row_tiles(ref):
    n = ref.shape[0] // ROW_TILE
    return jnp.concatenate([ref[pl.ds(j, n, stride=ROW_TILE), :] for j in range(ROW_TILE)], axis=-1)


def _sigmoid(x):
    return 0.5 * jnp.tanh(0.5 * x) + 0.5


def _in_proj_kernel(x_ref, g_ref, w_ref, qn_ref, kn_ref, cos_ref, sina_ref, sinb_ref,
                    q_ref, k_ref, v_ref, xr_ref, xg_ref, ga_ref, gr_ref):
    u = _rms(x_ref[...], g_ref[...]).astype(BF16)
    cos, sina, sinb = cos_ref[...], sina_ref[...], sinb_ref[...]

    def head(z, gain):
        y = _rms(z, gain)
        return y * cos + pltpu.roll(y, 96, 1) * sina + pltpu.roll(y, 32, 1) * sinb

    zq = _bdot(u, w_ref[:, OFF_Q:OFF_Q + Q_W])
    for h in range(N_Q_HEADS):
        sl = slice(h * HEAD_DIM, (h + 1) * HEAD_DIM)
        q_ref[:, sl] = head(zq[:, sl], qn_ref[...]).astype(BF16)
    zk = _bdot(u, w_ref[:, OFF_K:OFF_K + KV_W])
    for h in range(N_KV_HEADS):
        sl = slice(h * HEAD_DIM, (h + 1) * HEAD_DIM)
        k_ref[:, sl] = head(zk[:, sl], kn_ref[...]).astype(BF16)
    v_ref[...] = _bdot(u, w_ref[:, OFF_V:OFF_V + KV_W]).astype(BF16)
    xr_ref[...] = _bdot(u, w_ref[:, OFF_XR:OFF_XR + LRU_WIDTH])
    xg_ref[...] = _bdot(u, w_ref[:, OFF_XG:OFF_XG + LRU_WIDTH])
    ga_ref[...] = _bdot(u, w_ref[:, OFF_GA:OFF_GA + D_MODEL])
    gr_ref[...] = _bdot(u, w_ref[:, OFF_GR:OFF_GR + D_MODEL])


def _in_proj(x, g_mix, w_in, qn, kn, cos, sina, sinb):
    B, S, D = x.shape
    tm = TOKEN_TILE
    bm = lambda w: pl.BlockSpec((None, tm, w), lambda b, i: (b, i, 0))
    full = lambda a: pl.BlockSpec(a.shape, lambda b, i: (0,) * a.ndim)
    tab = pl.BlockSpec((tm, HEAD_DIM), lambda b, i: (i, 0))
    return pl.pallas_call(
        _in_proj_kernel,
        grid=(B, S // tm),
        in_specs=[bm(D), full(g_mix), full(w_in), full(qn), full(kn), tab, tab, tab],
        out_specs=[bm(Q_W), bm(KV_W), bm(KV_W), bm(LRU_WIDTH), bm(LRU_WIDTH), bm(D), bm(D)],
        out_shape=[
            jax.ShapeDtypeStruct((B, S, Q_W), BF16),
            jax.ShapeDtypeStruct((B, S, KV_W), BF16),
            jax.ShapeDtypeStruct((B, S, KV_W), BF16),
            jax.ShapeDtypeStruct((B, S, LRU_WIDTH), F32),
            jax.ShapeDtypeStruct((B, S, LRU_WIDTH), F32),
            jax.ShapeDtypeStruct((B, S, D), F32),
            jax.ShapeDtypeStruct((B, S, D), F32),
        ],
        compiler_params=_params("arbitrary", "arbitrary"),
        name="in_proj",
    )(x, g_mix, w_in, qn, kn, cos, sina, sinb)


def _attn_kernel(q_ref, k_ref, v_ref, o_ref):
    for g in range(N_KV_HEADS):
        ks = slice(g * HEAD_DIM, (g + 1) * HEAD_DIM)
        k = k_ref[:, ks]
        v1 = jnp.concatenate([v_ref[:, ks], jnp.ones((v_ref.shape[0], HEAD_DIM), v_ref.dtype)], axis=-1)
        for h in range(g * Q_GROUP, (g + 1) * Q_GROUP):
            sl = slice(h * HEAD_DIM, (h + 1) * HEAD_DIM)
            s = lax.dot_general(q_ref[:, sl], k, (((1,), (1,)), ((), ())), preferred_element_type=F32)
            p = jnp.exp2(s - jnp.max(s, axis=-1, keepdims=True))
            o = _bdot(p.astype(BF16), v1)
            o_ref[:, sl] = (o[:, :HEAD_DIM] * (1.0 / o[:, HEAD_DIM:HEAD_DIM + 1])).astype(BF16)


def _attention(q, k, v):
    B, S, _ = q.shape
    tq = ATTN_Q_TILE
    return pl.pallas_call(
        _attn_kernel,
        grid=(B, S // tq),
        in_specs=[
            pl.BlockSpec((None, tq, Q_W), lambda b, i: (b, i, 0)),
            pl.BlockSpec((None, S, KV_W), lambda b, i: (b, 0, 0)),
            pl.BlockSpec((None, S, KV_W), lambda b, i: (b, 0, 0)),
        ],
        out_specs=pl.BlockSpec((None, tq, Q_W), lambda b, i: (b, i, 0)),
        out_shape=jax.ShapeDtypeStruct((B, S, Q_W), BF16),
        compiler_params=_params("arbitrary", "arbitrary"),
        name="attention",
    )(q, k, v)


def _gelu_tanh(x):
    return 0.5 * x * (1.0 + jnp.tanh(np.sqrt(2.0 / np.pi) * (x + 0.044715 * (x * x * x))))


def _lru_kernel(*refs, batch, reverse):
    if reverse:
        (xc_ref, xp_ref, xn_ref, cw_ref, cb_ref, w_ref, b_ref, lam_ref, hf_ref, xg_ref,
         o_ref, ext_ref, a_ref, bt_ref, h_ref) = refs
    else:
        (xc_ref, xp_ref, xn_ref, cw_ref, cb_ref, w_ref, b_ref, lam_ref,
         o_ref, ext_ref, a_ref, bt_ref, h_ref) = refs
    step = pl.program_id(0)
    nsteps = pl.num_programs(0)
    chunk = (nsteps - 1 - step) if reverse else step
    rows = LRU_STEPS * batch

    @pl.when(step == 0)
    def _():
        h_ref[...] = jnp.zeros_like(h_ref)

    ext_ref[0:2 * batch, :] = jnp.where(chunk > 0, xp_ref[...], 0.0)
    ext_ref[2 * batch:2 * batch + rows, :] = xc_ref[...]
    ext_ref[2 * batch + rows:3 * batch + rows, :] = jnp.where(chunk < nsteps - 1, xn_ref[...], 0.0)
    c = cb_ref[...] + sum(cw_ref[j:j + 1, :] * ext_ref[j * batch:j * batch + rows, :] for j in range(4))
    cb16 = c.astype(BF16)
    lam = lam_ref[...]
    half_scale = (-0.5 * LRU_C) * (jnp.maximum(-lam, 0.0) + jnp.log1p(jnp.exp(-jnp.abs(lam))))
    for n in range(LRU_BLOCKS):
        sl = slice(n * LRU_BLOCK_DIM, (n + 1) * LRU_BLOCK_DIM)
        th = jnp.tanh(_bdot(cb16[:, sl], w_ref[n]) + b_ref[n])
        ig = 0.5 * th[:, LRU_BLOCK_DIM:] + 0.5
        log_a = half_scale[:, sl] * th[:, :LRU_BLOCK_DIM] + half_scale[:, sl]
        a = jnp.exp(log_a)
        a_ref[:, sl] = a
        one_minus_a2 = -jnp.tanh(log_a) * (a * a + 1.0)
        root = jnp.where(one_minus_a2 > 0.0, one_minus_a2 * lax.rsqrt(one_minus_a2), 0.0)
        bt_ref[:, sl] = root * ig * c[:, sl]

    def scan_step(t, h):
        tt = (LRU_STEPS - 1 - t) if reverse else t
        rs = pl.ds(pl.multiple_of(tt * batch, batch), batch)
        h = a_ref[rs, :] * h + bt_ref[rs, :]
        if reverse:
            o_ref[rs, :] = ((h + hf_ref[rs, :]) * _gelu_tanh(xg_ref[rs, :])).astype(o_ref.dtype)
        else:
            o_ref[rs, :] = h
        return h

    h_ref[...] = lax.fori_loop(0, LRU_STEPS, scan_step, h_ref[...], unroll=4)


def _lru_scan(xr2, conv_w, conv_b, wcat, bcat, lam, *, batch, reverse, h_fwd=None, xg2=None):
    n_rows, width = xr2.shape
    rows = LRU_STEPS * batch
    n = n_rows // rows
    ch = (lambda i: n - 1 - i) if reverse else (lambda i: i)
    tile = pl.BlockSpec((rows, width), lambda i: (ch(i), 0))
    prev = pl.BlockSpec((2 * batch, width), lambda i: (jnp.maximum(ch(i) * (LRU_STEPS // 2) - 1, 0), 0))
    nxt = pl.BlockSpec((batch, width),
                       lambda i: (jnp.minimum((ch(i) + 1) * LRU_STEPS, n * LRU_STEPS - 1), 0))
    full = lambda a: pl.BlockSpec(a.shape, lambda i: (0,) * a.ndim)
    ins = [xr2, xr2, xr2, conv_w, conv_b, wcat, bcat, lam]
    specs = [tile, prev, nxt, full(conv_w), full(conv_b), full(wcat), full(bcat), full(lam)]
    if reverse:
        ins += [h_fwd, xg2]
        specs += [tile, tile]
    return pl.pallas_call(
        functools.partial(_lru_kernel, batch=batch, reverse=reverse),
        grid=(n,),
        in_specs=specs,
        out_specs=tile,
        out_shape=jax.ShapeDtypeStruct((n_rows, width), BF16 if reverse else F32),
        scratch_shapes=[
            pltpu.VMEM((rows + 3 * batch, width), F32),
            pltpu.VMEM((rows, width), F32),
            pltpu.VMEM((rows, width), F32),
            pltpu.VMEM((batch, width), F32),
        ],
        compiler_params=_params("arbitrary"),
        name="lru_bwd" if reverse else "lru_fwd",
    )(*ins)


def _merge_kernel(x_ref, at_ref, yl_ref, ga_ref, gr_ref, wa_ref, wl_ref, wo_ref, gm_ref, wr_ref, br_ref,
                  h_ref, u_ref, idx_ref, gate_ref, cnt_ref):
    y_attn = _bdot(at_ref[...], wa_ref[...])
    y_lru = _bdot(yl_ref[...], wl_ref[...])
    merged = _sigmoid(ga_ref[...]) * y_attn + _sigmoid(gr_ref[...]) * y_lru
    h = x_ref[...] + _bdot(merged.astype(BF16), wo_ref[...])
    h_ref[...] = h
    u = _rms(h, gm_ref[...])
    _store_row_tiles(u_ref, u)
    u_hi = u.astype(BF16)
    u_lo = (u - u_hi.astype(F32)).astype(BF16)
    logits = _bdot(u_hi, wr_ref[0]) + (_bdot(u_lo, wr_ref[0]) + _bdot(u_hi, wr_ref[1]))
    lane = lax.broadcasted_iota(jnp.int32, logits.shape, 1)
    work = jnp.where(lane < N_EXPERTS, logits + br_ref[...], -jnp.inf)
    vals, hits = [], jnp.zeros(logits.shape, F32)
    for kk in range(TOP_K):
        m = jnp.max(work, axis=-1, keepdims=True)
        idx = jnp.min(jnp.where(work == m, lane, V7X_LANES), axis=-1, keepdims=True)
        pick = lane == idx
        work = jnp.where(pick, -jnp.inf, work)
        hits = hits + pick.astype(F32)
        vals.append(m)
        idx_ref[:, kk:kk + 1] = idx
    es = [jnp.exp(vv - vals[0]) for vv in vals]
    inv = 1.0 / sum(es)
    for kk in range(TOP_K):
        gate_ref[:, kk:kk + 1] = es[kk] * inv

    @pl.when((pl.program_id(0) == 0) & (pl.program_id(1) == 0))
    def _():
        cnt_ref[...] = jnp.zeros_like(cnt_ref)

    cnt_ref[0:1, :] += jnp.sum(hits, axis=0, keepdims=True)


def _merge(x, attn, ylru, ga, gr, wa, wl, wo, g_moe, wr, br):
    B, S, D = x.shape
    tm = TOKEN_TILE
    bm = lambda w: pl.BlockSpec((None, tm, w), lambda b, i: (b, i, 0))
    full = lambda a: pl.BlockSpec(a.shape, lambda b, i: (0,) * a.ndim)
    return pl.pallas_call(
        _merge_kernel,
        grid=(B, S // tm),
        in_specs=[bm(D), bm(Q_W), bm(LRU_WIDTH), bm(D), bm(D), full(wa), full(wl), full(wo),
                  full(g_moe), full(wr), full(br)],
        out_specs=[bm(D), pl.BlockSpec((None, tm * ROW_TILE, V7X_LANES), lambda b, i: (b, i, 0)),
                   bm(TOP_K), bm(TOP_K), pl.BlockSpec((8, V7X_LANES), lambda b, i: (0, 0))],
        out_shape=[
            jax.ShapeDtypeStruct((B, S, D), F32),
            jax.ShapeDtypeStruct((B, S * ROW_TILE, V7X_LANES), F32),
            jax.ShapeDtypeStruct((B, S, TOP_K), jnp.int32),
            jax.ShapeDtypeStruct((B, S, TOP_K), F32),
            jax.ShapeDtypeStruct((8, V7X_LANES), F32),
        ],
        compiler_params=_params("arbitrary", "arbitrary"),
        name="merge_router",
    )(x, attn, ylru, ga, gr, wa, wl, wo, g_moe, wr, br)


def _rank_kernel(idx_ref, ps_ref, dest_ref, carry_ref):
    @pl.when(pl.program_id(0) == 0)
    def _():
        carry_ref[...] = jnp.zeros_like(carry_ref)

    tm = idx_ref.shape[0]
    lane = lax.broadcasted_iota(jnp.int32, (tm, V7X_LANES), 1)
    picks = [lane == idx_ref[:, kk:kk + 1] for kk in range(TOP_K)]
    hits = sum(p.astype(F32) for p in picks)
    earlier = (lax.broadcasted_iota(jnp.int32, (tm, tm), 0) >
               lax.broadcasted_iota(jnp.int32, (tm, tm), 1)).astype(BF16)
    base = _bdot(earlier, hits.astype(BF16)) + carry_ref[...] + ps_ref[...]
    for kk in range(TOP_K):
        dest_ref[:, kk:kk + 1] = jnp.sum(jnp.where(picks[kk], base, 0.0), axis=-1,
                                         keepdims=True).astype(jnp.int32)
    carry_ref[...] += jnp.sum(hits, axis=0, keepdims=True)


def _rank(idx, pad_start_row):
    T = idx.shape[0]
    tm = RANK_TILE
    return pl.pallas_call(
        _rank_kernel,
        grid=(T // tm,),
        in_specs=[pl.BlockSpec((tm, TOP_K), lambda i: (i, 0)),
                  pl.BlockSpec((1, V7X_LANES), lambda i: (0, 0))],
        out_specs=pl.BlockSpec((tm, TOP_K), lambda i: (i, 0)),
        out_shape=jax.ShapeDtypeStruct((T, TOP_K), jnp.int32),
        scratch_shapes=[pltpu.VMEM((1, V7X_LANES), F32)],
        compiler_params=_params("arbitrary"),
        name="slot_rank",
    )(idx, pad_start_row)


def _invert_kernel(pad_end_ref, count_ref, dest_vmem, out_vmem, dest_ref, inv_ref, sem, *, n_tokens, chunk):
    step = pl.program_id(0)
    stage_in = pltpu.make_async_copy(dest_vmem, dest_ref, sem)
    stage_in.start()
    stage_in.wait()

    @pl.when(step == 0)
    def _():
        def fill(s, carry):
            inv_ref[s] = TOP_K * n_tokens + (s & (SPARE_ROWS - 1))
            return carry

        for e in range(N_EXPERTS):
            lo = pad_end_ref[e] - ((count_ref[e] + EXPERT_TILE - 1) & -EXPERT_TILE) + count_ref[e]
            lax.fori_loop(lo, pad_end_ref[e], fill, 0)
        lax.fori_loop(pad_end_ref[N_EXPERTS - 1], inv_ref.shape[0], fill, 0)

    group = 4 * TOP_K

    def body(g, carry):
        tok0 = step * (chunk // TOP_K) + g * (group // TOP_K)
        for u in range(group):
            inv_ref[dest_ref[g * group + u]] = tok0 + ((u % TOP_K) * n_tokens + u // TOP_K)
        return carry

    lax.fori_loop(0, chunk // group, body, 0)

    @pl.when(step == pl.num_programs(0) - 1)
    def _():
        stage_out = pltpu.make_async_copy(inv_ref, out_vmem, sem)
        stage_out.start()
        stage_out.wait()


def _invert(pad_end, counts, dest_flat, cap, n_tokens):
    chunk = INVERT_CHUNK
    return pl.pallas_call(
        functools.partial(_invert_kernel, n_tokens=n_tokens, chunk=chunk),
        grid_spec=pltpu.PrefetchScalarGridSpec(
            num_scalar_prefetch=2,
            grid=(dest_flat.shape[0] // chunk,),
            in_specs=[pl.BlockSpec((chunk,), lambda i, pe, ct: (i,))],
            out_specs=pl.BlockSpec((cap,), lambda i, pe, ct: (0,)),
            scratch_shapes=[pltpu.SMEM((chunk,), jnp.int32), pltpu.SMEM((cap,), jnp.int32),
                            pltpu.SemaphoreType.DMA(())],
        ),
        out_shape=jax.ShapeDtypeStruct((cap,), jnp.int32),
        compiler_params=_params("arbitrary"),
        name="slot_invert",
    )(pad_end, counts, dest_flat)


def _expert_kernel(be_ref, nu_ref, invp_ref, invc_ref, invn_ref, u_hbm, wgu_ref, bgu_ref, wdn_ref, bdn_ref,
                   y_hbm, xbuf0, xbuf1, ybuf0, ybuf1, gsem, ssem, *, n_tokens):
    xbuf, ybuf = (xbuf0, xbuf1), (ybuf0, ybuf1)
    j = pl.program_id(0)
    n_used = nu_ref[0]
    parity = j & 1

    def tile_rows(row):
        if isinstance(row, int):
            return pl.ds(row * ROW_TILE, ROW_TILE)
        return pl.ds(pl.multiple_of(row * ROW_TILE, ROW_TILE), ROW_TILE)

    def gather(inv_ref, r, buf, dep=0):
        tok = (inv_ref[0, 0, r] & (n_tokens - 1)) + dep
        return pltpu.make_async_copy(u_hbm.at[tile_rows(tok)], xbuf[buf].at[tile_rows(r)], gsem.at[buf])

    def scatter(inv_ref, r, buf, dep=0):
        return pltpu.make_async_copy(ybuf[buf].at[tile_rows(r)], y_hbm.at[tile_rows(inv_ref[0, 0, r] + dep)],
                                     ssem.at[buf])

    def wait_gather(buf):
        pltpu.make_async_copy(xbuf[buf], xbuf[buf], gsem.at[buf]).wait()

    def wait_scatter(buf):
        pltpu.make_async_copy(ybuf[buf], ybuf[buf], ssem.at[buf]).wait()

    @pl.when(j == 0)
    def _():
        def first(r, carry):
            gather(invc_ref, r, 0).start()
            return carry
        lax.fori_loop(0, EXPERT_TILE, first, 0)
        xbuf[1][...] = jnp.zeros(xbuf[1].shape, xbuf[1].dtype)
        ybuf[1][...] = jnp.zeros(ybuf[1].shape, ybuf[1].dtype)
        clears = [pltpu.make_async_copy(
            xbuf[1], y_hbm.at[pl.ds((TOP_K * n_tokens + c * EXPERT_TILE) * ROW_TILE, EXPERT_TILE * ROW_TILE)],
            ssem.at[1]) for c in range(SPARE_ROWS // EXPERT_TILE)]
        for c in clears:
            c.start()
        for c in clears:
            c.wait()

    def block(cur):
        oth = 1 - cur
        wait_gather(cur)

        @pl.when(j > 0)
        def _():
            wait_scatter(cur)

        x = _load_row_tiles(xbuf[cur]).astype(BF16)
        per = EXPERT_TILE // EXPERT_SLICES
        fc = D_EXPERT // EXPERT_SLICES
        acc = None
        dep = 0
        for c in range(EXPERT_SLICES):
            for r in range(c * per, (c + 1) * per):
                gather(invn_ref, r, oth, dep).start(priority=r % 2)
            for r in range(c * per, (c + 1) * per):
                scatter(invp_ref, r, oth, dep).start(priority=r % 2)
            gs = slice(c * fc, (c + 1) * fc)
            us = slice(D_EXPERT + c * fc, D_EXPERT + (c + 1) * fc)
            gate = jnp.minimum(_bdot(x, wgu_ref[:, gs].astype(BF16)) + bgu_ref[:, gs], SWIGLU_LIMIT)
            up = jnp.clip(_bdot(x, wgu_ref[:, us].astype(BF16)) + bgu_ref[:, us], -SWIGLU_LIMIT, SWIGLU_LIMIT)
            act = gate * _sigmoid(SWIGLU_ALPHA * gate)
            part = _bdot(((up + 1.0) * act).astype(BF16), wdn_ref[gs, :].astype(BF16))
            acc = part if acc is None else acc + part
            bits = pltpu.bitcast(part[0:8, 0:V7X_LANES], jnp.uint32)[0, 0]
            dep = lax.shift_right_logical(lax.shift_right_logical(bits, jnp.uint32(16)),
                                          jnp.uint32(16)).astype(jnp.int32)
        _store_row_tiles(ybuf[cur], acc + bdn_ref[...])

        @pl.when(j == n_used - 1)
        def _():
            wait_scatter(oth)
            wait_gather(oth)

            def last(r, carry):
                scatter(invc_ref, r, cur).start()
                return carry
            lax.fori_loop(0, EXPERT_TILE, last, 0)
            wait_scatter(cur)

    for cur in range(2):
        @pl.when((j < n_used) & (parity == cur))
        def _():
            block(cur)


def _experts(blk_exp, n_used, inv3, u, w_gu, b_gu, w_dn, b_dn):
    T, D = u.shape[0] // ROW_TILE, D_MODEL
    n_blocks = inv3.shape[0]
    exp3 = lambda j, be, nu: (be[j], 0, 0)
    inv_spec = lambda f: pl.BlockSpec((1, 1, EXPERT_TILE), lambda j, be, nu: (f(j, nu), 0, 0),
                                      memory_space=pltpu.SMEM)
    last = lambda j, nu: jnp.minimum(j, nu[0] - 1)
    return pl.pallas_call(
        functools.partial(_expert_kernel, n_tokens=T),
        grid_spec=pltpu.PrefetchScalarGridSpec(
            num_scalar_prefetch=2,
            grid=(n_blocks,),
            in_specs=[inv_spec(lambda j, nu: jnp.maximum(last(j, nu) - 1, 0)),
                      inv_spec(last),
                      inv_spec(lambda j, nu: jnp.minimum(j + 1, nu[0] - 1)),
                      pl.BlockSpec(memory_space=pl.ANY),
                      pl.BlockSpec((None, D, 2 * D_EXPERT), exp3),
                      pl.BlockSpec((None, 1, 2 * D_EXPERT), exp3),
                      pl.BlockSpec((None, D_EXPERT, D), exp3),
                      pl.BlockSpec((None, 1, D), exp3)],
            out_specs=pl.BlockSpec(memory_space=pl.ANY),
            scratch_shapes=[pltpu.VMEM((EXPERT_TILE * ROW_TILE, V7X_LANES), F32)] * 4 + [
                pltpu.SemaphoreType.DMA((2,)), pltpu.SemaphoreType.DMA((2,))],
        ),
        out_shape=jax.ShapeDtypeStruct(((TOP_K * T + SPARE_ROWS) * ROW_TILE, V7X_LANES), F32),
        compiler_params=_params("arbitrary"),
        name="experts",
    )(blk_exp, n_used, inv3, inv3, inv3, u, w_gu, b_gu, w_dn, b_dn)


def _combine_kernel(y0_ref, y1_ref, y2_ref, y3_ref, gate_ref, h_ref, p_ref, gp_ref, wg_ref, wp_ref, o_ref):
    h = h_ref[...]
    for kk, y_ref in enumerate((y0_ref, y1_ref, y2_ref, y3_ref)):
        h = h + gate_ref[:, kk:kk + 1] * _load_row_tiles(y_ref)
    u = _rms(h, gp_ref[...]).astype(BF16)
    o_ref[...] = h + _sigmoid(_bdot(u, wg_ref[...])) * _bdot(p_ref[...].astype(BF16), wp_ref[...])


def _combine(y4, gate, h, p, g_ple, wg, wp):
    T, D = h.shape
    tm = TOKEN_TILE
    full = lambda a: pl.BlockSpec(a.shape, lambda i: (0,) * a.ndim)
    plane = lambda kk: pl.BlockSpec((tm * ROW_TILE, V7X_LANES), lambda i: (kk * (T // tm) + i, 0))
    return pl.pallas_call(
        _combine_kernel,
        grid=(T // tm,),
        in_specs=[plane(0), plane(1), plane(2), plane(3),
                  pl.BlockSpec((tm, TOP_K), lambda i: (i, 0)),
                  pl.BlockSpec((tm, D), lambda i: (i, 0)),
                  pl.BlockSpec((tm, PLE_DIM), lambda i: (i, 0)),
                  full(g_ple), full(wg), full(wp)],
        out_specs=pl.BlockSpec((tm, D), lambda i: (i, 0)),
        out_shape=jax.ShapeDtypeStruct((T, D), F32),
        compiler_params=_params("arbitrary"),
        name="combine_ple",
    )(y4, y4, y4, y4, gate, h, p, g_ple, wg, wp)


def _rope_tables(seq_len):
    pos = np.arange(seq_len)
    axis_dim = HEAD_DIM // 2
    inv_freq = (ROPE_THETA ** (-np.arange(0, axis_dim, 2, dtype=np.float32) / axis_dim)).astype(np.float32)
    ang_r = (pos // GRID_W).astype(np.float32)[:, None] * inv_freq[None, :]
    ang_c = (pos % GRID_W).astype(np.float32)[:, None] * inv_freq[None, :]
    zero = np.zeros_like(ang_r)
    cos = np.concatenate([np.cos(ang_r)] * 2 + [np.cos(ang_c)] * 2, axis=-1)
    sina = np.concatenate([-np.sin(ang_r), zero, -np.sin(ang_c), zero], axis=-1)
    sinb = np.concatenate([zero, np.sin(ang_r), zero, np.sin(ang_c)], axis=-1)
    return jnp.asarray(cos, F32), jnp.asarray(sina, F32), jnp.asarray(sinb, F32)


def _layer(h, p, g_mix, w_in, q_norm, k_norm, conv_w, conv_b, lru_wa, lru_ba, lru_wi, lru_bi, lru_lam,
           w_attn_br, w_lru_br, w_out, g_moe, w_router, b_router, w_gu, b_gu, w_dn, b_dn,
           g_ple, w_ple_gate, w_ple_proj, rope):
    B, S, D = h.shape
    T = B * S
    assert S % ATTN_Q_TILE == 0 and S % TOKEN_TILE == 0 and S % LRU_STEPS == 0 and B % 8 == 0
    assert T % RANK_TILE == 0 and (T * TOP_K) % INVERT_CHUNK == 0
    row = lambda a: a.reshape(1, -1)
    q, k, v, xr, xg, ga, gr = _in_proj(
        h, row(g_mix), w_in.astype(BF16), row(q_norm) * (HEAD_DIM ** -0.5 * np.log2(np.e)), row(k_norm), *rope)
    attn = _attention(q, k, v)
    xr2 = xr.transpose(1, 0, 2).reshape(S * B, LRU_WIDTH)
    xg2 = xg.transpose(1, 0, 2).reshape(S * B, LRU_WIDTH)
    wcat = (0.5 * jnp.concatenate([lru_wa, lru_wi], axis=-1)).astype(BF16)
    bcat = 0.5 * jnp.concatenate([lru_ba.reshape(2, LRU_BLOCKS, 1, LRU_BLOCK_DIM),
                                  lru_bi.reshape(2, LRU_BLOCKS, 1, LRU_BLOCK_DIM)], axis=-1)
    h_fwd = _lru_scan(xr2, conv_w, row(conv_b), wcat[0], bcat[0], row(lru_lam[0]), batch=B, reverse=False)
    ylru = _lru_scan(xr2, conv_w, row(conv_b), wcat[1], bcat[1], row(lru_lam[1]), batch=B, reverse=True,
                     h_fwd=h_fwd, xg2=xg2).reshape(S, B, LRU_WIDTH).transpose(1, 0, 2)
    w_router_f = jnp.zeros((D, V7X_LANES), F32).at[:, :N_EXPERTS].set(w_router)
    w_router_hi = w_router_f.astype(BF16)
    w_router_p = jnp.stack([w_router_hi, (w_router_f - w_router_hi.astype(F32)).astype(BF16)])
    b_router_p = jnp.zeros((1, V7X_LANES), F32).at[0, :N_EXPERTS].set(b_router)
    h1, u2, idx, gate, cnt = _merge(h, attn, ylru, ga, gr, w_attn_br.astype(BF16), w_lru_br.astype(BF16),
                                    w_out.astype(BF16), row(g_moe), w_router_p, b_router_p)
    cap = T * TOP_K + N_EXPERTS * EXPERT_TILE
    counts = cnt[0, :N_EXPERTS].astype(jnp.int32)
    padded = ((counts + EXPERT_TILE - 1) // EXPERT_TILE) * EXPERT_TILE
    pad_end = jnp.cumsum(padded)
    pad_start = pad_end - padded
    blk_start = jnp.arange(cap // EXPERT_TILE, dtype=jnp.int32) * EXPERT_TILE
    blk_exp = jnp.minimum(jnp.sum(blk_start[:, None] >= pad_end[None, :], axis=1), N_EXPERTS - 1)
    n_used = (pad_end[-1:] // EXPERT_TILE).astype(jnp.int32)
    ps_row = jnp.zeros((1, V7X_LANES), F32).at[0, :N_EXPERTS].set(pad_start.astype(F32))
    dest = _rank(idx.reshape(T, TOP_K), ps_row).reshape(T * TOP_K)
    assert T & (T - 1) == 0, "token count must be a power of two for the slot encoding"
    inv3 = _invert(pad_end.astype(jnp.int32), counts, dest, cap, T).reshape(cap // EXPERT_TILE, 1, EXPERT_TILE)
    y4 = _experts(blk_exp.astype(jnp.int32), n_used, inv3, u2.reshape(T * ROW_TILE, V7X_LANES), w_gu,
                  b_gu.reshape(N_EXPERTS, 1, -1), w_dn, b_dn.reshape(N_EXPERTS, 1, -1))
    out = _combine(y4, gate.reshape(T, TOP_K), h1.reshape(T, D),
                   p.reshape(T, PLE_DIM), row(g_ple), w_ple_gate.astype(BF16), w_ple_proj.astype(BF16))
    return out.reshape(B, S, D)


def kernel(x, p, g_mix, w_in, q_norm, k_norm, conv_w, conv_b, lru_wa, lru_ba, lru_wi, lru_bi, lru_lam,
           w_attn_br, w_lru_br, w_out, g_moe, w_router, b_router, w_gu, b_gu, w_dn, b_dn,
           g_ple, w_ple_gate, w_ple_proj):
    rope = _rope_tables(x.shape[1])
    h = x
    for l in range(p.shape[0]):
        h = _layer(h, p[l], g_mix[l], w_in[l], q_norm[l], k_norm[l], conv_w[l], conv_b[l], lru_wa[l],
                   lru_ba[l], lru_wi[l], lru_bi[l], lru_lam[l], w_attn_br[l], w_lru_br[l], w_out[l],
                   g_moe[l], w_router[l], b_router[l], w_gu[l], b_gu[l], w_dn[l], b_dn[l],
                   g_ple[l], w_ple_gate[l], w_ple_proj[l], rope)
    return h
```

```python
import functools

import jax
import jax.numpy as jnp
import numpy as np
from jax import lax
from jax.experimental import pallas as pl
from jax.experimental.pallas import tpu as pltpu

D_MODEL = 1024
N_Q_HEADS = 8
N_KV_HEADS = 2
HEAD_DIM = 128
Q_GROUP = N_Q_HEADS // N_KV_HEADS
ROPE_THETA = 10000.0
GRID_W = 64
LRU_WIDTH = 1024
LRU_BLOCKS = 8
LRU_BLOCK_DIM = LRU_WIDTH // LRU_BLOCKS
LRU_C = 8.0
N_EXPERTS = 32
TOP_K = 4
D_EXPERT = 1024
SWIGLU_LIMIT = 7.0
SWIGLU_ALPHA = 1.702
PLE_DIM = 256
EPS = 1e-6
Q_W = N_Q_HEADS * HEAD_DIM
KV_W = N_KV_HEADS * HEAD_DIM
IN_WIDTH = Q_W + 2 * KV_W + 2 * LRU_WIDTH + 2 * D_MODEL
OFF_Q = 0
OFF_K = OFF_Q + Q_W
OFF_V = OFF_K + KV_W
OFF_XR = OFF_V + KV_W
OFF_XG = OFF_XR + LRU_WIDTH
OFF_GA = OFF_XG + LRU_WIDTH
OFF_GR = OFF_GA + D_MODEL

V7X_LANES = 128
V7X_VMEM_LIMIT_BYTES = 56 * 1024 * 1024
TOKEN_TILE = 512
ATTN_Q_TILE = 1024
LRU_STEPS = 64
EXPERT_TILE = 512
EXPERT_SLICES = 2
RANK_TILE = 512
INVERT_CHUNK = 8192
SPARE_ROWS = 1024

F32 = jnp.float32
BF16 = jnp.bfloat16


def _params(*sem):
    return pltpu.CompilerParams(dimension_semantics=sem, vmem_limit_bytes=V7X_VMEM_LIMIT_BYTES)


def _bdot(a, b):
    return jnp.dot(a, b, preferred_element_type=F32)


def _rms(x, g):
    return x * lax.rsqrt(jnp.mean(x * x, axis=-1, keepdims=True) + EPS) * g


ROW_TILE = 8


def _store_row_tiles(ref, x):
    n = x.shape[0]
    for j in range(ROW_TILE):
        ref[pl.ds(j, n, stride=ROW_TILE), :] = x[:, j * V7X_LANES:(j + 1) * V7X_LANES]


def _load_row_tiles(ref):
    n = ref.shape[0] // ROW_TILE
    return jnp.concatenate([ref[pl.ds(j, n, stride=ROW_TILE), :] for j in range(ROW_TILE)], axis=-1)


def _sigmoid(x):
    return 0.5 * jnp.tanh(0.5 * x) + 0.5


def _in_proj_kernel(x_ref, g_ref, w_ref, qn_ref, kn_ref, cos_ref, sina_ref, sinb_ref,
                    q_ref, k_ref, v_ref, xr_ref, xg_ref, ga_ref, gr_ref):
    u = _rms(x_ref[...], g_ref[...]).astype(BF16)
    cos, sina, sinb = cos_ref[...], sina_ref[...], sinb_ref[...]

    def head(z, gain):
        y = _rms(z, gain)
        return y * cos + pltpu.roll(y, 96, 1) * sina + pltpu.roll(y, 32, 1) * sinb

    zq = _bdot(u, w_ref[:, OFF_Q:OFF_Q + Q_W])
    for h in range(N_Q_HEADS):
        sl = slice(h * HEAD_DIM, (h + 1) * HEAD_DIM)
        q_ref[:, sl] = head(zq[:, sl], qn_ref[...]).astype(BF16)
    zk = _bdot(u, w_ref[:, OFF_K:OFF_K + KV_W])
    for h in range(N_KV_HEADS):
        sl = slice(h * HEAD_DIM, (h + 1) * HEAD_DIM)
        k_ref[:, sl] = head(zk[:, sl], kn_ref[...]).astype(BF16)
    v_ref[...] = _bdot(u, w_ref[:, OFF_V:OFF_V + KV_W]).astype(BF16)
    xr_ref[...] = _bdot(u, w_ref[:, OFF_XR:OFF_XR + LRU_WIDTH])
    xg_ref[...] = _bdot(u, w_ref[:, OFF_XG:OFF_XG + LRU_WIDTH])
    ga_ref[...] = _bdot(u, w_ref[:, OFF_GA:OFF_GA + D_MODEL])
    gr_ref[...] = _bdot(u, w_ref[:, OFF_GR:OFF_GR + D_MODEL])


def _in_proj(x, g_mix, w_in, qn, kn, cos, sina, sinb):
    B, S, D = x.shape
    tm = TOKEN_TILE
    bm = lambda w: pl.BlockSpec((None, tm, w), lambda b, i: (b, i, 0))
    full = lambda a: pl.BlockSpec(a.shape, lambda b, i: (0,) * a.ndim)
    tab = pl.BlockSpec((tm, HEAD_DIM), lambda b, i: (i, 0))
    return pl.pallas_call(
        _in_proj_kernel,
        grid=(B, S // tm),
        in_specs=[bm(D), full(g_mix), full(w_in), full(qn), full(kn), tab, tab, tab],
        out_specs=[bm(Q_W), bm(KV_W), bm(KV_W), bm(LRU_WIDTH), bm(LRU_WIDTH), bm(D), bm(D)],
        out_shape=[
            jax.ShapeDtypeStruct((B, S, Q_W), BF16),
            jax.ShapeDtypeStruct((B, S, KV_W), BF16),
            jax.ShapeDtypeStruct((B, S, KV_W), BF16),
            jax.ShapeDtypeStruct((B, S, LRU_WIDTH), F32),
            jax.ShapeDtypeStruct((B, S, LRU_WIDTH), F32),
            jax.ShapeDtypeStruct((B, S, D), F32),
            jax.ShapeDtypeStruct((B, S, D), F32),
        ],
        compiler_params=_params("arbitrary", "arbitrary"),
        name="in_proj",
    )(x, g_mix, w_in, qn, kn, cos, sina, sinb)


def _attn_kernel(q_ref, k_ref, v_ref, o_ref):
    for g in range(N_KV_HEADS):
        ks = slice(g * HEAD_DIM, (g + 1) * HEAD_DIM)
        k = k_ref[:, ks]
        v1 = jnp.concatenate([v_ref[:, ks], jnp.ones((v_ref.shape[0], HEAD_DIM), v_ref.dtype)], axis=-1)
        for h in range(g * Q_GROUP, (g + 1) * Q_GROUP):
            sl = slice(h * HEAD_DIM, (h + 1) * HEAD_DIM)
            s = lax.dot_general(q_ref[:, sl], k, (((1,), (1,)), ((), ())), preferred_element_type=F32)
            p = jnp.exp2(s - jnp.max(s, axis=-1, keepdims=True))
            o = _bdot(p.astype(BF16), v1)
            o_ref[:, sl] = (o[:, :HEAD_DIM] * (1.0 / o[:, HEAD_DIM:HEAD_DIM + 1])).astype(BF16)


def _attention(q, k, v):
    B, S, _ = q.shape
    tq = ATTN_Q_TILE
    return pl.pallas_call(
        _attn_kernel,
        grid=(B, S // tq),
        in_specs=[
            pl.BlockSpec((None, tq, Q_W), lambda b, i: (b, i, 0)),
            pl.BlockSpec((None, S, KV_W), lambda b, i: (b, 0, 0)),
            pl.BlockSpec((None, S, KV_W), lambda b, i: (b, 0, 0)),
        ],
        out_specs=pl.BlockSpec((None, tq, Q_W), lambda b, i: (b, i, 0)),
        out_shape=jax.ShapeDtypeStruct((B, S, Q_W), BF16),
        compiler_params=_params("arbitrary", "arbitrary"),
        name="attention",
    )(q, k, v)


def _gelu_tanh(x):
    return 0.5 * x * (1.0 + jnp.tanh(np.sqrt(2.0 / np.pi) * (x + 0.044715 * (x * x * x))))


def _lru_kernel(*refs, batch, reverse):
    if reverse:
        (xc_ref, xp_ref, xn_ref, cw_ref, cb_ref, w_ref, b_ref, lam_ref, hf_ref, xg_ref,
         o_ref, ext_ref, a_ref, bt_ref, h_ref) = refs
    else:
        (xc_ref, xp_ref, xn_ref, cw_ref, cb_ref, w_ref, b_ref, lam_ref,
         o_ref, ext_ref, a_ref, bt_ref, h_ref) = refs
    step = pl.program_id(0)
    nsteps = pl.num_programs(0)
    chunk = (nsteps - 1 - step) if reverse else step
    rows = LRU_STEPS * batch

    @pl.when(step == 0)
    def _():
        h_ref[...] = jnp.zeros_like(h_ref)

    ext_ref[0:2 * batch, :] = jnp.where(chunk > 0, xp_ref[...], 0.0)
    ext_ref[2 * batch:2 * batch + rows, :] = xc_ref[...]
    ext_ref[2 * batch + rows:3 * batch + rows, :] = jnp.where(chunk < nsteps - 1, xn_ref[...], 0.0)
    c = cb_ref[...] + sum(cw_ref[j:j + 1, :] * ext_ref[j * batch:j * batch + rows, :] for j in range(4))
    cb16 = c.astype(BF16)
    lam = lam_ref[...]
    half_scale = (-0.5 * LRU_C) * (jnp.maximum(-lam, 0.0) + jnp.log1p(jnp.exp(-jnp.abs(lam))))
    for n in range(LRU_BLOCKS):
        sl = slice(n * LRU_BLOCK_DIM, (n + 1) * LRU_BLOCK_DIM)
        th = jnp.tanh(_bdot(cb16[:, sl], w_ref[n]) + b_ref[n])
        ig = 0.5 * th[:, LRU_BLOCK_DIM:] + 0.5
        log_a = half_scale[:, sl] * th[:, :LRU_BLOCK_DIM] + half_scale[:, sl]
        a = jnp.exp(log_a)
        a_ref[:, sl] = a
        one_minus_a2 = -jnp.tanh(log_a) * (a * a + 1.0)
        root = jnp.where(one_minus_a2 > 0.0, one_minus_a2 * lax.rsqrt(one_minus_a2), 0.0)
        bt_ref[:, sl] = root * ig * c[:, sl]

    def scan_step(t, h):
        tt = (LRU_STEPS - 1 - t) if reverse else t
        rs = pl.ds(pl.multiple_of(tt * batch, batch), batch)
        h = a_ref[rs, :] * h + bt_ref[rs, :]
        if reverse:
            o_ref[rs, :] = ((h + hf_ref[rs, :]) * _gelu_tanh(xg_ref[rs, :])).astype(o_ref.dtype)
        else:
            o_ref[rs, :] = h
        return h

    h_ref[...] = lax.fori_loop(0, LRU_STEPS, scan_step, h_ref[...], unroll=4)


def _lru_scan(xr2, conv_w, conv_b, wcat, bcat, lam, *, batch, reverse, h_fwd=None, xg2=None):
    n_rows, width = xr2.shape
    rows = LRU_STEPS * batch
    n = n_rows // rows
    ch = (lambda i: n - 1 - i) if reverse else (lambda i: i)
    tile = pl.BlockSpec((rows, width), lambda i: (ch(i), 0))
    prev = pl.BlockSpec((2 * batch, width), lambda i: (jnp.maximum(ch(i) * (LRU_STEPS // 2) - 1, 0), 0))
    nxt = pl.BlockSpec((batch, width),
                       lambda i: (jnp.minimum((ch(i) + 1) * LRU_STEPS, n * LRU_STEPS - 1), 0))
    full = lambda a: pl.BlockSpec(a.shape, lambda i: (0,) * a.ndim)
    ins = [xr2, xr2, xr2, conv_w, conv_b, wcat, bcat, lam]
    specs = [tile, prev, nxt, full(conv_w), full(conv_b), full(wcat), full(bcat), full(lam)]
    if reverse:
        ins += [h_fwd, xg2]
        specs += [tile, tile]
    return pl.pallas_call(
        functools.partial(_lru_kernel, batch=batch, reverse=reverse),
        grid=(n,),
        in_specs=specs,
        out_specs=tile,
        out_shape=jax.ShapeDtypeStruct((n_rows, width), BF16 if reverse else F32),
        scratch_shapes=[
            pltpu.VMEM((rows + 3 * batch, width), F32),
            pltpu.VMEM((rows, width), F32),
            pltpu.VMEM((rows, width), F32),
            pltpu.VMEM((batch, width), F32),
        ],
        compiler_params=_params("arbitrary"),
        name="lru_bwd" if reverse else "lru_fwd",
    )(*ins)


def _merge_kernel(x_ref, at_ref, yl_ref, ga_ref, gr_ref, wa_ref, wl_ref, wo_ref, gm_ref, wr_ref, br_ref,
                  h_ref, u_ref, idx_ref, gate_ref, cnt_ref):
    y_attn = _bdot(at_ref[...], wa_ref[...])
    y_lru = _bdot(yl_ref[...], wl_ref[...])
    merged = _sigmoid(ga_ref[...]) * y_attn + _sigmoid(gr_ref[...]) * y_lru
    h = x_ref[...] + _bdot(merged.astype(BF16), wo_ref[...])
    h_ref[...] = h
    u = _rms(h, gm_ref[...])
    _store_row_tiles(u_ref, u)
    u_hi = u.astype(BF16)
    u_lo = (u - u_hi.astype(F32)).astype(BF16)
    logits = _bdot(u_hi, wr_ref[0]) + (_bdot(u_lo, wr_ref[0]) + _bdot(u_hi, wr_ref[1]))
    lane = lax.broadcasted_iota(jnp.int32, logits.shape, 1)
    work = jnp.where(lane < N_EXPERTS, logits + br_ref[...], -jnp.inf)
    vals, hits = [], jnp.zeros(logits.shape, F32)
    for kk in range(TOP_K):
        m = jnp.max(work, axis=-1, keepdims=True)
        idx = jnp.min(jnp.where(work == m, lane, V7X_LANES), axis=-1, keepdims=True)
        pick = lane == idx
        work = jnp.where(pick, -jnp.inf, work)
        hits = hits + pick.astype(F32)
        vals.append(m)
        idx_ref[:, kk:kk + 1] = idx
    es = [jnp.exp(vv - vals[0]) for vv in vals]
    inv = 1.0 / sum(es)
    for kk in range(TOP_K):
        gate_ref[:, kk:kk + 1] = es[kk] * inv

    @pl.when((pl.program_id(0) == 0) & (pl.program_id(1) == 0))
    def _():
        cnt_ref[...] = jnp.zeros_like(cnt_ref)

    cnt_ref[0:1, :] += jnp.sum(hits, axis=0, keepdims=True)


def _merge(x, attn, ylru, ga, gr, wa, wl, wo, g_moe, wr, br):
    B, S, D = x.shape
    tm = TOKEN_TILE
    bm = lambda w: pl.BlockSpec((None, tm, w), lambda b, i: (b, i, 0))
    full = lambda a: pl.BlockSpec(a.shape, lambda b, i: (0,) * a.ndim)
    return pl.pallas_call(
        _merge_kernel,
        grid=(B, S // tm),
        in_specs=[bm(D), bm(Q_W), bm(LRU_WIDTH), bm(D), bm(D), full(wa), full(wl), full(wo),
                  full(g_moe), full(wr), full(br)],
        out_specs=[bm(D), pl.BlockSpec((None, tm * ROW_TILE, V7X_LANES), lambda b, i: (b, i, 0)),
                   bm(TOP_K), bm(TOP_K), pl.BlockSpec((8, V7X_LANES), lambda b, i: (0, 0))],
        out_shape=[
            jax.ShapeDtypeStruct((B, S, D), F32),
            jax.ShapeDtypeStruct((B, S * ROW_TILE, V7X_LANES), F32),
            jax.ShapeDtypeStruct((B, S, TOP_K), jnp.int32),
            jax.ShapeDtypeStruct((B, S, TOP_K), F32),
            jax.ShapeDtypeStruct((8, V7X_LANES), F32),
        ],
        compiler_params=_params("arbitrary", "arbitrary"),
        name="merge_router",
    )(x, attn, ylru, ga, gr, wa, wl, wo, g_moe, wr, br)


def _rank_kernel(idx_ref, ps_ref, dest_ref, carry_ref):
    @pl.when(pl.program_id(0) == 0)
    def _():
        carry_ref[...] = jnp.zeros_like(carry_ref)

    tm = idx_ref.shape[0]
    lane = lax.broadcasted_iota(jnp.int32, (tm, V7X_LANES), 1)
    picks = [lane == idx_ref[:, kk:kk + 1] for kk in range(TOP_K)]
    hits = sum(p.astype(F32) for p in picks)
    earlier = (lax.broadcasted_iota(jnp.int32, (tm, tm), 0) >
               lax.broadcasted_iota(jnp.int32, (tm, tm), 1)).astype(BF16)
    base = _bdot(earlier, hits.astype(BF16)) + carry_ref[...] + ps_ref[...]
    for kk in range(TOP_K):
        dest_ref[:, kk:kk + 1] = jnp.sum(jnp.where(picks[kk], base, 0.0), axis=-1,
                                         keepdims=True).astype(jnp.int32)
    carry_ref[...] += jnp.sum(hits, axis=0, keepdims=True)


def _rank(idx, pad_start_row):
    T = idx.shape[0]
    tm = RANK_TILE
    return pl.pallas_call(
        _rank_kernel,
        grid=(T // tm,),
        in_specs=[pl.BlockSpec((tm, TOP_K), lambda i: (i, 0)),
                  pl.BlockSpec((1, V7X_LANES), lambda i: (0, 0))],
        out_specs=pl.BlockSpec((tm, TOP_K), lambda i: (i, 0)),
        out_shape=jax.ShapeDtypeStruct((T, TOP_K), jnp.int32),
        scratch_shapes=[pltpu.VMEM((1, V7X_LANES), F32)],
        compiler_params=_params("arbitrary"),
        name="slot_rank",
    )(idx, pad_start_row)


def _invert_kernel(pad_end_ref, count_ref, dest_vmem, out_vmem, dest_ref, inv_ref, sem, *, n_tokens, chunk):
    step = pl.program_id(0)
    stage_in = pltpu.make_async_copy(dest_vmem, dest_ref, sem)
    stage_in.start()
    stage_in.wait()

    @pl.when(step == 0)
    def _():
        def fill(s, carry):
            inv_ref[s] = TOP_K * n_tokens + (s & (SPARE_ROWS - 1))
            return carry

        for e in range(N_EXPERTS):
            lo = pad_end_ref[e] - ((count_ref[e] + EXPERT_TILE - 1) & -EXPERT_TILE) + count_ref[e]
            lax.fori_loop(lo, pad_end_ref[e], fill, 0)
        lax.fori_loop(pad_end_ref[N_EXPERTS - 1], inv_ref.shape[0], fill, 0)

    group = 4 * TOP_K

    def body(g, carry):
        tok0 = step * (chunk // TOP_K) + g * (group // TOP_K)
        for u in range(group):
            inv_ref[dest_ref[g * group + u]] = tok0 + ((u % TOP_K) * n_tokens + u // TOP_K)
        return carry

    lax.fori_loop(0, chunk // group, body, 0)

    @pl.when(step == pl.num_programs(0) - 1)
    def _():
        stage_out = pltpu.make_async_copy(inv_ref, out_vmem, sem)
        stage_out.start()
        stage_out.wait()


def _invert(pad_end, counts, dest_flat, cap, n_tokens):
    chunk = INVERT_CHUNK
    return pl.pallas_call(
        functools.partial(_invert_kernel, n_tokens=n_tokens, chunk=chunk),
        grid_spec=pltpu.PrefetchScalarGridSpec(
            num_scalar_prefetch=2,
            grid=(dest_flat.shape[0] // chunk,),
            in_specs=[pl.BlockSpec((chunk,), lambda i, pe, ct: (i,))],
            out_specs=pl.BlockSpec((cap,), lambda i, pe, ct: (0,)),
            scratch_shapes=[pltpu.SMEM((chunk,), jnp.int32), pltpu.SMEM((cap,), jnp.int32),
                            pltpu.SemaphoreType.DMA(())],
        ),
        out_shape=jax.ShapeDtypeStruct((cap,), jnp.int32),
        compiler_params=_params("arbitrary"),
        name="slot_invert",
    )(pad_end, counts, dest_flat)


def _expert_kernel(be_ref, nu_ref, invp_ref, invc_ref, invn_ref, u_hbm, wgu_ref, bgu_ref, wdn_ref, bdn_ref,
                   y_hbm, xbuf0, xbuf1, ybuf0, ybuf1, gsem, ssem, *, n_tokens):
    xbuf, ybuf = (xbuf0, xbuf1), (ybuf0, ybuf1)
    j = pl.program_id(0)
    n_used = nu_ref[0]
    parity = j & 1

    def tile_rows(row):
        if isinstance(row, int):
            return pl.ds(row * ROW_TILE, ROW_TILE)
        return pl.ds(pl.multiple_of(row * ROW_TILE, ROW_TILE), ROW_TILE)

    def gather(inv_ref, r, buf, dep=0):
        tok = (inv_ref[0, 0, r] & (n_tokens - 1)) + dep
        return pltpu.make_async_copy(u_hbm.at[tile_rows(tok)], xbuf[buf].at[tile_rows(r)], gsem.at[buf])

    def scatter(inv_ref, r, buf, dep=0):
        return pltpu.make_async_copy(ybuf[buf].at[tile_rows(r)], y_hbm.at[tile_rows(inv_ref[0, 0, r] + dep)],
                                     ssem.at[buf])

    def wait_gather(buf):
        pltpu.make_async_copy(xbuf[buf], xbuf[buf], gsem.at[buf]).wait()

    def wait_scatter(buf):
        pltpu.make_async_copy(ybuf[buf], ybuf[buf], ssem.at[buf]).wait()

    @pl.when(j == 0)
    def _():
        def first(r, carry):
            gather(invc_ref, r, 0).start()
            return carry
        lax.fori_loop(0, EXPERT_TILE, first, 0)
        xbuf[1][...] = jnp.zeros(xbuf[1].shape, xbuf[1].dtype)
        ybuf[1][...] = jnp.zeros(ybuf[1].shape, ybuf[1].dtype)
        clears = [pltpu.make_async_copy(
            xbuf[1], y_hbm.at[pl.ds((TOP_K * n_tokens + c * EXPERT_TILE) * ROW_TILE, EXPERT_TILE * ROW_TILE)],
            ssem.at[1]) for c in range(SPARE_ROWS // EXPERT_TILE)]
        for c in clears:
            c.start()
        for c in clears:
            c.wait()

    def block(cur):
        oth = 1 - cur
        wait_gather(cur)

        @pl.when(j > 0)
        def _():
            wait_scatter(cur)

        x = _load_row_tiles(xbuf[cur]).astype(BF16)
        per = EXPERT_TILE // EXPERT_SLICES
        fc = D_EXPERT // EXPERT_SLICES
        acc = None
        dep = 0
        for c in range(EXPERT_SLICES):
            for r in range(c * per, (c + 1) * per):
                gather(invn_ref, r, oth, dep).start(priority=r % 2)
            for r in range(c * per, (c + 1) * per):
                scatter(invp_ref, r, oth, dep).start(priority=r % 2)
            gs = slice(c * fc, (c + 1) * fc)
            us = slice(D_EXPERT + c * fc, D_EXPERT + (c + 1) * fc)
            gate = jnp.minimum(_bdot(x, wgu_ref[:, gs].astype(BF16)) + bgu_ref[:, gs], SWIGLU_LIMIT)
            up = jnp.clip(_bdot(x, wgu_ref[:, us].astype(BF16)) + bgu_ref[:, us], -SWIGLU_LIMIT, SWIGLU_LIMIT)
            act = gate * _sigmoid(SWIGLU_ALPHA * gate)
            part = _bdot(((up + 1.0) * act).astype(BF16), wdn_ref[gs, :].astype(BF16))
            acc = part if acc is None else acc + part
            bits = pltpu.bitcast(part[0:8, 0:V7X_LANES], jnp.uint32)[0, 0]
            dep = lax.shift_right_logical(lax.shift_right_logical(bits, jnp.uint32(16)),
                                          jnp.uint32(16)).astype(jnp.int32)
        _store_row_tiles(ybuf[cur], acc + bdn_ref[...])

        @pl.when(j == n_used - 1)
        def _():
            wait_scatter(oth)
            wait_gather(oth)

            def last(r, carry):
                scatter(invc_ref, r, cur).start()
                return carry
            lax.fori_loop(0, EXPERT_TILE, last, 0)
            wait_scatter(cur)

    for cur in range(2):
        @pl.when((j < n_used) & (parity == cur))
        def _():
            block(cur)


def _experts(blk_exp, n_used, inv3, u, w_gu, b_gu, w_dn, b_dn):
    T, D = u.shape[0] // ROW_TILE, D_MODEL
    n_blocks = inv3.shape[0]
    exp3 = lambda j, be, nu: (be[j], 0, 0)
    inv_spec = lambda f: pl.BlockSpec((1, 1, EXPERT_TILE), lambda j, be, nu: (f(j, nu), 0, 0),
                                      memory_space=pltpu.SMEM)
    last = lambda j, nu: jnp.minimum(j, nu[0] - 1)
    return pl.pallas_call(
        functools.partial(_expert_kernel, n_tokens=T),
        grid_spec=pltpu.PrefetchScalarGridSpec(
            num_scalar_prefetch=2,
            grid=(n_blocks,),
            in_specs=[inv_spec(lambda j, nu: jnp.maximum(last(j, nu) - 1, 0)),
                      inv_spec(last),
                      inv_spec(lambda j, nu: jnp.minimum(j + 1, nu[0] - 1)),
                      pl.BlockSpec(memory_space=pl.ANY),
                      pl.BlockSpec((None, D, 2 * D_EXPERT), exp3),
                      pl.BlockSpec((None, 1, 2 * D_EXPERT), exp3),
                      pl.BlockSpec((None, D_EXPERT, D), exp3),
                      pl.BlockSpec((None, 1, D), exp3)],
            out_specs=pl.BlockSpec(memory_space=pl.ANY),
            scratch_shapes=[pltpu.VMEM((EXPERT_TILE * ROW_TILE, V7X_LANES), F32)] * 4 + [
                pltpu.SemaphoreType.DMA((2,)), pltpu.SemaphoreType.DMA((2,))],
        ),
        out_shape=jax.ShapeDtypeStruct(((TOP_K * T + SPARE_ROWS) * ROW_TILE, V7X_LANES), F32),
        compiler_params=_params("arbitrary"),
        name="experts",
    )(blk_exp, n_used, inv3, inv3, inv3, u, w_gu, b_gu, w_dn, b_dn)


def _combine_kernel(y0_ref, y1_ref, y2_ref, y3_ref, gate_ref, h_ref, p_ref, gp_ref, wg_ref, wp_ref, o_ref):
    h = h_ref[...]
    for kk, y_ref in enumerate((y0_ref, y1_ref, y2_ref, y3_ref)):
        h = h + gate_ref[:, kk:kk + 1] * _load_row_tiles(y_ref)
    u = _rms(h, gp_ref[...]).astype(BF16)
    o_ref[...] = h + _sigmoid(_bdot(u, wg_ref[...])) * _bdot(p_ref[...].astype(BF16), wp_ref[...])


def _combine(y4, gate, h, p, g_ple, wg, wp):
    T, D = h.shape
    tm = TOKEN_TILE
    full = lambda a: pl.BlockSpec(a.shape, lambda i: (0,) * a.ndim)
    plane = lambda kk: pl.BlockSpec((tm * ROW_TILE, V7X_LANES), lambda i: (kk * (T // tm) + i, 0))
    return pl.pallas_call(
        _combine_kernel,
        grid=(T // tm,),
        in_specs=[plane(0), plane(1), plane(2), plane(3),
                  pl.BlockSpec((tm, TOP_K), lambda i: (i, 0)),
                  pl.BlockSpec((tm, D), lambda i: (i, 0)),
                  pl.BlockSpec((tm, PLE_DIM), lambda i: (i, 0)),
                  full(g_ple), full(wg), full(wp)],
        out_specs=pl.BlockSpec((tm, D), lambda i: (i, 0)),
        out_shape=jax.ShapeDtypeStruct((T, D), F32),
        compiler_params=_params("arbitrary"),
        name="combine_ple",
    )(y4, y4, y4, y4, gate, h, p, g_ple, wg, wp)


def _rope_tables(seq_len):
    pos = np.arange(seq_len)
    axis_dim = HEAD_DIM // 2
    inv_freq = (ROPE_THETA ** (-np.arange(0, axis_dim, 2, dtype=np.float32) / axis_dim)).astype(np.float32)
    ang_r = (pos // GRID_W).astype(np.float32)[:, None] * inv_freq[None, :]
    ang_c = (pos % GRID_W).astype(np.float32)[:, None] * inv_freq[None, :]
    zero = np.zeros_like(ang_r)
    cos = np.concatenate([np.cos(ang_r)] * 2 + [np.cos(ang_c)] * 2, axis=-1)
    sina = np.concatenate([-np.sin(ang_r), zero, -np.sin(ang_c), zero], axis=-1)
    sinb = np.concatenate([zero, np.sin(ang_r), zero, np.sin(ang_c)], axis=-1)
    return jnp.asarray(cos, F32), jnp.asarray(sina, F32), jnp.asarray(sinb, F32)


def _layer(h, p, g_mix, w_in, q_norm, k_norm, conv_w, conv_b, lru_wa, lru_ba, lru_wi, lru_bi, lru_lam,
           w_attn_br, w_lru_br, w_out, g_moe, w_router, b_router, w_gu, b_gu, w_dn, b_dn,
           g_ple, w_ple_gate, w_ple_proj, rope):
    B, S, D = h.shape
    T = B * S
    assert S % ATTN_Q_TILE == 0 and S % TOKEN_TILE == 0 and S % LRU_STEPS == 0 and B % 8 == 0
    assert T % RANK_TILE == 0 and (T * TOP_K) % INVERT_CHUNK == 0
    row = lambda a: a.reshape(1, -1)
    q, k, v, xr, xg, ga, gr = _in_proj(
        h, row(g_mix), w_in.astype(BF16), row(q_norm) * (HEAD_DIM ** -0.5 * np.log2(np.e)), row(k_norm), *rope)
    attn = _attention(q, k, v)
    xr2 = xr.transpose(1, 0, 2).reshape(S * B, LRU_WIDTH)
    xg2 = xg.transpose(1, 0, 2).reshape(S * B, LRU_WIDTH)
    wcat = (0.5 * jnp.concatenate([lru_wa, lru_wi], axis=-1)).astype(BF16)
    bcat = 0.5 * jnp.concatenate([lru_ba.reshape(2, LRU_BLOCKS, 1, LRU_BLOCK_DIM),
                                  lru_bi.reshape(2, LRU_BLOCKS, 1, LRU_BLOCK_DIM)], axis=-1)
    h_fwd = _lru_scan(xr2, conv_w, row(conv_b), wcat[0], bcat[0], row(lru_lam[0]), batch=B, reverse=False)
    ylru = _lru_scan(xr2, conv_w, row(conv_b), wcat[1], bcat[1], row(lru_lam[1]), batch=B, reverse=True,
                     h_fwd=h_fwd, xg2=xg2).reshape(S, B, LRU_WIDTH).transpose(1, 0, 2)
    w_router_f = jnp.zeros((D, V7X_LANES), F32).at[:, :N_EXPERTS].set(w_router)
    w_router_hi = w_router_f.astype(BF16)
    w_router_p = jnp.stack([w_router_hi, (w_router_f - w_router_hi.astype(F32)).astype(BF16)])
    b_router_p = jnp.zeros((1, V7X_LANES), F32).at[0, :N_EXPERTS].set(b_router)
    h1, u2, idx, gate, cnt = _merge(h, attn, ylru, ga, gr, w_attn_br.astype(BF16), w_lru_br.astype(BF16),
                                    w_out.astype(BF16), row(g_moe), w_router_p, b_router_p)
    cap = T * TOP_K + N_EXPERTS * EXPERT_TILE
    counts = cnt[0, :N_EXPERTS].astype(jnp.int32)
    padded = ((counts + EXPERT_TILE - 1) // EXPERT_TILE) * EXPERT_TILE
    pad_end = jnp.cumsum(padded)
    pad_start = pad_end - padded
    blk_start = jnp.arange(cap // EXPERT_TILE, dtype=jnp.int32) * EXPERT_TILE
    blk_exp = jnp.minimum(jnp.sum(blk_start[:, None] >= pad_end[None, :], axis=1), N_EXPERTS - 1)
    n_used = (pad_end[-1:] // EXPERT_TILE).astype(jnp.int32)
    ps_row = jnp.zeros((1, V7X_LANES), F32).at[0, :N_EXPERTS].set(pad_start.astype(F32))
    dest = _rank(idx.reshape(T, TOP_K), ps_row).reshape(T * TOP_K)
    assert T & (T - 1) == 0, "token count must be a power of two for the slot encoding"
    inv3 = _invert(pad_end.astype(jnp.int32), counts, dest, cap, T).reshape(cap // EXPERT_TILE, 1, EXPERT_TILE)
    y4 = _experts(blk_exp.astype(jnp.int32), n_used, inv3, u2.reshape(T * ROW_TILE, V7X_LANES), w_gu,
                  b_gu.reshape(N_EXPERTS, 1, -1), w_dn, b_dn.reshape(N_EXPERTS, 1, -1))
    out = _combine(y4, gate.reshape(T, TOP_K), h1.reshape(T, D),
                   p.reshape(T, PLE_DIM), row(g_ple), w_ple_gate.astype(BF16), w_ple_proj.astype(BF16))
    return out.reshape(B, S, D)


def kernel(x, p, g_mix, w_in, q_norm, k_norm, conv_w, conv_b, lru_wa, lru_ba, lru_wi, lru_bi, lru_lam,
           w_attn_br, w_lru_br, w_out, g_moe, w_router, b_router, w_gu, b_gu, w_dn, b_dn,
           g_ple, w_ple_gate, w_ple_proj):
    rope = _rope_tables(x.shape[1])
    h = x
    for l in range(p.shape[0]):
        h = _layer(h, p[l], g_mix[l], w_in[l], q_norm[l], k_norm[l], conv_w[l], conv_b[l], lru_wa[l],
                   lru_ba[l], lru_wi[l], lru_bi[l], lru_lam[l], w_attn_br[l], w_lru_br[l], w_out[l],
                   g_moe[l], w_router[l], b_router[l], w_gu[l], b_gu[l], w_dn[l], b_dn[l],
                   g_ple[l], w_ple_gate[l], w_ple_proj[l], rope)
    return h
```
